```python
import math
import jax, jax.numpy as jnp
from jax import lax
import numpy as np

D_MODEL = 1024
BATCH = 32
SEQ = 2048
DEPTH = 1
DEC_BATCH = 2
DEC_SEQ = 8192
PAST_LEN = 128

SSD_WIDTH = D_MODEL
SSD_HEAD_DIM = 64
SSD_HEADS = SSD_WIDTH // SSD_HEAD_DIM
SSD_GROUPS = 2
SSD_HEADS_PER_GROUP = SSD_HEADS // SSD_GROUPS
SSD_STATE = 128
SSD_CHUNK = 128
CONV_WIDTH = 5
CONV_CH = SSD_WIDTH + 2 * SSD_GROUPS * SSD_STATE
GMLP_WIDTH = D_MODEL
GMLP_HEADS = 8
GMLP_HEAD_DIM = GMLP_WIDTH // GMLP_HEADS
GMLP_CHUNK = 128
MIX_WIDTH = SSD_WIDTH + GMLP_WIDTH
IN_COLS = SSD_WIDTH + CONV_CH + 2 * SSD_HEADS + 2 * GMLP_WIDTH
Z_END = SSD_WIDTH
XBC_END = Z_END + CONV_CH
DTF_END = XBC_END + SSD_HEADS
DTB_END = DTF_END + SSD_HEADS
U_END = DTB_END + GMLP_WIDTH
N_EXPERTS = 32
TOP_K = 4
D_FF_EXPERT = D_MODEL
SWIGLU_LIMIT = 7.0
SWIGLU_ALPHA = 1.702
MOE_BLOCK = 256
RMS_EPS = 1e-5
LN_EPS = 1e-5

kernel_name = "hybrid_ssd_gmlp_moe_encoder"


def rmsnorm(x, g):
    xf = x.astype(jnp.float32)
    y = xf * lax.rsqrt(jnp.mean(xf * xf, axis=-1, keepdims=True) + RMS_EPS)
    return (y * g.astype(jnp.float32)).astype(x.dtype)


def layernorm(x, g, b):
    xf = x.astype(jnp.float32)
    mu = jnp.mean(xf, axis=-1, keepdims=True)
    xc = xf - mu
    y = xc * lax.rsqrt(jnp.mean(xc * xc, axis=-1, keepdims=True) + LN_EPS)
    return (y * g.astype(jnp.float32) + b.astype(jnp.float32)).astype(x.dtype)


def centred_dwconv(x, w, b):
    c = x.shape[-1]
    half = CONV_WIDTH // 2
    y = lax.conv_general_dilated(
        x, w[:, None, :].astype(x.dtype), window_strides=(1,),
        padding=[(half, half)], dimension_numbers=("NWC", "WIO", "NWC"),
        feature_group_count=c)
    return y + b.astype(x.dtype)


def ssd_scan(x, dt, a, bm, cm):
    b, l, g, j, p = x.shape
    n = bm.shape[-1]
    nc = l // SSD_CHUNK
    xdt = (x * dt[..., None]).reshape(b, nc, SSD_CHUNK, g, j, p)
    bc = bm.reshape(b, nc, SSD_CHUNK, g, n)
    cc = cm.reshape(b, nc, SSD_CHUNK, g, n)
    adt = (dt * a).reshape(b, nc, SSD_CHUNK, g, j).transpose(0, 3, 4, 1, 2)
    acs = jnp.cumsum(adt, axis=-1)
    causal = jnp.tril(jnp.ones((SSD_CHUNK, SSD_CHUNK), dtype=bool))
    seg = acs[..., :, None] - acs[..., None, :]
    lmat = jnp.exp(jnp.where(causal, seg, -jnp.inf))
    cb = jnp.einsum("bclgn,bcsgn->bgcls", cc, bc)
    y_diag = jnp.einsum("bgcls,bgjcls,bcsgjp->bclgjp", cb, lmat, xdt)
    decay = jnp.exp(acs[..., -1:] - acs)
    states = jnp.einsum("bclgn,bgjcl,bclgjp->cbgjpn", bc, decay, xdt)
    a_last = jnp.moveaxis(acs[..., -1], -1, 0)

    def step(carry, inp):
        s, al = inp
        return carry * jnp.exp(al)[..., None, None] + s, carry

    _, prev = lax.scan(step, jnp.zeros_like(states[0]), (states, a_last))
    y_off = jnp.einsum("bclgn,cbgjpn,bgjcl->bclgjp", cc, prev, jnp.exp(acs))
    return (y_diag + y_off).reshape(b, l, g, j, p)


def ssd_mixer(z, xbc, dtf_raw, dtb_raw, conv_w, conv_b, dt_bias_f, dt_bias_b,
              a_log_f, a_log_b, d_skip, ssd_norm_g):
    b, l, _ = xbc.shape
    xbc = jax.nn.silu(centred_dwconv(xbc, conv_w, conv_b)).astype(jnp.float32)
    xs = xbc[..., :SSD_WIDTH].reshape(b, l, SSD_GROUPS, SSD_HEADS_PER_GROUP, SSD_HEAD_DIM)
    gn = SSD_GROUPS * SSD_STATE
    bm = xbc[..., SSD_WIDTH:SSD_WIDTH + gn].reshape(b, l, SSD_GROUPS, SSD_STATE)
    cm = xbc[..., SSD_WIDTH + gn:].reshape(b, l, SSD_GROUPS, SSD_STATE)
    hshape = (SSD_GROUPS, SSD_HEADS_PER_GROUP)
    dtf = jax.nn.softplus(dtf_raw.astype(jnp.float32) + dt_bias_f.astype(jnp.float32)).reshape(b, l, *hshape)
    dtb = jax.nn.softplus(dtb_raw.astype(jnp.float32) + dt_bias_b.astype(jnp.float32)).reshape(b, l, *hshape)
    a_f = -jnp.exp(a_log_f.astype(jnp.float32)).reshape(hshape)
    a_b = -jnp.exp(a_log_b.astype(jnp.float32)).reshape(hshape)
    y_f = ssd_scan(xs, dtf, a_f, bm, cm)
    fl = lambda t: jnp.flip(t, axis=1)
    y_b = fl(ssd_scan(fl(xs), fl(dtb), a_b, fl(bm), fl(cm)))
    y = y_f + y_b + xs * d_skip.astype(jnp.float32).reshape(hshape)[..., None]
    y = y.reshape(b, l, SSD_WIDTH)
    gated = y * jax.nn.silu(z.astype(jnp.float32))
    return rmsnorm(gated, ssd_norm_g).astype(z.dtype)


def gmlp_mixer(u, v, ln_g, ln_b, w_spatial, b_spatial):
    b, l, _ = u.shape
    u = jax.nn.gelu(u, approximate=False)
    v = layernorm(jax.nn.gelu(v, approximate=False), ln_g, ln_b)
    vc = v.reshape(b, l // GMLP_CHUNK, GMLP_CHUNK, GMLP_HEADS, GMLP_HEAD_DIM)
    sv = jnp.einsum("hts,bcshd->bcthd", w_spatial.astype(v.dtype), vc)
    sv = sv + b_spatial.T[:, :, None].astype(v.dtype)
    return u * sv.reshape(b, l, GMLP_WIDTH)


def moe_ffn(h, router_w, router_b, w_gate_up, b_gate_up, w_down, b_down):
    n, d = h.shape
    logits = (h @ router_w).astype(jnp.float32) + router_b.astype(jnp.float32)
    top_val, top_idx = lax.top_k(logits, TOP_K)
    gates = jax.nn.softmax(top_val, axis=-1)
    nk = n * TOP_K
    flat_e = top_idx.reshape(nk)
    flat_tok = jnp.repeat(jnp.arange(n, dtype=jnp.int32), TOP_K)
    flat_gate = gates.reshape(nk)
    order = jnp.argsort(flat_e)
    se = flat_e[order]
    counts = jnp.bincount(flat_e, length=N_EXPERTS)
    padded = (counts + MOE_BLOCK - 1) // MOE_BLOCK * MOE_BLOCK
    pad_end = jnp.cumsum(padded)
    pad_start = pad_end - padded
    start = jnp.cumsum(counts) - counts
    dest = pad_start[se] + jnp.arange(nk, dtype=jnp.int32) - start[se]
    n_blocks = -(-(nk + N_EXPERTS * (MOE_BLOCK - 1)) // MOE_BLOCK)
    rows = n_blocks * MOE_BLOCK
    row_tok = jnp.zeros((rows,), jnp.int32).at[dest].set(flat_tok[order])
    row_gate = jnp.zeros((rows,), jnp.float32).at[dest].set(flat_gate[order])
    block_e = jnp.minimum(
        jnp.searchsorted(pad_end, jnp.arange(n_blocks, dtype=pad_end.dtype) * MOE_BLOCK, side="right"),
        N_EXPERTS - 1).astype(jnp.int32)

    def expert_block(args):
        tok, e = args
        xb = h[tok]
        gu = xb @ w_gate_up[e] + b_gate_up[e]
        gate = jnp.minimum(gu[:, :D_FF_EXPERT], SWIGLU_LIMIT)
        up = jnp.clip(gu[:, D_FF_EXPERT:], -SWIGLU_LIMIT, SWIGLU_LIMIT)
        act = (up + 1.0) * (gate * jax.nn.sigmoid(gate * SWIGLU_ALPHA))
        return act @ w_down[e] + b_down[e]

    yb = lax.map(expert_block, (row_tok.reshape(n_blocks, MOE_BLOCK), block_e))
    y = yb.reshape(rows, d) * row_gate[:, None].astype(yb.dtype)
    return jax.ops.segment_sum(y, row_tok, num_segments=n)


def encoder_layer(x, norm1_g, w_in, conv_w, conv_b, dt_bias_f, dt_bias_b, a_log_f, a_log_b,
                  d_skip, ssd_norm_g, gmlp_ln_g, gmlp_ln_b, w_spatial, b_spatial, w_out,
                  norm2_g, router_w, router_b, w_gate_up, b_gate_up, w_down, b_down):
    b, l, d = x.shape
    h = rmsnorm(x, norm1_g)
    p = h @ w_in
    y_ssd = ssd_mixer(p[..., :Z_END], p[..., Z_END:XBC_END], p[..., XBC_END:DTF_END],
                      p[..., DTF_END:DTB_END], conv_w, conv_b, dt_bias_f, dt_bias_b,
                      a_log_f, a_log_b, d_skip, ssd_norm_g)
    y_gmlp = gmlp_mixer(p[..., DTB_END:U_END], p[..., U_END:], gmlp_ln_g, gmlp_ln_b,
                        w_spatial, b_spatial)
    x = x + jnp.concatenate([y_ssd, y_gmlp], axis=-1) @ w_out
    h2 = rmsnorm(x, norm2_g).reshape(b * l, d)
    x = x + moe_ffn(h2, router_w, router_b, w_gate_up, b_gate_up, w_down, b_down).reshape(b, l, d)
    return x


def encoder(x, norm1_g, w_in, conv_w, conv_b, dt_bias_f, dt_bias_b, a_log_f, a_log_b,
            d_skip, ssd_norm_g, gmlp_ln_g, gmlp_ln_b, w_spatial, b_spatial, w_out,
            norm2_g, router_w, router_b, w_gate_up, b_gate_up, w_down, b_down, final_g):
    for i in range(DEPTH):
        x = encoder_layer(x, norm1_g[i], w_in[i], conv_w[i], conv_b[i], dt_bias_f[i], dt_bias_b[i],
                          a_log_f[i], a_log_b[i], d_skip[i], ssd_norm_g[i], gmlp_ln_g[i],
                          gmlp_ln_b[i], w_spatial[i], b_spatial[i], w_out[i], norm2_g[i],
                          router_w[i], router_b[i], w_gate_up[i], b_gate_up[i], w_down[i], b_down[i])
    return rmsnorm(x, final_g)


def setup_inputs(seed: int = 0) -> dict:
    key = jax.random.key(seed)
    ks = jax.random.split(key, 32)
    f32 = jnp.float32
    nrm = lambda k, shape, scale: jax.random.normal(k, shape, f32) * scale
    dt0 = jnp.exp(jax.random.uniform(ks[6], (2, DEPTH, SSD_HEADS), f32,
                                     math.log(1e-3), math.log(1e-1)))
    dt_bias = dt0 + jnp.log(-jnp.expm1(-dt0))
    a_log = jnp.log(jax.random.uniform(ks[7], (2, DEPTH, SSD_HEADS), f32, 1.0, 16.0))
    return {
        "x_prompt": nrm(ks[0], (BATCH, SEQ, D_MODEL), 1.0),
        "x_sample": nrm(ks[1], (DEC_BATCH, DEC_SEQ, D_MODEL), 1.0),
        "norm1_g": 1.0 + nrm(ks[2], (DEPTH, D_MODEL), 0.02),
        "w_in": nrm(ks[3], (DEPTH, D_MODEL, IN_COLS), D_MODEL ** -0.5),
        "conv_w": nrm(ks[4], (DEPTH, CONV_WIDTH, CONV_CH), CONV_WIDTH ** -0.5),
        "conv_b": nrm(ks[5], (DEPTH, CONV_CH), 0.02),
        "dt_bias_f": dt_bias[0],
        "dt_bias_b": dt_bias[1],
        "a_log_f": a_log[0],
        "a_log_b": a_log[1],
        "d_skip": 1.0 + nrm(ks[8], (DEPTH, SSD_HEADS), 0.02),
        "ssd_norm_g": 1.0 + nrm(ks[9], (DEPTH, SSD_WIDTH), 0.02),
        "gmlp_ln_g": 1.0 + nrm(ks[10], (DEPTH, GMLP_WIDTH), 0.02),
        "gmlp_ln_b": nrm(ks[11], (DEPTH, GMLP_WIDTH), 0.02),
        "w_spatial": nrm(ks[12], (DEPTH, GMLP_HEADS, GMLP_CHUNK, GMLP_CHUNK), GMLP_CHUNK ** -0.5),
        "b_spatial": 1.0 + nrm(ks[13], (DEPTH, GMLP_HEADS, GMLP_CHUNK), 0.02),
        "w_out": nrm(ks[14], (DEPTH, MIX_WIDTH, D_MODEL), MIX_WIDTH ** -0.5),
        "norm2_g": 1.0 + nrm(ks[15], (DEPTH, D_MODEL), 0.02),
        "router_w": nrm(ks[16], (DEPTH, D_MODEL, N_EXPERTS), D_MODEL ** -0.5),
        "router_b": nrm(ks[17], (DEPTH, N_EXPERTS), 0.01),
        "w_gate_up": nrm(ks[18], (DEPTH, N_EXPERTS, D_MODEL, 2 * D_FF_EXPERT), D_MODEL ** -0.5),
        "b_gate_up": nrm(ks[19], (DEPTH, N_EXPERTS, 2 * D_FF_EXPERT), 0.01),
        "w_down": nrm(ks[20], (DEPTH, N_EXPERTS, D_FF_EXPERT, D_MODEL), D_FF_EXPERT ** -0.5),
        "b_down": nrm(ks[21], (DEPTH, N_EXPERTS, D_MODEL), 0.01),
        "final_g": 1.0 + nrm(ks[22], (D_MODEL,), 0.02),
    }


def reference(x_prompt, x_sample, norm1_g, w_in, conv_w, conv_b, dt_bias_f, dt_bias_b,
              a_log_f, a_log_b, d_skip, ssd_norm_g, gmlp_ln_g, gmlp_ln_b, w_spatial,
              b_spatial, w_out, norm2_g, router_w, router_b, w_gate_up, b_gate_up,
              w_down, b_down, final_g):
    y_prompt = encoder(x_prompt, norm1_g, w_in, conv_w, conv_b, dt_bias_f, dt_bias_b,
                       a_log_f, a_log_b, d_skip, ssd_norm_g, gmlp_ln_g, gmlp_ln_b,
                       w_spatial, b_spatial, w_out, norm2_g, router_w, router_b,
                       w_gate_up, b_gate_up, w_down, b_down, final_g)
    y_sample = encoder(x_sample, norm1_g, w_in, conv_w, conv_b, dt_bias_f, dt_bias_b,
                       a_log_f, a_log_b, d_skip, ssd_norm_g, gmlp_ln_g, gmlp_ln_b,
                       w_spatial, b_spatial, w_out, norm2_g, router_w, router_b,
                       w_gate_up, b_gate_up, w_down, b_down, final_g)
    return (y_prompt, y_sample)
```

```python
import functools

import jax
import jax.numpy as jnp
from jax import lax
from jax.experimental import pallas as pl
from jax.experimental.pallas import tpu as pltpu

F32 = jnp.float32
BF16 = jnp.bfloat16
U32 = jnp.uint32
I32 = jnp.int32

D_MODEL = 1024
SSD_WIDTH = 1024
HEAD_DIM = 64
HEADS = 16
GROUPS = 2
HEADS_PER_GROUP = 8
STATE = 128
CHUNK = 128
CONV_WIDTH = 5
CONV_HALF = 2
CONV_CH = SSD_WIDTH + 2 * GROUPS * STATE
GMLP_WIDTH = 1024
GMLP_HEADS = 8
GMLP_HEAD_DIM = 128
N_EXPERTS = 32
TOP_K = 4
D_FF = 1024
SWIGLU_LIMIT = 7.0
SWIGLU_ALPHA = 1.702
RMS_EPS = 1e-5
LN_EPS = 1e-5

LANES = 128
HALO = 16
TM_IN = 512
TM_OUT = 512
TT_ROWS = 256
BM = 256
HALF = D_MODEL // 2
VMEM_LIMIT = 56 * 1024 * 1024
NEG_BIG = -1e30


def _dot(a, b):
    return jnp.dot(a, b, preferred_element_type=F32)


def _gelu(x):
    return 0.5 * x * (1.0 + lax.erf(x * (2.0 ** -0.5)))


def _pack_bf16_pair(lo, hi):
    lo_bits = lax.bitcast_convert_type(lo.astype(BF16).astype(F32), U32)
    hi_bits = lax.bitcast_convert_type(hi.astype(BF16).astype(F32), U32)
    return (lo_bits >> 16) | hi_bits


def _unpack_bf16_pair(w):
    lo = lax.bitcast_convert_type(w << 16, F32)
    hi = lax.bitcast_convert_type(w & jnp.uint32(0xFFFF0000), F32)
    return lo, hi


def _const_spec(shape):
    nd = len(shape)
    return pl.BlockSpec(shape, lambda *_: (0,) * nd)


def _inproj_kernel(x_ref, xp_ref, xn_ref, g1_ref, wz_ref, wxbc_ref, wdt_ref, wu_ref, wv_ref,
                   cw_ref, cb_ref, dtb_ref, lng_ref, lnb_ref,
                   sz_ref, xbc_ref, dt_ref, u_ref, v_ref, pbuf, *, tiles_per_seq):
    i = pl.program_id(0)
    tm = x_ref.shape[0]
    g1 = g1_ref[...]

    def norm(xv):
        ms = jnp.mean(xv * xv, axis=-1, keepdims=True)
        return (xv * lax.rsqrt(ms + RMS_EPS) * g1).astype(BF16)

    h = norm(x_ref[...])
    h_all = jnp.concatenate([norm(xp_ref[...]), h, norm(xn_ref[...])], axis=0)

    pbuf[...] = _dot(h_all, wxbc_ref[...])

    @pl.when(i % tiles_per_seq == 0)
    def _():
        pbuf[0:HALO, :] = jnp.zeros((HALO, CONV_CH), F32)

    @pl.when(i % tiles_per_seq == tiles_per_seq - 1)
    def _():
        pbuf[HALO + tm:2 * HALO + tm, :] = jnp.zeros((HALO, CONV_CH), F32)

    acc = jnp.broadcast_to(cb_ref[...], (tm, CONV_CH))
    for k in range(CONV_WIDTH):
        acc = acc + cw_ref[k:k + 1, :] * pbuf[pl.ds(HALO - CONV_HALF + k, tm), :]
    xbc_ref[...] = (acc * jax.nn.sigmoid(acc)).astype(BF16)

    z = _dot(h, wz_ref[...])
    sz_ref[...] = (z * jax.nn.sigmoid(z)).astype(BF16)

    dt_ref[...] = jax.nn.softplus(_dot(h, wdt_ref[...]) + dtb_ref[...])

    u_ref[...] = _gelu(_dot(h, wu_ref[...])).astype(BF16)

    v = _gelu(_dot(h, wv_ref[...]))
    mu = jnp.mean(v, axis=-1, keepdims=True)
    vc = v - mu
    var = jnp.mean(vc * vc, axis=-1, keepdims=True)
    v_ref[...] = (vc * lax.rsqrt(var + LN_EPS) * lng_ref[...] + lnb_ref[...]).astype(BF16)


def _inproj(x2d, seq_len, p):
    t = x2d.shape[0]
    tm = TM_IN
    tiles_per_seq = seq_len // tm
    nh = tm // HALO
    last_halo = t // HALO - 1
    kern = functools.partial(_inproj_kernel, tiles_per_seq=tiles_per_seq)
    row = lambda i: (i, 0)
    out_shape = (
        jax.ShapeDtypeStruct((t, SSD_WIDTH), BF16),
        jax.ShapeDtypeStruct((t, CONV_CH), BF16),
        jax.ShapeDtypeStruct((t, LANES), F32),
        jax.ShapeDtypeStruct((t, GMLP_WIDTH), BF16),
        jax.ShapeDtypeStruct((t, GMLP_WIDTH), BF16),
    )
    return pl.pallas_call(
        kern,
        out_shape=out_shape,
        grid=(t // tm,),
        in_specs=[
            pl.BlockSpec((tm, D_MODEL), row),
            pl.BlockSpec((HALO, D_MODEL), lambda i: (jnp.maximum(i * nh - 1, 0), 0)),
            pl.BlockSpec((HALO, D_MODEL), lambda i: (jnp.minimum((i + 1) * nh, last_halo), 0)),
            _const_spec((1, D_MODEL)),
            _const_spec((D_MODEL, SSD_WIDTH)),
            _const_spec((D_MODEL, CONV_CH)),
            _const_spec((D_MODEL, LANES)),
            _const_spec((D_MODEL, GMLP_WIDTH)),
            _const_spec((D_MODEL, GMLP_WIDTH)),
            _const_spec((8, CONV_CH)),
            _const_spec((1, CONV_CH)),
            _const_spec((1, LANES)),
            _const_spec((1, GMLP_WIDTH)),
            _const_spec((1, GMLP_WIDTH)),
        ],
        out_specs=(
            pl.BlockSpec((tm, SSD_WIDTH), row),
            pl.BlockSpec((tm, CONV_CH), row),
            pl.BlockSpec((tm, LANES), row),
            pl.BlockSpec((tm, GMLP_WIDTH), row),
            pl.BlockSpec((tm, GMLP_WIDTH), row),
        ),
        scratch_shapes=[pltpu.VMEM((tm + 2 * HALO, CONV_CH), F32)],
        compiler_params=pltpu.CompilerParams(
            dimension_semantics=("arbitrary",), vmem_limit_bytes=VMEM_LIMIT),
        name="inproj",
    )(x2d, x2d, x2d, p["g1"], p["w_z"], p["w_xbc"], p["w_dt"], p["w_u"], p["w_v"],
      p["conv_w"], p["conv_b"], p["dt_bias"], p["ln_g"], p["ln_b"])


def _split2(q):
    hi = q.astype(BF16)
    lo = (q - hi.astype(F32)).astype(BF16)
    return hi, lo


def _ssd_role(xbc_ref, dt_ref, arow, tri_ref, rexp_ref, dskip_ref, s_ref, y_ref, *, backward):
    lane0 = HEADS if backward else 0
    dt_all = dt_ref[...]
    adt = dt_all * arow
    a1 = adt.astype(BF16)
    r1 = adt - a1.astype(F32)
    a2 = r1.astype(BF16)
    a3 = (r1 - a2.astype(F32)).astype(BF16)
    tri = tri_ref[...]
    acs = _dot(tri, a1) + _dot(tri, a2) + _dot(tri, a3)
    edge = 0 if backward else CHUNK - 1
    tot = acs[edge:edge + 1, :]
    decay = jnp.exp(tot - acs)
    eacs = jnp.exp(acs)
    acs_t = acs.T
    rexp = rexp_ref[...]

    def expand(q):
        hi, lo = _split2(q)
        return _dot(hi, rexp) + _dot(lo, rexp)

    dt_e = expand(dt_all)
    decay_e = expand(decay)
    eacs_e = expand(eacs)
    etot_e = eacs_e[edge:edge + 1, :]

    xf = xbc_ref[:, 0:SSD_WIDTH].astype(F32)
    xdt = xf * dt_e
    xdd = (xdt * decay_e).astype(BF16)
    xdt_b = xdt.astype(BF16)

    li = lax.broadcasted_iota(I32, (CHUNK, CHUNK), 0)
    si = lax.broadcasted_iota(I32, (CHUNK, CHUNK), 1)
    keep = (si >= li) if backward else (si <= li)
    left = si < HEAD_DIM
    zero_b = jnp.zeros((CHUNK, LANES), BF16)

    for g in range(GROUPS):
        b_g = xbc_ref[:, SSD_WIDTH + g * STATE:SSD_WIDTH + (g + 1) * STATE]
        c_g = xbc_ref[:, SSD_WIDTH + (GROUPS + g) * STATE:SSD_WIDTH + (GROUPS + g + 1) * STATE]
        b_t = b_g.astype(F32).T.astype(BF16)
        cb = _dot(c_g, b_t)
        gc0 = g * HEADS_PER_GROUP * HEAD_DIM
        gw = HEADS_PER_GROUP * HEAD_DIM
        s_old = s_ref[:, gc0:gc0 + gw]
        y_off = _dot(c_g, s_old.astype(BF16))
        s_new = _dot(b_t, xdd[:, gc0:gc0 + gw])
        s_ref[:, gc0:gc0 + gw] = s_old * etot_e[:, gc0:gc0 + gw] + s_new
        for pp in range(HEADS_PER_GROUP // 2):
            c0 = gc0 + pp * LANES
            ms = []
            for q in range(2):
                hl = lane0 + g * HEADS_PER_GROUP + 2 * pp + q
                seg = acs[:, hl:hl + 1] - acs_t[hl:hl + 1, :]
                lmat = jnp.exp(jnp.where(keep, seg, -jnp.inf))
                ms.append((cb * lmat).astype(BF16))
            lhs = jnp.concatenate(ms, axis=1)
            xp = xdt_b[:, c0:c0 + LANES]
            rhs = jnp.concatenate([jnp.where(left, xp, zero_b), jnp.where(left, zero_b, xp)], axis=0)
            y = _dot(lhs, rhs) + y_off[:, pp * LANES:(pp + 1) * LANES] * eacs_e[:, c0:c0 + LANES]
            if not backward:
                y = y + xf[:, c0:c0 + LANES] * dskip_ref[:, c0:c0 + LANES]
            y_ref[:, c0:c0 + LANES] = y.astype(BF16)


def _ssd_kernel(xf_ref, dtf_ref, xb_ref, dtb_ref, alog_ref, trilo_ref, triup_ref, rf_ref, rb_ref,
                dskip_ref, yf_ref, yb_ref, sf_ref, sb_ref):
    @pl.when(pl.program_id(1) == 0)
    def _():
        sf_ref[...] = jnp.zeros_like(sf_ref)
        sb_ref[...] = jnp.zeros_like(sb_ref)

    arow = -jnp.exp(alog_ref[...])
    _ssd_role(xf_ref, dtf_ref, arow, trilo_ref, rf_ref, dskip_ref, sf_ref, yf_ref, backward=False)
    _ssd_role(xb_ref, dtb_ref, arow, triup_ref, rb_ref, dskip_ref, sb_ref, yb_ref, backward=True)


def _ssd(xbc, dt, batch, seq_len, p):
    t = xbc.shape[0]
    nc = seq_len // CHUNK
    fwd = lambda b, c: (b * nc + c, 0)
    bwd = lambda b, c: (b * nc + nc - 1 - c, 0)
    return pl.pallas_call(
        _ssd_kernel,
        out_shape=(jax.ShapeDtypeStruct((t, SSD_WIDTH), BF16), jax.ShapeDtypeStruct((t, SSD_WIDTH), BF16)),
        grid=(batch, nc),
        in_specs=[
            pl.BlockSpec((CHUNK, CONV_CH), fwd),
            pl.BlockSpec((CHUNK, LANES), fwd),
            pl.BlockSpec((CHUNK, CONV_CH), bwd),
            pl.BlockSpec((CHUNK, LANES), bwd),
            _const_spec((1, LANES)),
            _const_spec((CHUNK, CHUNK)),
            _const_spec((CHUNK, CHUNK)),
            _const_spec((LANES, SSD_WIDTH)),
            _const_spec((LANES, SSD_WIDTH)),
            _const_spec((1, SSD_WIDTH)),
        ],
        out_specs=(pl.BlockSpec((CHUNK, SSD_WIDTH), fwd), pl.BlockSpec((CHUNK, SSD_WIDTH), bwd)),
        scratch_shapes=[pltpu.VMEM((STATE, SSD_WIDTH), F32), pltpu.VMEM((STATE, SSD_WIDTH), F32)],
        compiler_params=pltpu.CompilerParams(
            dimension_semantics=("arbitrary", "arbitrary"), vmem_limit_bytes=VMEM_LIMIT),
        name="ssd",
    )(xbc, dt, xbc, dt, p["a_log"], p["tri_lo"], p["tri_up"], p["rexp_f"], p["rexp_b"], p["d_skip"])


def _outproj_kernel(yf_ref, yb_ref, sz_ref, u_ref, v_ref, x_ref, gs_ref, wsp_ref, bsp_ref, wo_ref,
                    g2_ref, wr_ref, rb_ref, tri_ref,
                    x1_ref, h2w_ref, me_ref, mp_ref, mg_ref, cnt_ref, gm_buf, carry):
    i = pl.program_id(0)
    tm = x_ref.shape[0]

    @pl.when(i == 0)
    def _():
        carry[...] = jnp.zeros_like(carry)

    gated = (yf_ref[...].astype(F32) + yb_ref[...].astype(F32)) * sz_ref[...].astype(F32)
    ms = jnp.mean(gated * gated, axis=-1, keepdims=True)
    ssd = (gated * lax.rsqrt(ms + RMS_EPS) * gs_ref[...]).astype(BF16)

    for c in range(tm // CHUNK):
        r0 = c * CHUNK
        for hh in range(GMLP_HEADS):
            c0 = hh * GMLP_HEAD_DIM
            sv = _dot(wsp_ref[hh], v_ref[r0:r0 + CHUNK, c0:c0 + GMLP_HEAD_DIM]) + bsp_ref[hh]
            gm_buf[r0:r0 + CHUNK, c0:c0 + GMLP_HEAD_DIM] = (
                u_ref[r0:r0 + CHUNK, c0:c0 + GMLP_HEAD_DIM].astype(F32) * sv).astype(BF16)

    x1 = x_ref[...] + _dot(ssd, wo_ref[0:SSD_WIDTH, :]) + _dot(gm_buf[...], wo_ref[SSD_WIDTH:, :])
    x1_ref[...] = x1

    ms2 = jnp.mean(x1 * x1, axis=-1, keepdims=True)
    h2 = (x1 * lax.rsqrt(ms2 + RMS_EPS) * g2_ref[...]).astype(BF16)
    h2f = h2.astype(F32)
    h2w_ref[...] = _pack_bf16_pair(h2f[:, 0:HALF], h2f[:, HALF:])

    logits = _dot(h2, wr_ref[...]) + rb_ref[...]
    lane = lax.broadcasted_iota(I32, (tm, LANES), 1).astype(F32)
    work = logits
    sel_all = jnp.zeros((tm, LANES), F32)
    idxs, vals = [], []
    for _ in range(TOP_K):
        m = jnp.max(work, axis=-1, keepdims=True)
        idx = jnp.min(jnp.where(work == m, lane, float(LANES)), axis=-1, keepdims=True)
        sel = lane == idx
        sel_all = jnp.where(sel, 1.0, sel_all)
        work = jnp.where(sel, -jnp.inf, work)
        idxs.append(idx)
        vals.append(m)
    exps = [jnp.exp(v - vals[0]) for v in vals]
    denom = exps[0] + exps[1] + exps[2] + exps[3]

    pos = _dot(tri_ref[...], sel_all.astype(BF16)) + carry[0:1, :]
    new_cnt = carry[0:1, :] + jnp.sum(sel_all, axis=0, keepdims=True)
    carry[...] = jnp.broadcast_to(new_cnt, carry.shape)
    cnt_ref[...] = jnp.broadcast_to(new_cnt, cnt_ref.shape)

    me = jnp.zeros((tm, LANES), I32)
    mp = jnp.zeros((tm, LANES), I32)
    mg = jnp.zeros((tm, LANES), F32)
    for k in range(TOP_K):
        pk = jnp.sum(jnp.where(lane == idxs[k], pos, 0.0), axis=-1, keepdims=True)
        me = jnp.where(lane == k, idxs[k].astype(I32), me)
        mp = jnp.where(lane == k, pk.astype(I32), mp)
        mg = jnp.where(lane == k, exps[k] / denom, mg)
    me_ref[...] = me
    mp_ref[...] = mp
    mg_ref[...] = mg


def _outproj(yf, yb, sz, u, v, x2d, p):
    t = x2d.shape[0]
    tm = TM_OUT
    row = lambda i: (i, 0)
    out_shape = (
        jax.ShapeDtypeStruct((t, D_MODEL), F32),
        jax.ShapeDtypeStruct((t, HALF), U32),
        jax.ShapeDtypeStruct((t, LANES), I32),
        jax.ShapeDtypeStruct((t, LANES), I32),
        jax.ShapeDtypeStruct((t, LANES), F32),
        jax.ShapeDtypeStruct((8, LANES), F32),
    )
    bf_row = pl.BlockSpec((tm, D_MODEL), row)
    return pl.pallas_call(
        _outproj_kernel,
        out_shape=out_shape,
        grid=(t // tm,),
        in_specs=[
            bf_row, bf_row, bf_row, bf_row, bf_row, bf_row,
            _const_spec((1, SSD_WIDTH)),
            _const_spec((GMLP_HEADS, CHUNK, CHUNK)),
            _const_spec((GMLP_HEADS, CHUNK, GMLP_HEAD_DIM)),
            _const_spec((SSD_WIDTH + GMLP_WIDTH, D_MODEL)),
            _const_spec((1, D_MODEL)),
            _const_spec((D_MODEL, LANES)),
            _const_spec((1, LANES)),
            _const_spec((tm, tm)),
        ],
        out_specs=(
            pl.BlockSpec((tm, D_MODEL), row),
            pl.BlockSpec((tm, HALF), row),
            pl.BlockSpec((tm, LANES), row),
            pl.BlockSpec((tm, LANES), row),
            pl.BlockSpec((tm, LANES), row),
            _const_spec((8, LANES)),
        ),
        scratch_shapes=[pltpu.VMEM((tm, GMLP_WIDTH), BF16), pltpu.VMEM((8, LANES), F32)],
        compiler_params=pltpu.CompilerParams(
            dimension_semantics=("arbitrary",), vmem_limit_bytes=VMEM_LIMIT),
        name="outproj",
    )(yf, yb, sz, u, v, x2d, p["gs"], p["w_sp"], p["b_sp"], p["w_out"], p["g2"], p["w_r"], p["b_r"],
      p["tri_strict"])


def _dispatch_kernel(ps_ref, e_ref, p_ref, h_ref, xs_in_ref, xs_ref, sem):
    del xs_in_ref
    tt = h_ref.shape[0]

    def row_copy(src_row, dst_row):
        return pltpu.make_async_copy(h_ref.at[pl.ds(src_row, 1)], xs_ref.at[pl.ds(dst_row, 1)], sem)

    def issue(i, c):
        for k in range(TOP_K):
            j = i * TOP_K + k
            row_copy(i, ps_ref[e_ref[j]] + p_ref[j]).start()
        return c

    lax.fori_loop(0, tt, issue, 0)

    def drain(i, c):
        for k in range(TOP_K):
            row_copy(0, 0).wait()
        return c

    lax.fori_loop(0, tt, drain, 0)


def _dispatch(pad_start, e4, p4, h2w, rows):
    t = h2w.shape[0]
    tt = TT_ROWS
    xs0 = jnp.zeros((rows, HALF), U32)
    grid_spec = pltpu.PrefetchScalarGridSpec(
        num_scalar_prefetch=1,
        grid=(t // tt,),
        in_specs=[
            pl.BlockSpec((tt * TOP_K,), lambda i, ps: (i,), memory_space=pltpu.SMEM),
            pl.BlockSpec((tt * TOP_K,), lambda i, ps: (i,), memory_space=pltpu.SMEM),
            pl.BlockSpec((tt, HALF), lambda i, ps: (i, 0)),
            pl.BlockSpec(memory_space=pl.ANY),
        ],
        out_specs=pl.BlockSpec(memory_space=pl.ANY),
        scratch_shapes=[pltpu.SemaphoreType.DMA(())],
    )
    return pl.pallas_call(
        _dispatch_kernel,
        out_shape=jax.ShapeDtypeStruct((rows, HALF), U32),
        grid_spec=grid_spec,
        input_output_aliases={4: 0},
        compiler_params=pltpu.CompilerParams(
            dimension_semantics=("arbitrary",), has_side_effects=True),
        name="dispatch",
    )(pad_start, e4, p4, h2w, xs0)


def _experts_kernel(be_ref, nu_ref, xs_ref, wgu_ref, bgu_ref, wd_ref, bd_ref, ys_ref):
    i = pl.program_id(0)

    @pl.when(i < nu_ref[0])
    def _():
        lo, hi = _unpack_bf16_pair(xs_ref[...])
        xb = jnp.concatenate([lo.astype(BF16), hi.astype(BF16)], axis=1)
        gu = _dot(xb, wgu_ref[0]) + bgu_ref[0]
        gate = jnp.minimum(gu[:, 0:D_FF], SWIGLU_LIMIT)
        up = jnp.clip(gu[:, D_FF:], -SWIGLU_LIMIT, SWIGLU_LIMIT)
        act = (up + 1.0) * (gate * jax.nn.sigmoid(gate * SWIGLU_ALPHA))
        y = _dot(act.astype(BF16), wd_ref[0]) + bd_ref[0]
        ys_ref[...] = _pack_bf16_pair(y[:, 0:HALF], y[:, HALF:])

    @pl.when(i >= nu_ref[0])
    def _():
        ys_ref[...] = jnp.zeros_like(ys_ref)


def _experts(block_e, n_used, xs, p):
    rows = xs.shape[0]
    nb = rows // BM
    grid_spec = pltpu.PrefetchScalarGridSpec(
        num_scalar_prefetch=2,
        grid=(nb,),
        in_specs=[
            pl.BlockSpec((BM, HALF), lambda i, be, nu: (i, 0)),
            pl.BlockSpec((1, D_MODEL, 2 * D_FF), lambda i, be, nu: (be[i], 0, 0)),
            pl.BlockSpec((1, 1, 2 * D_FF), lambda i, be, nu: (be[i], 0, 0)),
            pl.BlockSpec((1, D_FF, D_MODEL), lambda i, be, nu: (be[i], 0, 0)),
            pl.BlockSpec((1, 1, D_MODEL), lambda i, be, nu: (be[i], 0, 0)),
        ],
        out_specs=pl.BlockSpec((BM, HALF), lambda i, be, nu: (i, 0)),
    )
    return pl.pallas_call(
        _experts_kernel,
        out_shape=jax.ShapeDtypeStruct((rows, HALF), U32),
        grid_spec=grid_spec,
        compiler_params=pltpu.CompilerParams(
            dimension_semantics=("arbitrary",), vmem_limit_bytes=VMEM_LIMIT),
        name="experts",
    )(block_e, n_used, xs, p["w_gu"], p["b_gu"], p["w_d"], p["b_d"])


def _combine_kernel(ps_ref, e_ref, p_ref, mg_ref, x1_ref, gf_ref, ys_ref, o_ref, buf, sem):
    tt = x1_ref.shape[0]

    def row_copy(src_row, k, i):
        return pltpu.make_async_copy(ys_ref.at[pl.ds(src_row, 1)], buf.at[k, pl.ds(i, 1)], sem)

    def issue(i, c):
        for k in range(TOP_K):
            j = i * TOP_K + k
            row_copy(ps_ref[e_ref[j]] + p_ref[j], k, i).start()
        return c

    lax.fori_loop(0, tt, issue, 0)

    def drain(i, c):
        for k in range(TOP_K):
            row_copy(0, k, 0).wait()
        return c

    lax.fori_loop(0, tt, drain, 0)

    mg = mg_ref[...]
    acc = x1_ref[...]
    for k in range(TOP_K):
        lo, hi = _unpack_bf16_pair(buf[k])
        acc = acc + mg[:, k:k + 1] * jnp.concatenate([lo, hi], axis=1)
    ms = jnp.mean(acc * acc, axis=-1, keepdims=True)
    o_ref[...] = acc * lax.rsqrt(ms + RMS_EPS) * gf_ref[...]


def _combine(pad_start, e4, p4, mg, x1, ys, p):
    t = x1.shape[0]
    tt = TT_ROWS
    grid_spec = pltpu.PrefetchScalarGridSpec(
        num_scalar_prefetch=1,
        grid=(t // tt,),
        in_specs=[
            pl.BlockSpec((tt * TOP_K,), lambda i, ps: (i,), memory_space=pltpu.SMEM),
            pl.BlockSpec((tt * TOP_K,), lambda i, ps: (i,), memory_space=pltpu.SMEM),
            pl.BlockSpec((tt, LANES), lambda i, ps: (i, 0)),
            pl.BlockSpec((tt, D_MODEL), lambda i, ps: (i, 0)),
            pl.BlockSpec((1, D_MODEL), lambda i, ps: (0, 0)),
            pl.BlockSpec(memory_space=pl.ANY),
        ],
        out_specs=pl.BlockSpec((tt, D_MODEL), lambda i, ps: (i, 0)),
        scratch_shapes=[pltpu.VMEM((TOP_K, tt, HALF), U32), pltpu.SemaphoreType.DMA(())],
    )
    return pl.pallas_call(
        _combine_kernel,
        out_shape=jax.ShapeDtypeStruct((t, D_MODEL), F32),
        grid_spec=grid_spec,
        compiler_params=pltpu.CompilerParams(
            dimension_semantics=("arbitrary",), vmem_limit_bytes=VMEM_LIMIT),
        name="combine",
    )(pad_start, e4, p4, mg, x1, p["gf"], ys)


def _prepare_params(norm1_g, w_in, conv_w, conv_b, dt_bias_f, dt_bias_b, a_log_f, a_log_b, d_skip,
                    ssd_norm_g, gmlp_ln_g, gmlp_ln_b, w_spatial, b_spatial, w_out, norm2_g,
                    router_w, router_b, w_gate_up, b_gate_up, w_down, b_down, final_g):
    z_end = SSD_WIDTH
    xbc_end = z_end + CONV_CH
    dt_end = xbc_end + 2 * HEADS
    u_end = dt_end + GMLP_WIDTH
    w = w_in[0]
    pad_lanes = lambda a, fill=0.0: jnp.pad(a, ((0, 0), (0, LANES - a.shape[1])), constant_values=fill)
    head_lane = jnp.arange(LANES)[:, None]
    chan_head = (jnp.arange(SSD_WIDTH) // HEAD_DIM)[None, :]
    ar = jnp.arange(CHUNK)
    art = jnp.arange(TM_OUT)
    return {
        "g1": norm1_g[0][None, :],
        "w_z": w[:, :z_end].astype(BF16),
        "w_xbc": w[:, z_end:xbc_end].astype(BF16),
        "w_dt": pad_lanes(w[:, xbc_end:dt_end]).astype(BF16),
        "w_u": w[:, dt_end:u_end].astype(BF16),
        "w_v": w[:, u_end:].astype(BF16),
        "conv_w": jnp.pad(conv_w[0], ((0, 8 - CONV_WIDTH), (0, 0))),
        "conv_b": conv_b[0][None, :],
        "dt_bias": pad_lanes(jnp.concatenate([dt_bias_f[0], dt_bias_b[0]])[None, :]),
        "ln_g": gmlp_ln_g[0][None, :],
        "ln_b": gmlp_ln_b[0][None, :],
        "a_log": pad_lanes(jnp.concatenate([a_log_f[0], a_log_b[0]])[None, :]),
        "tri_lo": (ar[:, None] >= ar[None, :]).astype(BF16),
        "tri_up": (ar[:, None] <= ar[None, :]).astype(BF16),
        "rexp_f": (head_lane == chan_head).astype(BF16),
        "rexp_b": (head_lane == chan_head + HEADS).astype(BF16),
        "d_skip": jnp.repeat(d_skip[0], HEAD_DIM)[None, :],
        "gs": ssd_norm_g[0][None, :],
        "w_sp": w_spatial[0].astype(BF16),
        "b_sp": jnp.broadcast_to(b_spatial[0][:, :, None], (GMLP_HEADS, CHUNK, GMLP_HEAD_DIM)),
        "w_out": w_out[0].astype(BF16),
        "g2": norm2_g[0][None, :],
        "w_r": pad_lanes(router_w[0]).astype(BF16),
        "b_r": pad_lanes(router_b[0][None, :], NEG_BIG),
        "tri_strict": (art[:, None] > art[None, :]).astype(BF16),
        "w_gu": w_gate_up[0].astype(BF16),
        "b_gu": b_gate_up[0][:, None, :],
        "w_d": w_down[0].astype(BF16),
        "b_d": b_down[0][:, None, :],
        "gf": final_g[None, :],
    }


def _encoder(x, p):
    batch, seq_len, _ = x.shape
    t = batch * seq_len
    x2d = x.reshape(t, D_MODEL)
    sz, xbc, dt, u, v = _inproj(x2d, seq_len, p)
    yf, yb = _ssd(xbc, dt, batch, seq_len, p)
    x1, h2w, me, mp, mg, cnt = _outproj(yf, yb, sz, u, v, x2d, p)

    counts = cnt[0, :N_EXPERTS].astype(I32)
    padded = (counts + BM - 1) // BM * BM
    pad_end = jnp.cumsum(padded)
    pad_start = (pad_end - padded).astype(I32)
    n_blocks = -(-(t * TOP_K + N_EXPERTS * (BM - 1)) // BM)
    n_used = (pad_end[-1] // BM).astype(I32)
    blk = jnp.minimum(jnp.arange(n_blocks, dtype=I32), n_used - 1) * BM
    block_e = jnp.minimum(jnp.searchsorted(pad_end, blk, side="right"), N_EXPERTS - 1).astype(I32)
    e4 = me[:, :TOP_K].reshape(t * TOP_K)
    p4 = mp[:, :TOP_K].reshape(t * TOP_K)

    xs = _dispatch(pad_start, e4, p4, h2w, n_blocks * BM)
    ys = _experts(block_e, n_used.reshape(1), xs, p)
    y = _combine(pad_start, e4, p4, mg, x1, ys, p)
    return y.reshape(batch, seq_len, D_MODEL)


def kernel(x_prompt, x_sample, norm1_g, w_in, conv_w, conv_b, dt_bias_f, dt_bias_b, a_log_f, a_log_b,
           d_skip, ssd_norm_g, gmlp_ln_g, gmlp_ln_b, w_spatial, b_spatial, w_out, norm2_g, router_w,
           router_b, w_gate_up, b_gate_up, w_down, b_down, final_g):
    p = _prepare_params(norm1_g, w_in, conv_w, conv_b, dt_bias_f, dt_bias_b, a_log_f, a_log_b, d_skip,
                        ssd_norm_g, gmlp_ln_g, gmlp_ln_b, w_spatial, b_spatial, w_out, norm2_g,
                        router_w, router_b, w_gate_up, b_gate_up, w_down, b_down, final_g)
    return (_encoder(x_prompt, p), _encoder(x_sample, p))
```

```python
import functools

import jax
import jax.numpy as jnp
from jax import lax
from jax.experimental import pallas as pl
from jax.experimental.pallas import tpu as pltpu

F32 = jnp.float32
BF16 = jnp.bfloat16
U32 = jnp.uint32
I32 = jnp.int32

D_MODEL = 1024
SSD_WIDTH = 1024
HEAD_DIM = 64
HEADS = 16
GROUPS = 2
HEADS_PER_GROUP = 8
STATE = 128
CHUNK = 128
CONV_WIDTH = 5
CONV_HALF = 2
CONV_CH = SSD_WIDTH + 2 * GROUPS * STATE
GMLP_WIDTH = 1024
GMLP_HEADS = 8
GMLP_HEAD_DIM = 128
N_EXPERTS = 32
TOP_K = 4
D_FF = 1024
SWIGLU_LIMIT = 7.0
SWIGLU_ALPHA = 1.702
RMS_EPS = 1e-5
LN_EPS = 1e-5

LANES = 128
HALO = 16
TM_IN = 512
TM_OUT = 512
BM = 256
SEG = 8
COPY_ROWS = (32, 16, 8)
SORT_BLK = 512
R_LOCAL = -(-(TM_OUT * TOP_K + N_EXPERTS * (SEG - 1)) // SORT_BLK) * SORT_BLK
HALF = D_MODEL // 2
VMEM_LIMIT = 56 * 1024 * 1024
NEG_BIG = -1e30


def _dot(a, b):
    return jnp.dot(a, b, preferred_element_type=F32)


def _gelu(x):
    return 0.5 * x * (1.0 + lax.erf(x * (2.0 ** -0.5)))


def _pack_bf16_pair(lo, hi):
    lo_bits = lax.bitcast_convert_type(lo.astype(BF16).astype(F32), U32)
    hi_bits = lax.bitcast_convert_type(hi.astype(BF16).astype(F32), U32)
    return (lo_bits >> 16) | hi_bits


def _unpack_bf16_pair(w):
    lo = lax.bitcast_convert_type(w << 16, F32)
    hi = lax.bitcast_convert_type(w & jnp.uint32(0xFFFF0000), F32)
    return lo, hi


def _const_spec(shape):
    nd = len(shape)
    return pl.BlockSpec(shape, lambda *_: (0,) * nd)


def _inproj_kernel(x_ref, xp_ref, xn_ref, g1_ref, wz_ref, wxbc_ref, wdt_ref, wu_ref, wv_ref,
                   cw_ref, cb_ref, dtb_ref, lng_ref, lnb_ref,
                   sz_ref, xbc_ref, dt_ref, u_ref, v_ref, pbuf, *, tiles_per_seq):
    i = pl.program_id(0)
    tm = x_ref.shape[0]
    g1 = g1_ref[...]

    def norm(xv):
        ms = jnp.mean(xv * xv, axis=-1, keepdims=True)
        return (xv * lax.rsqrt(ms + RMS_EPS) * g1).astype(BF16)

    h = norm(x_ref[...])
    h_all = jnp.concatenate([norm(xp_ref[...]), h, norm(xn_ref[...])], axis=0)

    pbuf[...] = _dot(h_all, wxbc_ref[...])

    @pl.when(i % tiles_per_seq == 0)
    def _():
        pbuf[0:HALO, :] = jnp.zeros((HALO, CONV_CH), F32)

    @pl.when(i % tiles_per_seq == tiles_per_seq - 1)
    def _():
        pbuf[HALO + tm:2 * HALO + tm, :] = jnp.zeros((HALO, CONV_CH), F32)

    acc = jnp.broadcast_to(cb_ref[...], (tm, CONV_CH))
    for k in range(CONV_WIDTH):
        acc = acc + cw_ref[k:k + 1, :] * pbuf[pl.ds(HALO - CONV_HALF + k, tm), :]
    xbc_ref[...] = (acc * jax.nn.sigmoid(acc)).astype(BF16)

    z = _dot(h, wz_ref[...])
    sz_ref[...] = (z * jax.nn.sigmoid(z)).astype(BF16)

    dt_ref[...] = jax.nn.softplus(_dot(h, wdt_ref[...]) + dtb_ref[...])

    u_ref[...] = _gelu(_dot(h, wu_ref[...])).astype(BF16)

    v = _gelu(_dot(h, wv_ref[...]))
    mu = jnp.mean(v, axis=-1, keepdims=True)
    vc = v - mu
    var = jnp.mean(vc * vc, axis=-1, keepdims=True)
    v_ref[...] = (vc * lax.rsqrt(var + LN_EPS) * lng_ref[...] + lnb_ref[...]).astype(BF16)


def _inproj(x2d, seq_len, p):
    t = x2d.shape[0]
    tm = TM_IN
    tiles_per_seq = seq_len // tm
    nh = tm // HALO
    last_halo = t // HALO - 1
    kern = functools.partial(_inproj_kernel, tiles_per_seq=tiles_per_seq)
    row = lambda i: (i, 0)
    out_shape = (
        jax.ShapeDtypeStruct((t, SSD_WIDTH), BF16),
        jax.ShapeDtypeStruct((t, CONV_CH), BF16),
        jax.ShapeDtypeStruct((t, LANES), F32),
        jax.ShapeDtypeStruct((t, GMLP_WIDTH), BF16),
        jax.ShapeDtypeStruct((t, GMLP_WIDTH), BF16),
    )
    return pl.pallas_call(
        kern,
        out_shape=out_shape,
        grid=(t // tm,),
        in_specs=[
            pl.BlockSpec((tm, D_MODEL), row),
            pl.BlockSpec((HALO, D_MODEL), lambda i: (jnp.maximum(i * nh - 1, 0), 0)),
            pl.BlockSpec((HALO, D_MODEL), lambda i: (jnp.minimum((i + 1) * nh, last_halo), 0)),
            _const_spec((1, D_MODEL)),
            _const_spec((D_MODEL, SSD_WIDTH)),
            _const_spec((D_MODEL, CONV_CH)),
            _const_spec((D_MODEL, LANES)),
            _const_spec((D_MODEL, GMLP_WIDTH)),
            _const_spec((D_MODEL, GMLP_WIDTH)),
            _const_spec((8, CONV_CH)),
            _const_spec((1, CONV_CH)),
            _const_spec((1, LANES)),
            _const_spec((1, GMLP_WIDTH)),
            _const_spec((1, GMLP_WIDTH)),
        ],
        out_specs=(
            pl.BlockSpec((tm, SSD_WIDTH), row),
            pl.BlockSpec((tm, CONV_CH), row),
            pl.BlockSpec((tm, LANES), row),
            pl.BlockSpec((tm, GMLP_WIDTH), row),
            pl.BlockSpec((tm, GMLP_WIDTH), row),
        ),
        scratch_shapes=[pltpu.VMEM((tm + 2 * HALO, CONV_CH), F32)],
        compiler_params=pltpu.CompilerParams(
            dimension_semantics=("arbitrary",), vmem_limit_bytes=VMEM_LIMIT),
        name="inproj",
    )(x2d, x2d, x2d, p["g1"], p["w_z"], p["w_xbc"], p["w_dt"], p["w_u"], p["w_v"],
      p["conv_w"], p["conv_b"], p["dt_bias"], p["ln_g"], p["ln_b"])


def _split2(q):
    hi = q.astype(BF16)
    lo = (q - hi.astype(F32)).astype(BF16)
    return hi, lo


def _ssd_role(xbc_ref, dt_ref, arow, tri_ref, rexp_ref, dskip_ref, s_ref, y_ref, *, backward):
    lane0 = HEADS if backward else 0
    dt_all = dt_ref[...]
    adt = dt_all * arow
    a1 = adt.astype(BF16)
    r1 = adt - a1.astype(F32)
    a2 = r1.astype(BF16)
    a3 = (r1 - a2.astype(F32)).astype(BF16)
    tri = tri_ref[...]
    acs = _dot(tri, a1) + _dot(tri, a2) + _dot(tri, a3)
    edge = 0 if backward else CHUNK - 1
    tot = acs[edge:edge + 1, :]
    decay = jnp.exp(tot - acs)
    eacs = jnp.exp(acs)
    acs_t = acs.T
    rexp = rexp_ref[...]

    def expand(q):
        hi, lo = _split2(q)
        return _dot(hi, rexp) + _dot(lo, rexp)

    dt_e = expand(dt_all)
    decay_e = expand(decay)
    eacs_e = expand(eacs)
    etot_e = eacs_e[edge:edge + 1, :]

    xf = xbc_ref[:, 0:SSD_WIDTH].astype(F32)
    xdt = xf * dt_e
    xdd = (xdt * decay_e).astype(BF16)
    xdt_b = xdt.astype(BF16)

    li = lax.broadcasted_iota(I32, (CHUNK, CHUNK), 0)
    si = lax.broadcasted_iota(I32, (CHUNK, CHUNK), 1)
    keep = (si >= li) if backward else (si <= li)
    left = si < HEAD_DIM
    zero_b = jnp.zeros((CHUNK, LANES), BF16)

    for g in range(GROUPS):
        b_g = xbc_ref[:, SSD_WIDTH + g * STATE:SSD_WIDTH + (g + 1) * STATE]
        c_g = xbc_ref[:, SSD_WIDTH + (GROUPS + g) * STATE:SSD_WIDTH + (GROUPS + g + 1) * STATE]
        b_t = b_g.astype(F32).T.astype(BF16)
        cb = _dot(c_g, b_t)
        gc0 = g * HEADS_PER_GROUP * HEAD_DIM
        gw = HEADS_PER_GROUP * HEAD_DIM
        s_old = s_ref[:, gc0:gc0 + gw]
        y_off = _dot(c_g, s_old.astype(BF16))
        s_new = _dot(b_t, xdd[:, gc0:gc0 + gw])
        s_ref[:, gc0:gc0 + gw] = s_old * etot_e[:, gc0:gc0 + gw] + s_new
        for pp in range(HEADS_PER_GROUP // 2):
            c0 = gc0 + pp * LANES
            ms = []
            for q in range(2):
                hl = lane0 + g * HEADS_PER_GROUP + 2 * pp + q
                seg = acs[:, hl:hl + 1] - acs_t[hl:hl + 1, :]
                lmat = jnp.exp(jnp.where(keep, seg, -jnp.inf))
                ms.append((cb * lmat).astype(BF16))
            lhs = jnp.concatenate(ms, axis=1)
            xp = xdt_b[:, c0:c0 + LANES]
            rhs = jnp.concatenate([jnp.where(left, xp, zero_b), jnp.where(left, zero_b, xp)], axis=0)
            y = _dot(lhs, rhs) + y_off[:, pp * LANES:(pp + 1) * LANES] * eacs_e[:, c0:c0 + LANES]
            if not backward:
                y = y + xf[:, c0:c0 + LANES] * dskip_ref[:, c0:c0 + LANES]
            y_ref[:, c0:c0 + LANES] = y.astype(BF16)


def _ssd_kernel(xf_ref, dtf_ref, xb_ref, dtb_ref, alog_ref, trilo_ref, triup_ref, rf_ref, rb_ref,
                dskip_ref, yf_ref, yb_ref, sf_ref, sb_ref):
    @pl.when(pl.program_id(1) == 0)
    def _():
        sf_ref[...] = jnp.zeros_like(sf_ref)
        sb_ref[...] = jnp.zeros_like(sb_ref)

    arow = -jnp.exp(alog_ref[...])
    _ssd_role(xf_ref, dtf_ref, arow, trilo_ref, rf_ref, dskip_ref, sf_ref, yf_ref, backward=False)
    _ssd_role(xb_ref, dtb_ref, arow, triup_ref, rb_ref, dskip_ref, sb_ref, yb_ref, backward=True)


def _ssd(xbc, dt, batch, seq_len, p):
    t = xbc.shape[0]
    nc = seq_len // CHUNK
    fwd = lambda b, c: (b * nc + c, 0)
    bwd = lambda b, c: (b * nc + nc - 1 - c, 0)
    return pl.pallas_call(
        _ssd_kernel,
        out_shape=(jax.ShapeDtypeStruct((t, SSD_WIDTH), BF16), jax.ShapeDtypeStruct((t, SSD_WIDTH), BF16)),
        grid=(batch, nc),
        in_specs=[
            pl.BlockSpec((CHUNK, CONV_CH), fwd),
            pl.BlockSpec((CHUNK, LANES), fwd),
            pl.BlockSpec((CHUNK, CONV_CH), bwd),
            pl.BlockSpec((CHUNK, LANES), bwd),
            _const_spec((1, LANES)),
            _const_spec((CHUNK, CHUNK)),
            _const_spec((CHUNK, CHUNK)),
            _const_spec((LANES, SSD_WIDTH)),
            _const_spec((LANES, SSD_WIDTH)),
            _const_spec((1, SSD_WIDTH)),
        ],
        out_specs=(pl.BlockSpec((CHUNK, SSD_WIDTH), fwd), pl.BlockSpec((CHUNK, SSD_WIDTH), bwd)),
        scratch_shapes=[pltpu.VMEM((STATE, SSD_WIDTH), F32), pltpu.VMEM((STATE, SSD_WIDTH), F32)],
        compiler_params=pltpu.CompilerParams(
            dimension_semantics=("arbitrary", "arbitrary"), vmem_limit_bytes=VMEM_LIMIT),
        name="ssd",
    )(xbc, dt, xbc, dt, p["a_log"], p["tri_lo"], p["tri_up"], p["rexp_f"], p["rexp_b"], p["d_skip"])


def _outproj_kernel(yf_ref, yb_ref, sz_ref, u_ref, v_ref, x_ref, gs_ref, wsp_ref, bsp_ref, wo_ref,
                    g2_ref, wr_ref, rb_ref, tri_ref, triu_ref,
                    x1_ref, h2_ref, dl_ref, dlt_ref, mg_ref, tn_ref, tc_ref, tl_ref, cnt_ref,
                    gm_buf, carry):
    i = pl.program_id(0)
    tm = x_ref.shape[0]

    @pl.when(i == 0)
    def _():
        carry[...] = jnp.zeros_like(carry)

    gated = (yf_ref[...].astype(F32) + yb_ref[...].astype(F32)) * sz_ref[...].astype(F32)
    ms = jnp.mean(gated * gated, axis=-1, keepdims=True)
    ssd = (gated * lax.rsqrt(ms + RMS_EPS) * gs_ref[...]).astype(BF16)

    for c in range(tm // CHUNK):
        r0 = c * CHUNK
        for hh in range(GMLP_HEADS):
            c0 = hh * GMLP_HEAD_DIM
            sv = _dot(wsp_ref[hh], v_ref[r0:r0 + CHUNK, c0:c0 + GMLP_HEAD_DIM]) + bsp_ref[hh]
            gm_buf[r0:r0 + CHUNK, c0:c0 + GMLP_HEAD_DIM] = (
                u_ref[r0:r0 + CHUNK, c0:c0 + GMLP_HEAD_DIM].astype(F32) * sv).astype(BF16)

    x1 = x_ref[...] + _dot(ssd, wo_ref[0:SSD_WIDTH, :]) + _dot(gm_buf[...], wo_ref[SSD_WIDTH:, :])
    x1_ref[...] = x1

    ms2 = jnp.mean(x1 * x1, axis=-1, keepdims=True)
    h2 = (x1 * lax.rsqrt(ms2 + RMS_EPS) * g2_ref[...]).astype(BF16)
    h2_ref[...] = h2

    logits = _dot(h2, wr_ref[...]) + rb_ref[...]
    lane = lax.broadcasted_iota(I32, (tm, LANES), 1).astype(F32)
    work = logits
    sel_all = jnp.zeros((tm, LANES), F32)
    idxs, vals = [], []
    for _ in range(TOP_K):
        m = jnp.max(work, axis=-1, keepdims=True)
        idx = jnp.min(jnp.where(work == m, lane, float(LANES)), axis=-1, keepdims=True)
        sel = lane == idx
        sel_all = jnp.where(sel, 1.0, sel_all)
        work = jnp.where(sel, -jnp.inf, work)
        idxs.append(idx)
        vals.append(m)
    exps = [jnp.exp(v - vals[0]) for v in vals]
    denom = exps[0] + exps[1] + exps[2] + exps[3]

    lrank = _dot(tri_ref[...], sel_all.astype(BF16))
    n_e = jnp.sum(sel_all, axis=0, keepdims=True)
    n_al = jnp.floor((n_e + (SEG - 1)) * (1.0 / SEG)) * SEG
    lstart = _dot(jnp.broadcast_to(n_al, (8, LANES)).astype(BF16), triu_ref[...])[0:1, :]
    old_cnt = carry[0:1, :]
    new_cnt = old_cnt + n_al
    carry[...] = jnp.broadcast_to(new_cnt, carry.shape)
    cnt_ref[...] = jnp.broadcast_to(new_cnt, cnt_ref.shape)
    tn_ref[0] = n_al.astype(I32)
    tc_ref[0] = old_cnt.astype(I32)
    tl_ref[0] = lstart.astype(I32)

    local_row = lstart + lrank
    dl = jnp.zeros((tm, LANES), F32)
    mg = jnp.zeros((tm, LANES), F32)
    for k in range(TOP_K):
        dk = jnp.sum(jnp.where(lane == idxs[k], local_row, 0.0), axis=-1, keepdims=True)
        dl = jnp.where(lane == k, dk, dl)
        mg = jnp.where(lane == k, exps[k] / denom, mg)
    dl_ref[...] = dl
    mg_ref[...] = mg
    dlt_ref[...] = dl.T[0:8, :]


def _outproj(yf, yb, sz, u, v, x2d, p):
    t = x2d.shape[0]
    tm = TM_OUT
    row = lambda i: (i, 0)
    nt = t // tm
    out_shape = (
        jax.ShapeDtypeStruct((t, D_MODEL), F32),
        jax.ShapeDtypeStruct((t, D_MODEL), BF16),
        jax.ShapeDtypeStruct((t, LANES), F32),
        jax.ShapeDtypeStruct((nt * 8, tm), F32),
        jax.ShapeDtypeStruct((t, LANES), F32),
        jax.ShapeDtypeStruct((nt, 1, LANES), I32),
        jax.ShapeDtypeStruct((nt, 1, LANES), I32),
        jax.ShapeDtypeStruct((nt, 1, LANES), I32),
        jax.ShapeDtypeStruct((8, LANES), F32),
    )
    bf_row = pl.BlockSpec((tm, D_MODEL), row)
    tile_meta = pl.BlockSpec((1, 1, LANES), lambda i: (i, 0, 0))
    return pl.pallas_call(
        _outproj_kernel,
        out_shape=out_shape,
        grid=(t // tm,),
        in_specs=[
            bf_row, bf_row, bf_row, bf_row, bf_row, bf_row,
            _const_spec((1, SSD_WIDTH)),
            _const_spec((GMLP_HEADS, CHUNK, CHUNK)),
            _const_spec((GMLP_HEADS, CHUNK, GMLP_HEAD_DIM)),
            _const_spec((SSD_WIDTH + GMLP_WIDTH, D_MODEL)),
            _const_spec((1, D_MODEL)),
            _const_spec((D_MODEL, LANES)),
            _const_spec((1, LANES)),
            _const_spec((tm, tm)),
            _const_spec((LANES, LANES)),
        ],
        out_specs=(
            pl.BlockSpec((tm, D_MODEL), row),
            pl.BlockSpec((tm, D_MODEL), row),
            pl.BlockSpec((tm, LANES), row),
            pl.BlockSpec((8, tm), row),
            pl.BlockSpec((tm, LANES), row),
            tile_meta, tile_meta, tile_meta,
            _const_spec((8, LANES)),
        ),
        scratch_shapes=[pltpu.VMEM((tm, GMLP_WIDTH), BF16), pltpu.VMEM((8, LANES), F32)],
        compiler_params=pltpu.CompilerParams(
            dimension_semantics=("arbitrary",), vmem_limit_bytes=VMEM_LIMIT),
        name="outproj",
    )(yf, yb, sz, u, v, x2d, p["gs"], p["w_sp"], p["b_sp"], p["w_out"], p["g2"], p["w_r"], p["b_r"],
      p["tri_strict"], p["triu_strict"])


def _copy_segment(make_copy, n, src_row, dst_row):
    big, mid, small = COPY_ROWS

    def body(c, carry_):
        make_copy(pl.multiple_of(src_row + c * big, SEG), pl.multiple_of(dst_row + c * big, SEG), big).start()
        return carry_
    nbig = n // big
    lax.fori_loop(0, nbig, body, 0)
    done = nbig * big

    @pl.when((n & mid) != 0)
    def _():
        make_copy(pl.multiple_of(src_row + done, SEG), pl.multiple_of(dst_row + done, SEG), mid).start()

    done2 = done + (n & mid)

    @pl.when((n & small) != 0)
    def _():
        make_copy(pl.multiple_of(src_row + done2, SEG), pl.multiple_of(dst_row + done2, SEG), small).start()


def _wait_rows(make_copy, nrows):
    w = 1 << (R_LOCAL.bit_length() - 1)
    while w >= SEG:
        @pl.when((nrows & w) != 0)
        def _(w=w):
            make_copy(0, 0, w).wait()
        w //= 2


def _dispatch_kernel(ps_ref, tn_ref, tc_ref, tl_ref, h2_ref, dlt_ref, xs_ref, sbuf, sem, nprev):
    i = pl.program_id(0)
    nt = pl.num_programs(0)
    tm = h2_ref.shape[0]
    slot = i % 2

    h2 = h2_ref[...]
    riota = lax.broadcasted_iota(I32, (SORT_BLK, tm), 0).astype(F32)
    d = [dlt_ref[k:k + 1, :] for k in range(TOP_K)]
    for blk in range(R_LOCAL // SORT_BLK):
        r = riota + float(blk * SORT_BLK)
        pm = jnp.where(r == d[0], 1.0, jnp.where(r == d[1], 1.0, jnp.where(r == d[2], 1.0,
                       jnp.where(r == d[3], 1.0, 0.0)))).astype(BF16)
        s = _dot(pm, h2)
        sbuf[slot, blk * SORT_BLK:(blk + 1) * SORT_BLK, :] = _pack_bf16_pair(s[:, 0:HALF], s[:, HALF:])

    def rows_copy(src_slot):
        def make(src_row, dst_row, rows):
            return pltpu.make_async_copy(sbuf.at[src_slot, pl.ds(src_row, rows)],
                                         xs_ref.at[pl.ds(dst_row, rows)], sem)
        return make

    @pl.when(i > 0)
    def _():
        _wait_rows(rows_copy(0), nprev[0])

    def per_expert(e, total):
        j = i * N_EXPERTS + e
        n = tn_ref[j]
        _copy_segment(rows_copy(slot), n, tl_ref[j], ps_ref[e] + tc_ref[j])
        return total + n

    total = lax.fori_loop(0, N_EXPERTS, per_expert, 0)
    nprev[0] = total

    @pl.when(i == nt - 1)
    def _():
        _wait_rows(rows_copy(0), total)


def _dispatch(pad_start, tn, tc, tl, h2, dlt, rows):
    t = h2.shape[0]
    tm = TM_OUT
    grid_spec = pltpu.PrefetchScalarGridSpec(
        num_scalar_prefetch=4,
        grid=(t // tm,),
        in_specs=[
            pl.BlockSpec((tm, D_MODEL), lambda i, *_: (i, 0)),
            pl.BlockSpec((8, tm), lambda i, *_: (i, 0)),
        ],
        out_specs=pl.BlockSpec(memory_space=pl.ANY),
        scratch_shapes=[pltpu.VMEM((2, R_LOCAL, HALF), U32), pltpu.SemaphoreType.DMA(()),
                        pltpu.SMEM((1,), I32)],
    )
    return pl.pallas_call(
        _dispatch_kernel,
        out_shape=jax.ShapeDtypeStruct((rows, HALF), U32),
        grid_spec=grid_spec,
        compiler_params=pltpu.CompilerParams(
            dimension_semantics=("arbitrary",), vmem_limit_bytes=VMEM_LIMIT, has_side_effects=True),
        name="dispatch",
    )(pad_start, tn, tc, tl, h2, dlt)


def _experts_kernel(be_ref, nv_ref, nu_ref, xs_ref, wgu_ref, bgu_ref, wd_ref, bd_ref, ys_ref):
    i = pl.program_id(0)

    @pl.when(i < nu_ref[0])
    def _():
        rowi = lax.broadcasted_iota(I32, (BM, HALF), 0)
        lo, hi = _unpack_bf16_pair(jnp.where(rowi < nv_ref[i], xs_ref[...], jnp.uint32(0)))
        xb = jnp.concatenate([lo.astype(BF16), hi.astype(BF16)], axis=1)
        gu = _dot(xb, wgu_ref[0]) + bgu_ref[0]
        gate = jnp.minimum(gu[:, 0:D_FF], SWIGLU_LIMIT)
        up = jnp.clip(gu[:, D_FF:], -SWIGLU_LIMIT, SWIGLU_LIMIT)
        act = (up + 1.0) * (gate * jax.nn.sigmoid(gate * SWIGLU_ALPHA))
        y = _dot(act.astype(BF16), wd_ref[0]) + bd_ref[0]
        ys_ref[...] = _pack_bf16_pair(y[:, 0:HALF], y[:, HALF:])

    @pl.when(i >= nu_ref[0])
    def _():
        ys_ref[...] = jnp.zeros_like(ys_ref)


def _experts(block_e, block_nv, n_used, xs, p):
    rows = xs.shape[0]
    nb = rows // BM
    grid_spec = pltpu.PrefetchScalarGridSpec(
        num_scalar_prefetch=3,
        grid=(nb,),
        in_specs=[
            pl.BlockSpec((BM, HALF), lambda i, be, *_: (i, 0)),
            pl.BlockSpec((1, D_MODEL, 2 * D_FF), lambda i, be, *_: (be[i], 0, 0)),
            pl.BlockSpec((1, 1, 2 * D_FF), lambda i, be, *_: (be[i], 0, 0)),
            pl.BlockSpec((1, D_FF, D_MODEL), lambda i, be, *_: (be[i], 0, 0)),
            pl.BlockSpec((1, 1, D_MODEL), lambda i, be, *_: (be[i], 0, 0)),
        ],
        out_specs=pl.BlockSpec((BM, HALF), lambda i, be, *_: (i, 0)),
    )
    return pl.pallas_call(
        _experts_kernel,
        out_shape=jax.ShapeDtypeStruct((rows, HALF), U32),
        grid_spec=grid_spec,
        compiler_params=pltpu.CompilerParams(
            dimension_semantics=("arbitrary",), vmem_limit_bytes=VMEM_LIMIT),
        name="experts",
    )(block_e, block_nv, n_used, xs, p["w_gu"], p["b_gu"], p["w_d"], p["b_d"])


def _combine_kernel(ps_ref, tn_ref, tc_ref, tl_ref, dl_ref, mg_ref, x1_ref, gf_ref, ys_ref, o_ref,
                    ybuf, sems):
    i = pl.program_id(0)
    nt = pl.num_programs(0)
    tm = x1_ref.shape[0]

    def rows_copy(slot):
        def make(src_row, dst_row, rows):
            return pltpu.make_async_copy(ys_ref.at[pl.ds(src_row, rows)],
                                         ybuf.at[slot, pl.ds(dst_row, rows)], sems.at[slot])
        return make

    def issue(tile, slot):
        def per_expert(e, c0):
            j = tile * N_EXPERTS + e
            _copy_segment(rows_copy(slot), tn_ref[j], ps_ref[e] + tc_ref[j], tl_ref[j])
            return c0
        lax.fori_loop(0, N_EXPERTS, per_expert, 0)

    @pl.when(i == 0)
    def _():
        ybuf[...] = jnp.zeros_like(ybuf)
        issue(0, 0)

    @pl.when(i + 1 < nt)
    def _():
        issue(i + 1, (i + 1) % 2)

    slot = i % 2
    n_mine = lax.fori_loop(0, N_EXPERTS, lambda e, c: c + tn_ref[i * N_EXPERTS + e], 0)
    _wait_rows(rows_copy(slot), n_mine)

    dl = dl_ref[...]
    mg = mg_ref[...]
    dk = [jnp.broadcast_to(dl[:, k:k + 1], (tm, SORT_BLK)) for k in range(TOP_K)]
    gk = [jnp.broadcast_to(mg[:, k:k + 1], (tm, SORT_BLK)) for k in range(TOP_K)]
    ciota = lax.broadcasted_iota(I32, (tm, SORT_BLK), 1).astype(F32)
    x1 = x1_ref[...]
    acc_lo = x1[:, 0:HALF]
    acc_hi = x1[:, HALF:]
    for blk in range(R_LOCAL // SORT_BLK):
        r = ciota + float(blk * SORT_BLK)
        g = jnp.where(r == dk[0], gk[0], jnp.where(r == dk[1], gk[1], jnp.where(r == dk[2], gk[2],
                      jnp.where(r == dk[3], gk[3], 0.0)))).astype(BF16)
        lo, hi = _unpack_bf16_pair(ybuf[slot, blk * SORT_BLK:(blk + 1) * SORT_BLK, :])
        acc_lo = acc_lo + _dot(g, lo.astype(BF16))
        acc_hi = acc_hi + _dot(g, hi.astype(BF16))
    acc = jnp.concatenate([acc_lo, acc_hi], axis=1)
    ms = jnp.mean(acc * acc, axis=-1, keepdims=True)
    o_ref[...] = acc * lax.rsqrt(ms + RMS_EPS) * gf_ref[...]


def _combine(pad_start, tn, tc, tl, dl, mg, x1, ys, p):
    t = x1.shape[0]
    tm = TM_OUT
    grid_spec = pltpu.PrefetchScalarGridSpec(
        num_scalar_prefetch=4,
        grid=(t // tm,),
        in_specs=[
            pl.BlockSpec((tm, LANES), lambda i, *_: (i, 0)),
            pl.BlockSpec((tm, LANES), lambda i, *_: (i, 0)),
            pl.BlockSpec((tm, D_MODEL), lambda i, *_: (i, 0)),
            pl.BlockSpec((1, D_MODEL), lambda i, *_: (0, 0)),
            pl.BlockSpec(memory_space=pl.ANY),
        ],
        out_specs=pl.BlockSpec((tm, D_MODEL), lambda i, *_: (i, 0)),
        scratch_shapes=[pltpu.VMEM((2, R_LOCAL, HALF), U32), pltpu.SemaphoreType.DMA((2,))],
    )
    return pl.pallas_call(
        _combine_kernel,
        out_shape=jax.ShapeDtypeStruct((t, D_MODEL), F32),
        grid_spec=grid_spec,
        compiler_params=pltpu.CompilerParams(
            dimension_semantics=("arbitrary",), vmem_limit_bytes=VMEM_LIMIT),
        name="combine",
    )(pad_start, tn, tc, tl, dl, mg, x1, p["gf"], ys)


def _prepare_params(norm1_g, w_in, conv_w, conv_b, dt_bias_f, dt_bias_b, a_log_f, a_log_b, d_skip,
                    ssd_norm_g, gmlp_ln_g, gmlp_ln_b, w_spatial, b_spatial, w_out, norm2_g,
                    router_w, router_b, w_gate_up, b_gate_up, w_down, b_down, final_g):
    z_end = SSD_WIDTH
    xbc_end = z_end + CONV_CH
    dt_end = xbc_end + 2 * HEADS
    u_end = dt_end + GMLP_WIDTH
    w = w_in[0]
    pad_lanes = lambda a, fill=0.0: jnp.pad(a, ((0, 0), (0, LANES - a.shape[1])), constant_values=fill)
    head_lane = jnp.arange(LANES)[:, None]
    chan_head = (jnp.arange(SSD_WIDTH) // HEAD_DIM)[None, :]
    ar = jnp.arange(CHUNK)
    art = jnp.arange(TM_OUT)
    return {
        "g1": norm1_g[0][None, :],
        "w_z": w[:, :z_end].astype(BF16),
        "w_xbc": w[:, z_end:xbc_end].astype(BF16),
        "w_dt": pad_lanes(w[:, xbc_end:dt_end]).astype(BF16),
        "w_u": w[:, dt_end:u_end].astype(BF16),
        "w_v": w[:, u_end:].astype(BF16),
        "conv_w": jnp.pad(conv_w[0], ((0, 8 - CONV_WIDTH), (0, 0))),
        "conv_b": conv_b[0][None, :],
        "dt_bias": pad_lanes(jnp.concatenate([dt_bias_f[0], dt_bias_b[0]])[None, :]),
        "ln_g": gmlp_ln_g[0][None, :],
        "ln_b": gmlp_ln_b[0][None, :],
        "a_log": pad_lanes(jnp.concatenate([a_log_f[0], a_log_b[0]])[None, :]),
        "tri_lo": (ar[:, None] >= ar[None, :]).astype(BF16),
        "tri_up": (ar[:, None] <= ar[None, :]).astype(BF16),
        "rexp_f": (head_lane == chan_head).astype(BF16),
        "rexp_b": (head_lane == chan_head + HEADS).astype(BF16),
        "d_skip": jnp.repeat(d_skip[0], HEAD_DIM)[None, :],
        "gs": ssd_norm_g[0][None, :],
        "w_sp": w_spatial[0].astype(BF16),
        "b_sp": jnp.broadcast_to(b_spatial[0][:, :, None], (GMLP_HEADS, CHUNK, GMLP_HEAD_DIM)),
        "w_out": w_out[0].astype(BF16),
        "g2": norm2_g[0][None, :],
        "w_r": pad_lanes(router_w[0]).astype(BF16),
        "b_r": pad_lanes(router_b[0][None, :], NEG_BIG),
        "tri_strict": (art[:, None] > art[None, :]).astype(BF16),
        "triu_strict": (jnp.arange(LANES)[:, None] < jnp.arange(LANES)[None, :]).astype(BF16),
        "w_gu": w_gate_up[0].astype(BF16),
        "b_gu": b_gate_up[0][:, None, :],
        "w_d": w_down[0].astype(BF16),
        "b_d": b_down[0][:, None, :],
        "gf": final_g[None, :],
    }


def _encoder(x, p):
    batch, seq_len, _ = x.shape
    t = batch * seq_len
    x2d = x.reshape(t, D_MODEL)
    sz, xbc, dt, u, v = _inproj(x2d, seq_len, p)
    yf, yb = _ssd(xbc, dt, batch, seq_len, p)
    x1, h2, dl, dlt, mg, tn, tc, tl, cnt = _outproj(yf, yb, sz, u, v, x2d, p)

    counts = cnt[0, :N_EXPERTS].astype(I32)
    padded = (counts + BM - 1) // BM * BM
    pad_end = jnp.cumsum(padded)
    pad_start = (pad_end - padded).astype(I32)
    n_tiles = t // TM_OUT
    n_blocks = -(-(t * TOP_K + n_tiles * N_EXPERTS * (SEG - 1) + N_EXPERTS * (BM - 1)) // BM)
    n_used = (pad_end[-1] // BM).astype(I32)
    blk_row = jnp.minimum(jnp.arange(n_blocks, dtype=I32), n_used - 1) * BM
    onehot_e = (pad_end[None, :] <= blk_row[:, None]).astype(I32)
    block_e = jnp.minimum(jnp.sum(onehot_e, axis=1), N_EXPERTS - 1).astype(I32)
    is_e = (jnp.arange(N_EXPERTS, dtype=I32)[None, :] == block_e[:, None]).astype(I32)
    block_nv = jnp.clip(jnp.sum(is_e * (counts + pad_start)[None, :], axis=1) - blk_row, 0, BM).astype(I32)
    tiles = lambda a: a.reshape(-1, LANES)[:, :N_EXPERTS].reshape(-1)
    tn, tc, tl = tiles(tn), tiles(tc), tiles(tl)

    xs = _dispatch(pad_start, tn, tc, tl, h2, dlt, n_blocks * BM)
    ys = _experts(block_e, block_nv, n_used.reshape(1), xs, p)
    y = _combine(pad_start, tn, tc, tl, dl, mg, x1, ys, p)
    return y.reshape(batch, seq_len, D_MODEL)


def kernel(x_prompt, x_sample, norm1_g, w_in, conv_w, conv_b, dt_bias_f, dt_bias_b, a_log_f, a_log_b,
           d_skip, ssd_norm_g, gmlp_ln_g, gmlp_ln_b, w_spatial, b_spatial, w_out, norm2_g, router_w,
           router_b, w_gate_up, b_gate_up, w_down, b_down, final_g):
    p = _prepare_params(norm1_g, w_in, conv_w, conv_b, dt_bias_f, dt_bias_b, a_log_f, a_log_b, d_skip,
                        ssd_norm_g, gmlp_ln_g, gmlp_ln_b, w_spatial, b_spatial, w_out, norm2_g,
                        router_w, router_b, w_gate_up, b_gate_up, w_down, b_down, final_g)
    return (_encoder(x_prompt, p), _encoder(x_sample, p))
```

```python
import functools

import jax
import jax.numpy as jnp
from jax import lax
from jax.experimental import pallas as pl
from jax.experimental.pallas import tpu as pltpu

F32 = jnp.float32
BF16 = jnp.bfloat16
U32 = jnp.uint32
I32 = jnp.int32

D_MODEL = 1024
SSD_WIDTH = 1024
HEAD_DIM = 64
HEADS = 16
GROUPS = 2
HEADS_PER_GROUP = 8
STATE = 128
CHUNK = 128
CONV_WIDTH = 5
CONV_HALF = 2
CONV_CH = SSD_WIDTH + 2 * GROUPS * STATE
GMLP_WIDTH = 1024
GMLP_HEADS = 8
GMLP_HEAD_DIM = 128
N_EXPERTS = 32
TOP_K = 4
D_FF = 1024
SWIGLU_LIMIT = 7.0
SWIGLU_ALPHA = 1.702
RMS_EPS = 1e-5
LN_EPS = 1e-5

LANES = 128
HALO = 16
TM_IN = 512
TM_OUT = 512
BM = 512
SEG = 8
COPY_ROWS = (32, 16, 8)
SORT_BLK = 512
R_LOCAL = -(-(TM_OUT * TOP_K + N_EXPERTS * (SEG - 1)) // SORT_BLK) * SORT_BLK
HALF = D_MODEL // 2
VMEM_LIMIT = 56 * 1024 * 1024
NEG_BIG = -1e30


def _dot(a, b):
    return jnp.dot(a, b, preferred_element_type=F32)


def _gelu(x):
    return 0.5 * x * (1.0 + lax.erf(x * (2.0 ** -0.5)))


def _pack_bf16_pair(lo, hi):
    lo_bits = lax.bitcast_convert_type(lo.astype(BF16).astype(F32), U32)
    hi_bits = lax.bitcast_convert_type(hi.astype(BF16).astype(F32), U32)
    return (lo_bits >> 16) | hi_bits


def _unpack_bf16_pair(w):
    lo = lax.bitcast_convert_type(w << 16, F32)
    hi = lax.bitcast_convert_type(w & jnp.uint32(0xFFFF0000), F32)
    return lo, hi


def _const_spec(shape):
    nd = len(shape)
    return pl.BlockSpec(shape, lambda *_: (0,) * nd)


def _inproj_kernel(x_ref, xp_ref, xn_ref, g1_ref, wz_ref, wxbc_ref, wdt_ref, wu_ref, wv_ref,
                   cw_ref, cb_ref, dtb_ref, lng_ref, lnb_ref,
                   sz_ref, xbc_ref, dt_ref, u_ref, v_ref, pbuf, *, tiles_per_seq):
    i = pl.program_id(0)
    tm = x_ref.shape[0]
    g1 = g1_ref[...]

    def norm(xv):
        ms = jnp.mean(xv * xv, axis=-1, keepdims=True)
        return (xv * lax.rsqrt(ms + RMS_EPS) * g1).astype(BF16)

    h = norm(x_ref[...])
    h_all = jnp.concatenate([norm(xp_ref[...]), h, norm(xn_ref[...])], axis=0)

    pbuf[...] = _dot(h_all, wxbc_ref[...])

    @pl.when(i % tiles_per_seq == 0)
    def _():
        pbuf[0:HALO, :] = jnp.zeros((HALO, CONV_CH), F32)

    @pl.when(i % tiles_per_seq == tiles_per_seq - 1)
    def _():
        pbuf[HALO + tm:2 * HALO + tm, :] = jnp.zeros((HALO, CONV_CH), F32)

    acc = jnp.broadcast_to(cb_ref[...], (tm, CONV_CH))
    for k in range(CONV_WIDTH):
        acc = acc + cw_ref[k:k + 1, :] * pbuf[pl.ds(HALO - CONV_HALF + k, tm), :]
    xbc_ref[...] = (acc * jax.nn.sigmoid(acc)).astype(BF16)

    z = _dot(h, wz_ref[...])
    sz_ref[...] = (z * jax.nn.sigmoid(z)).astype(BF16)

    dt_ref[...] = jax.nn.softplus(_dot(h, wdt_ref[...]) + dtb_ref[...])

    u_ref[...] = _gelu(_dot(h, wu_ref[...])).astype(BF16)

    v = _gelu(_dot(h, wv_ref[...]))
    mu = jnp.mean(v, axis=-1, keepdims=True)
    vc = v - mu
    var = jnp.mean(vc * vc, axis=-1, keepdims=True)
    v_ref[...] = (vc * lax.rsqrt(var + LN_EPS) * lng_ref[...] + lnb_ref[...]).astype(BF16)


def _inproj(x2d, seq_len, p):
    t = x2d.shape[0]
    tm = TM_IN
    tiles_per_seq = seq_len // tm
    nh = tm // HALO
    last_halo = t // HALO - 1
    kern = functools.partial(_inproj_kernel, tiles_per_seq=tiles_per_seq)
    row = lambda i: (i, 0)
    out_shape = (
        jax.ShapeDtypeStruct((t, SSD_WIDTH), BF16),
        jax.ShapeDtypeStruct((t, CONV_CH), BF16),
        jax.ShapeDtypeStruct((t, LANES), F32),
        jax.ShapeDtypeStruct((t, GMLP_WIDTH), BF16),
        jax.ShapeDtypeStruct((t, GMLP_WIDTH), BF16),
    )
    return pl.pallas_call(
        kern,
        out_shape=out_shape,
        grid=(t // tm,),
        in_specs=[
            pl.BlockSpec((tm, D_MODEL), row),
            pl.BlockSpec((HALO, D_MODEL), lambda i: (jnp.maximum(i * nh - 1, 0), 0)),
            pl.BlockSpec((HALO, D_MODEL), lambda i: (jnp.minimum((i + 1) * nh, last_halo), 0)),
            _const_spec((1, D_MODEL)),
            _const_spec((D_MODEL, SSD_WIDTH)),
            _const_spec((D_MODEL, CONV_CH)),
            _const_spec((D_MODEL, LANES)),
            _const_spec((D_MODEL, GMLP_WIDTH)),
            _const_spec((D_MODEL, GMLP_WIDTH)),
            _const_spec((8, CONV_CH)),
            _const_spec((1, CONV_CH)),
            _const_spec((1, LANES)),
            _const_spec((1, GMLP_WIDTH)),
            _const_spec((1, GMLP_WIDTH)),
        ],
        out_specs=(
            pl.BlockSpec((tm, SSD_WIDTH), row),
            pl.BlockSpec((tm, CONV_CH), row),
            pl.BlockSpec((tm, LANES), row),
            pl.BlockSpec((tm, GMLP_WIDTH), row),
            pl.BlockSpec((tm, GMLP_WIDTH), row),
        ),
        scratch_shapes=[pltpu.VMEM((tm + 2 * HALO, CONV_CH), F32)],
        compiler_params=pltpu.CompilerParams(
            dimension_semantics=("arbitrary",), vmem_limit_bytes=VMEM_LIMIT),
        name="inproj",
    )(x2d, x2d, x2d, p["g1"], p["w_z"], p["w_xbc"], p["w_dt"], p["w_u"], p["w_v"],
      p["conv_w"], p["conv_b"], p["dt_bias"], p["ln_g"], p["ln_b"])


def _split2(q):
    hi = q.astype(BF16)
    lo = (q - hi.astype(F32)).astype(BF16)
    return hi, lo


def _ssd_role(xbc_ref, dt_ref, arow, tri_ref, rexp_ref, dskip_ref, s_ref, y_ref, *, backward):
    lane0 = HEADS if backward else 0
    dt_all = dt_ref[...]
    adt = dt_all * arow
    a1 = adt.astype(BF16)
    r1 = adt - a1.astype(F32)
    a2 = r1.astype(BF16)
    a3 = (r1 - a2.astype(F32)).astype(BF16)
    acs3 = _dot(tri_ref[...], jnp.concatenate([a1, a2, a3], axis=1))
    acs = acs3[:, 0:LANES] + acs3[:, LANES:2 * LANES] + acs3[:, 2 * LANES:]
    edge = 0 if backward else CHUNK - 1
    tot = acs[edge:edge + 1, :]
    decay = jnp.exp(tot - acs)
    eacs = jnp.exp(acs)
    acs_t = acs.T

    hi, lo = _split2(jnp.concatenate([dt_all, decay, eacs], axis=0))
    expanded = _dot(jnp.concatenate([hi, lo], axis=1), rexp_ref[...])
    dt_e = expanded[0:CHUNK]
    decay_e = expanded[CHUNK:2 * CHUNK]
    eacs_e = expanded[2 * CHUNK:]
    etot_e = eacs_e[edge:edge + 1, :]

    xf = xbc_ref[:, 0:SSD_WIDTH].astype(F32)
    xdt = xf * dt_e
    xdd = (xdt * decay_e).astype(BF16)
    xdt_b = xdt.astype(BF16)

    li = lax.broadcasted_iota(I32, (CHUNK, CHUNK), 0)
    si = lax.broadcasted_iota(I32, (CHUNK, CHUNK), 1)
    keep = (si >= li) if backward else (si <= li)
    left = si < HEAD_DIM
    zero_b = jnp.zeros((CHUNK, LANES), BF16)

    for g in range(GROUPS):
        b_g = xbc_ref[:, SSD_WIDTH + g * STATE:SSD_WIDTH + (g + 1) * STATE]
        c_g = xbc_ref[:, SSD_WIDTH + (GROUPS + g) * STATE:SSD_WIDTH + (GROUPS + g + 1) * STATE]
        b_t = b_g.astype(F32).T.astype(BF16)
        cb = _dot(c_g, b_t)
        gc0 = g * HEADS_PER_GROUP * HEAD_DIM
        gw = HEADS_PER_GROUP * HEAD_DIM
        s_old = s_ref[:, gc0:gc0 + gw]
        y_off = _dot(c_g, s_old.astype(BF16))
        s_new = _dot(b_t, xdd[:, gc0:gc0 + gw])
        s_ref[:, gc0:gc0 + gw] = s_old * etot_e[:, gc0:gc0 + gw] + s_new
        for pp in range(HEADS_PER_GROUP // 2):
            c0 = gc0 + pp * LANES
            ms = []
            for q in range(2):
                hl = lane0 + g * HEADS_PER_GROUP + 2 * pp + q
                seg = acs[:, hl:hl + 1] - acs_t[hl:hl + 1, :]
                lmat = jnp.exp(jnp.where(keep, seg, -jnp.inf))
                ms.append((cb * lmat).astype(BF16))
            lhs = jnp.concatenate(ms, axis=1)
            xp = xdt_b[:, c0:c0 + LANES]
            rhs = jnp.concatenate([jnp.where(left, xp, zero_b), jnp.where(left, zero_b, xp)], axis=0)
            y = _dot(lhs, rhs) + y_off[:, pp * LANES:(pp + 1) * LANES] * eacs_e[:, c0:c0 + LANES]
            if not backward:
                y = y + xf[:, c0:c0 + LANES] * dskip_ref[:, c0:c0 + LANES]
            y_ref[:, c0:c0 + LANES] = y.astype(BF16)


def _ssd_kernel(xf_ref, dtf_ref, xb_ref, dtb_ref, alog_ref, trilo_ref, triup_ref, rf_ref, rb_ref,
                dskip_ref, yf_ref, yb_ref, sf_ref, sb_ref):
    @pl.when(pl.program_id(1) == 0)
    def _():
        sf_ref[...] = jnp.zeros_like(sf_ref)
        sb_ref[...] = jnp.zeros_like(sb_ref)

    arow = -jnp.exp(alog_ref[...])
    _ssd_role(xf_ref, dtf_ref, arow, trilo_ref, rf_ref, dskip_ref, sf_ref, yf_ref, backward=False)
    _ssd_role(xb_ref, dtb_ref, arow, triup_ref, rb_ref, dskip_ref, sb_ref, yb_ref, backward=True)


def _ssd(xbc, dt, batch, seq_len, p):
    t = xbc.shape[0]
    nc = seq_len // CHUNK
    fwd = lambda b, c: (b * nc + c, 0)
    bwd = lambda b, c: (b * nc + nc - 1 - c, 0)
    return pl.pallas_call(
        _ssd_kernel,
        out_shape=(jax.ShapeDtypeStruct((t, SSD_WIDTH), BF16), jax.ShapeDtypeStruct((t, SSD_WIDTH), BF16)),
        grid=(batch, nc),
        in_specs=[
            pl.BlockSpec((CHUNK, CONV_CH), fwd),
            pl.BlockSpec((CHUNK, LANES), fwd),
            pl.BlockSpec((CHUNK, CONV_CH), bwd),
            pl.BlockSpec((CHUNK, LANES), bwd),
            _const_spec((1, LANES)),
            _const_spec((CHUNK, CHUNK)),
            _const_spec((CHUNK, CHUNK)),
            _const_spec((2 * LANES, SSD_WIDTH)),
            _const_spec((2 * LANES, SSD_WIDTH)),
            _const_spec((1, SSD_WIDTH)),
        ],
        out_specs=(pl.BlockSpec((CHUNK, SSD_WIDTH), fwd), pl.BlockSpec((CHUNK, SSD_WIDTH), bwd)),
        scratch_shapes=[pltpu.VMEM((STATE, SSD_WIDTH), F32), pltpu.VMEM((STATE, SSD_WIDTH), F32)],
        compiler_params=pltpu.CompilerParams(
            dimension_semantics=("arbitrary", "arbitrary"), vmem_limit_bytes=VMEM_LIMIT),
        name="ssd",
    )(xbc, dt, xbc, dt, p["a_log"], p["tri_lo"], p["tri_up"], p["rexp_f"], p["rexp_b"], p["d_skip"])


def _outproj_kernel(yf_ref, yb_ref, sz_ref, u_ref, v_ref, x_ref, gs_ref, wsp_ref, bsp_ref, wo_ref,
                    g2_ref, wr_ref, rb_ref, tri_ref, triu_ref,
                    x1_ref, h2_ref, dl_ref, dlt_ref, mg_ref, tn_ref, tc_ref, tl_ref, cnt_ref,
                    gm_buf, carry):
    i = pl.program_id(0)
    tm = x_ref.shape[0]

    @pl.when(i == 0)
    def _():
        carry[...] = jnp.zeros_like(carry)

    gated = (yf_ref[...].astype(F32) + yb_ref[...].astype(F32)) * sz_ref[...].astype(F32)
    ms = jnp.mean(gated * gated, axis=-1, keepdims=True)
    ssd = (gated * lax.rsqrt(ms + RMS_EPS) * gs_ref[...]).astype(BF16)

    for c in range(tm // CHUNK):
        r0 = c * CHUNK
        for hh in range(GMLP_HEADS):
            c0 = hh * GMLP_HEAD_DIM
            sv = _dot(wsp_ref[hh], v_ref[r0:r0 + CHUNK, c0:c0 + GMLP_HEAD_DIM]) + bsp_ref[hh]
            gm_buf[r0:r0 + CHUNK, c0:c0 + GMLP_HEAD_DIM] = (
                u_ref[r0:r0 + CHUNK, c0:c0 + GMLP_HEAD_DIM].astype(F32) * sv).astype(BF16)

    x1 = x_ref[...] + _dot(ssd, wo_ref[0:SSD_WIDTH, :]) + _dot(gm_buf[...], wo_ref[SSD_WIDTH:, :])
    x1_ref[...] = x1

    ms2 = jnp.mean(x1 * x1, axis=-1, keepdims=True)
    h2 = (x1 * lax.rsqrt(ms2 + RMS_EPS) * g2_ref[...]).astype(BF16)
    h2_ref[...] = h2

    logits = _dot(h2, wr_ref[...]) + rb_ref[...]
    lane = lax.broadcasted_iota(I32, (tm, LANES), 1).astype(F32)
    work = logits
    sel_all = jnp.zeros((tm, LANES), F32)
    idxs, vals = [], []
    for _ in range(TOP_K):
        m = jnp.max(work, axis=-1, keepdims=True)
        idx = jnp.min(jnp.where(work == m, lane, float(LANES)), axis=-1, keepdims=True)
        sel = lane == idx
        sel_all = jnp.where(sel, 1.0, sel_all)
        work = jnp.where(sel, -jnp.inf, work)
        idxs.append(idx)
        vals.append(m)
    exps = [jnp.exp(v - vals[0]) for v in vals]
    denom = exps[0] + exps[1] + exps[2] + exps[3]

    lrank = _dot(tri_ref[...], sel_all.astype(BF16))
    n_e = jnp.sum(sel_all, axis=0, keepdims=True)
    n_al = jnp.floor((n_e + (SEG - 1)) * (1.0 / SEG)) * SEG
    lstart = _dot(jnp.broadcast_to(n_al, (8, LANES)).astype(BF16), triu_ref[...])[0:1, :]
    old_cnt = carry[0:1, :]
    new_cnt = old_cnt + n_al
    carry[...] = jnp.broadcast_to(new_cnt, carry.shape)
    cnt_ref[...] = jnp.broadcast_to(new_cnt, cnt_ref.shape)
    tn_ref[0] = n_al.astype(I32)
    tc_ref[0] = old_cnt.astype(I32)
    tl_ref[0] = lstart.astype(I32)

    local_row = lstart + lrank
    dl = jnp.zeros((tm, LANES), F32)
    mg = jnp.zeros((tm, LANES), F32)
    for k in range(TOP_K):
        dk = jnp.sum(jnp.where(lane == idxs[k], local_row, 0.0), axis=-1, keepdims=True)
        dl = jnp.where(lane == k, dk, dl)
        mg = jnp.where(lane == k, exps[k] / denom, mg)
    dl_ref[...] = dl
    mg_ref[...] = mg
    dlt_ref[...] = dl.T[0:8, :]


def _outproj(yf, yb, sz, u, v, x2d, p):
    t = x2d.shape[0]
    tm = TM_OUT
    row = lambda i: (i, 0)
    nt = t // tm
    out_shape = (
        jax.ShapeDtypeStruct((t, D_MODEL), F32),
        jax.ShapeDtypeStruct((t, D_MODEL), BF16),
        jax.ShapeDtypeStruct((t, LANES), F32),
        jax.ShapeDtypeStruct((nt * 8, tm), F32),
        jax.ShapeDtypeStruct((t, LANES), F32),
        jax.ShapeDtypeStruct((nt, 1, LANES), I32),
        jax.ShapeDtypeStruct((nt, 1, LANES), I32),
        jax.ShapeDtypeStruct((nt, 1, LANES), I32),
        jax.ShapeDtypeStruct((8, LANES), F32),
    )
    bf_row = pl.BlockSpec((tm, D_MODEL), row)
    tile_meta = pl.BlockSpec((1, 1, LANES), lambda i: (i, 0, 0))
    return pl.pallas_call(
        _outproj_kernel,
        out_shape=out_shape,
        grid=(t // tm,),
        in_specs=[
            bf_row, bf_row, bf_row, bf_row, bf_row, bf_row,
            _const_spec((1, SSD_WIDTH)),
            _const_spec((GMLP_HEADS, CHUNK, CHUNK)),
            _const_spec((GMLP_HEADS, CHUNK, GMLP_HEAD_DIM)),
            _const_spec((SSD_WIDTH + GMLP_WIDTH, D_MODEL)),
            _const_spec((1, D_MODEL)),
            _const_spec((D_MODEL, LANES)),
            _const_spec((1, LANES)),
            _const_spec((tm, tm)),
            _const_spec((LANES, LANES)),
        ],
        out_specs=(
            pl.BlockSpec((tm, D_MODEL), row),
            pl.BlockSpec((tm, D_MODEL), row),
            pl.BlockSpec((tm, LANES), row),
            pl.BlockSpec((8, tm), row),
            pl.BlockSpec((tm, LANES), row),
            tile_meta, tile_meta, tile_meta,
            _const_spec((8, LANES)),
        ),
        scratch_shapes=[pltpu.VMEM((tm, GMLP_WIDTH), BF16), pltpu.VMEM((8, LANES), F32)],
        compiler_params=pltpu.CompilerParams(
            dimension_semantics=("arbitrary",), vmem_limit_bytes=VMEM_LIMIT),
        name="outproj",
    )(yf, yb, sz, u, v, x2d, p["gs"], p["w_sp"], p["b_sp"], p["w_out"], p["g2"], p["w_r"], p["b_r"],
      p["tri_strict"], p["triu_strict"])


def _copy_segment(make_copy, n, src_row, dst_row):
    big, mid, small = COPY_ROWS

    def body(c, carry_):
        make_copy(pl.multiple_of(src_row + c * big, SEG), pl.multiple_of(dst_row + c * big, SEG), big).start()
        return carry_
    nbig = n // big
    lax.fori_loop(0, nbig, body, 0)
    done = nbig * big

    @pl.when((n & mid) != 0)
    def _():
        make_copy(pl.multiple_of(src_row + done, SEG), pl.multiple_of(dst_row + done, SEG), mid).start()

    done2 = done + (n & mid)

    @pl.when((n & small) != 0)
    def _():
        make_copy(pl.multiple_of(src_row + done2, SEG), pl.multiple_of(dst_row + done2, SEG), small).start()


def _wait_rows(make_copy, nrows, limit=R_LOCAL):
    w = 1 << (limit.bit_length() - 1)
    while w >= SEG:
        @pl.when((nrows & w) != 0)
        def _(w=w):
            make_copy(0, 0, w).wait()
        w //= 2


def _dispatch_kernel(ps_ref, tn_ref, tc_ref, tl_ref, cn_ref, nu_ref, h2_ref, dlt_ref, xs_ref,
                     sbuf, zbuf, sem, nprev):
    i = pl.program_id(0)
    nt = pl.num_programs(0)
    tm = h2_ref.shape[0]
    slot = i % 2

    h2 = h2_ref[...]
    riota = lax.broadcasted_iota(I32, (SORT_BLK, tm), 0).astype(F32)
    d = [dlt_ref[k:k + 1, :] for k in range(TOP_K)]
    for blk in range(R_LOCAL // SORT_BLK):
        r = riota + float(blk * SORT_BLK)
        pm = jnp.where(r == d[0], 1.0, jnp.where(r == d[1], 1.0, jnp.where(r == d[2], 1.0,
                       jnp.where(r == d[3], 1.0, 0.0)))).astype(BF16)
        s = _dot(pm, h2)
        sbuf[slot, blk * SORT_BLK:(blk + 1) * SORT_BLK, :] = _pack_bf16_pair(s[:, 0:HALF], s[:, HALF:])

    def rows_copy(src_slot):
        def make(src_row, dst_row, rows):
            return pltpu.make_async_copy(sbuf.at[src_slot, pl.ds(src_row, rows)],
                                         xs_ref.at[pl.ds(dst_row, rows)], sem)
        return make

    @pl.when(i > 0)
    def _():
        _wait_rows(rows_copy(0), nprev[0])

    def per_expert(e, total):
        j = i * N_EXPERTS + e
        n = tn_ref[j]
        _copy_segment(rows_copy(slot), n, tl_ref[j], ps_ref[e] + tc_ref[j])
        return total + n

    total = lax.fori_loop(0, N_EXPERTS, per_expert, 0)
    nprev[0] = total

    @pl.when(i == nt - 1)
    def _():
        _wait_rows(rows_copy(0), total)
        zbuf[...] = jnp.zeros_like(zbuf)

        def zero_copy(src_row, dst_row, rows):
            return pltpu.make_async_copy(zbuf.at[pl.ds(src_row, rows)], xs_ref.at[pl.ds(dst_row, rows)], sem)

        def tail_rows(e):
            return (cn_ref[e] + (BM - 1)) // BM * BM - cn_ref[e]

        def start_tail(e, c):
            _copy_segment(zero_copy, tail_rows(e), 0, ps_ref[e] + cn_ref[e])
            return c
        lax.fori_loop(0, N_EXPERTS, start_tail, 0)

        def start_block(b, c):
            zero_copy(0, pl.multiple_of(b * BM, BM), BM).start()
            return c
        n_rows_total = xs_ref.shape[0]
        lax.fori_loop(nu_ref[0], n_rows_total // BM, start_block, 0)

        def wait_tail(e, c):
            _wait_rows(zero_copy, tail_rows(e), limit=BM - SEG)
            return c
        lax.fori_loop(0, N_EXPERTS, wait_tail, 0)

        def wait_block(b, c):
            zero_copy(0, 0, BM).wait()
            return c
        lax.fori_loop(nu_ref[0], n_rows_total // BM, wait_block, 0)


def _dispatch(pad_start, tn, tc, tl, counts, n_used, h2, dlt, rows):
    t = h2.shape[0]
    tm = TM_OUT
    grid_spec = pltpu.PrefetchScalarGridSpec(
        num_scalar_prefetch=6,
        grid=(t // tm,),
        in_specs=[
            pl.BlockSpec((tm, D_MODEL), lambda i, *_: (i, 0)),
            pl.BlockSpec((8, tm), lambda i, *_: (i, 0)),
        ],
        out_specs=pl.BlockSpec(memory_space=pl.ANY),
        scratch_shapes=[pltpu.VMEM((2, R_LOCAL, HALF), U32), pltpu.VMEM((BM, HALF), U32),
                        pltpu.SemaphoreType.DMA(()), pltpu.SMEM((1,), I32)],
    )
    return pl.pallas_call(
        _dispatch_kernel,
        out_shape=jax.ShapeDtypeStruct((rows, HALF), U32),
        grid_spec=grid_spec,
        compiler_params=pltpu.CompilerParams(
            dimension_semantics=("arbitrary",), vmem_limit_bytes=VMEM_LIMIT, has_side_effects=True),
        name="dispatch",
    )(pad_start, tn, tc, tl, counts, n_used, h2, dlt)


def _experts_kernel(be_ref, nu_ref, xs_ref, wgu_ref, bgu_ref, wd_ref, bd_ref, ys_ref):
    i = pl.program_id(0)

    @pl.when(i < nu_ref[0])
    def _():
        lo, hi = _unpack_bf16_pair(xs_ref[...])
        xb = jnp.concatenate([lo.astype(BF16), hi.astype(BF16)], axis=1)
        gu = _dot(xb, wgu_ref[0]) + bgu_ref[0]
        gate = jnp.minimum(gu[:, 0:D_FF], SWIGLU_LIMIT)
        up = jnp.clip(gu[:, D_FF:], -SWIGLU_LIMIT, SWIGLU_LIMIT)
        act = (up + 1.0) * (gate * jax.nn.sigmoid(gate * SWIGLU_ALPHA))
        y = _dot(act.astype(BF16), wd_ref[0]) + bd_ref[0]
        ys_ref[...] = _pack_bf16_pair(y[:, 0:HALF], y[:, HALF:])

    @pl.when(i >= nu_ref[0])
    def _():
        ys_ref[...] = jnp.zeros_like(ys_ref)


def _experts(block_e, n_used, xs, p):
    rows = xs.shape[0]
    nb = rows // BM
    grid_spec = pltpu.PrefetchScalarGridSpec(
        num_scalar_prefetch=2,
        grid=(nb,),
        in_specs=[
            pl.BlockSpec((BM, HALF), lambda i, be, *_: (i, 0)),
            pl.BlockSpec((1, D_MODEL, 2 * D_FF), lambda i, be, *_: (be[i], 0, 0)),
            pl.BlockSpec((1, 1, 2 * D_FF), lambda i, be, *_: (be[i], 0, 0)),
            pl.BlockSpec((1, D_FF, D_MODEL), lambda i, be, *_: (be[i], 0, 0)),
            pl.BlockSpec((1, 1, D_MODEL), lambda i, be, *_: (be[i], 0, 0)),
        ],
        out_specs=pl.BlockSpec((BM, HALF), lambda i, be, *_: (i, 0)),
    )
    return pl.pallas_call(
        _experts_kernel,
        out_shape=jax.ShapeDtypeStruct((rows, HALF), U32),
        grid_spec=grid_spec,
        compiler_params=pltpu.CompilerParams(
            dimension_semantics=("arbitrary",), vmem_limit_bytes=VMEM_LIMIT),
        name="experts",
    )(block_e, n_used, xs, p["w_gu"], p["b_gu"], p["w_d"], p["b_d"])


def _combine_kernel(ps_ref, tn_ref, tc_ref, tl_ref, dl_ref, mg_ref, x1_ref, gf_ref, ys_ref, o_ref,
                    ybuf, sems):
    i = pl.program_id(0)
    nt = pl.num_programs(0)
    tm = x1_ref.shape[0]

    def rows_copy(slot):
        def make(src_row, dst_row, rows):
            return pltpu.make_async_copy(ys_ref.at[pl.ds(src_row, rows)],
                                         ybuf.at[slot, pl.ds(dst_row, rows)], sems.at[slot])
        return make

    def issue(tile, slot):
        def per_expert(e, c0):
            j = tile * N_EXPERTS + e
            _copy_segment(rows_copy(slot), tn_ref[j], ps_ref[e] + tc_ref[j], tl_ref[j])
            return c0
        lax.fori_loop(0, N_EXPERTS, per_expert, 0)

    @pl.when(i == 0)
    def _():
        ybuf[...] = jnp.zeros_like(ybuf)
        issue(0, 0)

    @pl.when(i + 1 < nt)
    def _():
        issue(i + 1, (i + 1) % 2)

    slot = i % 2
    n_mine = lax.fori_loop(0, N_EXPERTS, lambda e, c: c + tn_ref[i * N_EXPERTS + e], 0)
    _wait_rows(rows_copy(slot), n_mine)

    dl = dl_ref[...]
    mg = mg_ref[...]
    dk = [jnp.broadcast_to(dl[:, k:k + 1], (tm, SORT_BLK)) for k in range(TOP_K)]
    gk = [jnp.broadcast_to(mg[:, k:k + 1], (tm, SORT_BLK)) for k in range(TOP_K)]
    ciota = lax.broadcasted_iota(I32, (tm, SORT_BLK), 1).astype(F32)
    x1 = x1_ref[...]
    acc_lo = x1[:, 0:HALF]
    acc_hi = x1[:, HALF:]
    for blk in range(R_LOCAL // SORT_BLK):
        r = ciota + float(blk * SORT_BLK)
        g = jnp.where(r == dk[0], gk[0], jnp.where(r == dk[1], gk[1], jnp.where(r == dk[2], gk[2],
                      jnp.where(r == dk[3], gk[3], 0.0)))).astype(BF16)
        lo, hi = _unpack_bf16_pair(ybuf[slot, blk * SORT_BLK:(blk + 1) * SORT_BLK, :])
        acc_lo = acc_lo + _dot(g, lo.astype(BF16))
        acc_hi = acc_hi + _dot(g, hi.astype(BF16))
    acc = jnp.concatenate([acc_lo, acc_hi], axis=1)
    ms = jnp.mean(acc * acc, axis=-1, keepdims=True)
    o_ref[...] = acc * lax.rsqrt(ms + RMS_EPS) * gf_ref[...]


def _combine(pad_start, tn, tc, tl, dl, mg, x1, ys, p):
    t = x1.shape[0]
    tm = TM_OUT
    grid_spec = pltpu.PrefetchScalarGridSpec(
        num_scalar_prefetch=4,
        grid=(t // tm,),
        in_specs=[
            pl.BlockSpec((tm, LANES), lambda i, *_: (i, 0)),
            pl.BlockSpec((tm, LANES), lambda i, *_: (i, 0)),
            pl.BlockSpec((tm, D_MODEL), lambda i, *_: (i, 0)),
            pl.BlockSpec((1, D_MODEL), lambda i, *_: (0, 0)),
            pl.BlockSpec(memory_space=pl.ANY),
        ],
        out_specs=pl.BlockSpec((tm, D_MODEL), lambda i, *_: (i, 0)),
        scratch_shapes=[pltpu.VMEM((2, R_LOCAL, HALF), U32), pltpu.SemaphoreType.DMA((2,))],
    )
    return pl.pallas_call(
        _combine_kernel,
        out_shape=jax.ShapeDtypeStruct((t, D_MODEL), F32),
        grid_spec=grid_spec,
        compiler_params=pltpu.CompilerParams(
            dimension_semantics=("arbitrary",), vmem_limit_bytes=VMEM_LIMIT),
        name="combine",
    )(pad_start, tn, tc, tl, dl, mg, x1, p["gf"], ys)


def _prepare_params(norm1_g, w_in, conv_w, conv_b, dt_bias_f, dt_bias_b, a_log_f, a_log_b, d_skip,
                    ssd_norm_g, gmlp_ln_g, gmlp_ln_b, w_spatial, b_spatial, w_out, norm2_g,
                    router_w, router_b, w_gate_up, b_gate_up, w_down, b_down, final_g):
    z_end = SSD_WIDTH
    xbc_end = z_end + CONV_CH
    dt_end = xbc_end + 2 * HEADS
    u_end = dt_end + GMLP_WIDTH
    w = w_in[0]
    pad_lanes = lambda a, fill=0.0: jnp.pad(a, ((0, 0), (0, LANES - a.shape[1])), constant_values=fill)
    head_lane = jnp.arange(LANES)[:, None]
    chan_head = (jnp.arange(SSD_WIDTH) // HEAD_DIM)[None, :]
    ar = jnp.arange(CHUNK)
    art = jnp.arange(TM_OUT)
    return {
        "g1": norm1_g[0][None, :],
        "w_z": w[:, :z_end].astype(BF16),
        "w_xbc": w[:, z_end:xbc_end].astype(BF16),
        "w_dt": pad_lanes(w[:, xbc_end:dt_end]).astype(BF16),
        "w_u": w[:, dt_end:u_end].astype(BF16),
        "w_v": w[:, u_end:].astype(BF16),
        "conv_w": jnp.pad(conv_w[0], ((0, 8 - CONV_WIDTH), (0, 0))),
        "conv_b": conv_b[0][None, :],
        "dt_bias": pad_lanes(jnp.concatenate([dt_bias_f[0], dt_bias_b[0]])[None, :]),
        "ln_g": gmlp_ln_g[0][None, :],
        "ln_b": gmlp_ln_b[0][None, :],
        "a_log": pad_lanes(jnp.concatenate([a_log_f[0], a_log_b[0]])[None, :]),
        "tri_lo": (ar[:, None] >= ar[None, :]).astype(BF16),
        "tri_up": (ar[:, None] <= ar[None, :]).astype(BF16),
        "rexp_f": jnp.tile((head_lane == chan_head).astype(BF16), (2, 1)),
        "rexp_b": jnp.tile((head_lane == chan_head + HEADS).astype(BF16), (2, 1)),
        "d_skip": jnp.repeat(d_skip[0], HEAD_DIM)[None, :],
        "gs": ssd_norm_g[0][None, :],
        "w_sp": w_spatial[0].astype(BF16),
        "b_sp": jnp.broadcast_to(b_spatial[0][:, :, None], (GMLP_HEADS, CHUNK, GMLP_HEAD_DIM)),
        "w_out": w_out[0].astype(BF16),
        "g2": norm2_g[0][None, :],
        "w_r": pad_lanes(router_w[0]).astype(BF16),
        "b_r": pad_lanes(router_b[0][None, :], NEG_BIG),
        "tri_strict": (art[:, None] > art[None, :]).astype(BF16),
        "triu_strict": (jnp.arange(LANES)[:, None] < jnp.arange(LANES)[None, :]).astype(BF16),
        "w_gu": w_gate_up[0].astype(BF16),
        "b_gu": b_gate_up[0][:, None, :],
        "w_d": w_down[0].astype(BF16),
        "b_d": b_down[0][:, None, :],
        "gf": final_g[None, :],
    }


def _encoder(x, p):
    batch, seq_len, _ = x.shape
    t = batch * seq_len
    x2d = x.reshape(t, D_MODEL)
    sz, xbc, dt, u, v = _inproj(x2d, seq_len, p)
    yf, yb = _ssd(xbc, dt, batch, seq_len, p)
    x1, h2, dl, dlt, mg, tn, tc, tl, cnt = _outproj(yf, yb, sz, u, v, x2d, p)

    counts = cnt[0, :N_EXPERTS].astype(I32)
    padded = (counts + BM - 1) // BM * BM
    pad_end = jnp.cumsum(padded)
    pad_start = (pad_end - padded).astype(I32)
    n_tiles = t // TM_OUT
    n_blocks = -(-(t * TOP_K + n_tiles * N_EXPERTS * (SEG - 1) + N_EXPERTS * (BM - 1)) // BM)
    n_used = (pad_end[-1] // BM).astype(I32)
    blk_row = jnp.minimum(jnp.arange(n_blocks, dtype=I32), n_used - 1) * BM
    onehot_e = (pad_end[None, :] <= blk_row[:, None]).astype(I32)
    block_e = jnp.minimum(jnp.sum(onehot_e, axis=1), N_EXPERTS - 1).astype(I32)
    tiles = lambda a: a.reshape(-1, LANES)[:, :N_EXPERTS].reshape(-1)
    tn, tc, tl = tiles(tn), tiles(tc), tiles(tl)
    n_used = n_used.reshape(1)

    xs = _dispatch(pad_start, tn, tc, tl, counts, n_used, h2, dlt, n_blocks * BM)
    ys = _experts(block_e, n_used, xs, p)
    y = _combine(pad_start, tn, tc, tl, dl, mg, x1, ys, p)
    return y.reshape(batch, seq_len, D_MODEL)


def kernel(x_prompt, x_sample, norm1_g, w_in, conv_w, conv_b, dt_bias_f, dt_bias_b, a_log_f, a_log_b,
           d_skip, ssd_norm_g, gmlp_ln_g, gmlp_ln_b, w_spatial, b_spatial, w_out, norm2_g, router_w,
           router_b, w_gate_up, b_gate_up, w_down, b_down, final_g):
    p = _prepare_params(norm1_g, w_in, conv_w, conv_b, dt_bias_f, dt_bias_b, a_log_f, a_log_b, d_skip,
                        ssd_norm_g, gmlp_ln_g, gmlp_ln_b, w_spatial, b_spatial, w_out, norm2_g,
                        router_w, router_b, w_gate_up, b_gate_up, w_down, b_down, final_g)
    return (_encoder(x_prompt, p), _encoder(x_sample, p))
```

```python
import functools

import jax
import jax.numpy as jnp
from jax import lax
from jax.experimental import pallas as pl
from jax.experimental.pallas import tpu as pltpu

F32 = jnp.float32
BF16 = jnp.bfloat16
U32 = jnp.uint32
I32 = jnp.int32

D_MODEL = 1024
SSD_WIDTH = 1024
HEAD_DIM = 64
HEADS = 16
GROUPS = 2
HEADS_PER_GROUP = 8
STATE = 128
CHUNK = 128
CONV_WIDTH = 5
CONV_HALF = 2
CONV_CH = SSD_WIDTH + 2 * GROUPS * STATE
GMLP_WIDTH = 1024
GMLP_HEADS = 8
GMLP_HEAD_DIM = 128
N_EXPERTS = 32
TOP_K = 4
D_FF = 1024
SWIGLU_LIMIT = 7.0
SWIGLU_ALPHA = 1.702
RMS_EPS = 1e-5
LN_EPS = 1e-5

LANES = 128
HALO = 16
TM_IN = 512
IN_COLS = 256
IN_ROWS = 128
TM_OUT = 512
BM = 512
SEG = 8
COPY_ROWS = (32, 16, 8)
SORT_BLK = 512
R_LOCAL = -(-(TM_OUT * TOP_K + N_EXPERTS * (SEG - 1)) // SORT_BLK) * SORT_BLK
HALF = D_MODEL // 2
VMEM_LIMIT = 56 * 1024 * 1024
NEG_BIG = -1e30


def _dot(a, b):
    return jnp.dot(a, b, preferred_element_type=F32)


def _gelu(x):
    return 0.5 * x * (1.0 + lax.erf(x * (2.0 ** -0.5)))


def _pack_bf16_pair(lo, hi):
    lo_bits = lax.bitcast_convert_type(lo.astype(BF16).astype(F32), U32)
    hi_bits = lax.bitcast_convert_type(hi.astype(BF16).astype(F32), U32)
    return (lo_bits >> 16) | hi_bits


def _unpack_bf16_pair(w):
    lo = lax.bitcast_convert_type(w << 16, F32)
    hi = lax.bitcast_convert_type(w & jnp.uint32(0xFFFF0000), F32)
    return lo, hi


def _const_spec(shape):
    nd = len(shape)
    return pl.BlockSpec(shape, lambda *_: (0,) * nd)


def _inproj_kernel(x_ref, xp_ref, xn_ref, g1_ref, wz_ref, wxbc_ref, wdt_ref, wu_ref, wv_ref,
                   cw_ref, cb_ref, dtb_ref, lng_ref, lnb_ref,
                   sz_ref, xbc_ref, dt_ref, u_ref, v_ref, buf0, buf1, buf2, gbuf, *, tiles_per_seq):
    i = pl.program_id(0)
    tm = x_ref.shape[0]
    g1 = g1_ref[...]
    row_blocks = [(r0, IN_ROWS) for r0 in range(0, tm, IN_ROWS)]

    def norm(xv):
        ms = jnp.mean(xv * xv, axis=-1, keepdims=True)
        return xv * lax.rsqrt(ms + RMS_EPS) * g1

    keep_prev = jnp.where(i % tiles_per_seq == 0, 0.0, 1.0)
    keep_next = jnp.where(i % tiles_per_seq == tiles_per_seq - 1, 0.0, 1.0)
    h = norm(x_ref[...]).astype(BF16)
    h_all = jnp.concatenate([(norm(xp_ref[...]) * keep_prev).astype(BF16), h,
                             (norm(xn_ref[...]) * keep_next).astype(BF16)], axis=0)

    def silu_to(out_ref):
        def epilogue(buf, c0, wd):
            for r0, nr in row_blocks:
                r = buf[r0:r0 + nr, 0:wd]
                out_ref[r0:r0 + nr, c0:c0 + wd] = (r * jax.nn.sigmoid(r)).astype(BF16)
        return epilogue

    def conv_silu(buf, c0, wd):
        for r0, nr in row_blocks:
            acc = jnp.broadcast_to(cb_ref[:, c0:c0 + wd], (nr, wd))
            for k in range(CONV_WIDTH):
                acc = acc + cw_ref[k:k + 1, c0:c0 + wd] * buf[pl.ds(r0 + HALO - CONV_HALF + k, nr), 0:wd]
            xbc_ref[r0:r0 + nr, c0:c0 + wd] = (acc * jax.nn.sigmoid(acc)).astype(BF16)

    def softplus_dt(buf, c0, wd):
        dt_ref[...] = jax.nn.softplus(buf[0:tm, 0:wd] + dtb_ref[...])

    def gelu_u(buf, c0, wd):
        for r0, nr in row_blocks:
            u_ref[r0:r0 + nr, c0:c0 + wd] = _gelu(buf[r0:r0 + nr, 0:wd]).astype(BF16)

    sums = [jnp.zeros((nr, LANES), F32) for _, nr in row_blocks]

    def gelu_v(buf, c0, wd):
        for b, (r0, nr) in enumerate(row_blocks):
            g = _gelu(buf[r0:r0 + nr, 0:wd])
            gbuf[r0:r0 + nr, c0:c0 + wd] = g
            for l0 in range(0, wd, LANES):
                sums[b] = sums[b] + g[:, l0:l0 + LANES]

    heavy = [(h_all, wxbc_ref, c0, IN_COLS, conv_silu) for c0 in range(0, CONV_CH, IN_COLS)]
    light = [(h, wz_ref, c0, IN_COLS, silu_to(sz_ref)) for c0 in range(0, SSD_WIDTH, IN_COLS)]
    light += [(h, wu_ref, c0, IN_COLS, gelu_u) for c0 in range(0, GMLP_WIDTH, IN_COLS)]
    light += [(h, wv_ref, c0, IN_COLS, gelu_v) for c0 in range(0, GMLP_WIDTH, IN_COLS)]
    light.append((h, wdt_ref, 0, LANES, softplus_dt))
    stages = []
    while heavy or light:
        if heavy:
            stages.append(heavy.pop(0))
        if light:
            stages.append(light.pop(0))
    bufs = (buf0, buf1, buf2)
    ahead = len(bufs) - 1

    def matmul(j):
        lhs, w_ref, c0, wd, _ = stages[j]
        bufs[j % len(bufs)][0:lhs.shape[0], 0:wd] = _dot(lhs, w_ref[:, c0:c0 + wd])

    for j in range(ahead):
        matmul(j)
    for j, (_, _, c0, wd, epilogue) in enumerate(stages):
        if j + ahead < len(stages):
            matmul(j + ahead)
        epilogue(bufs[j % len(bufs)], c0, wd)

    inv_n = 1.0 / GMLP_WIDTH
    for b, (r0, nr) in enumerate(row_blocks):
        mu = jnp.sum(sums[b], axis=-1, keepdims=True) * inv_n
        sq = jnp.zeros((nr, LANES), F32)
        for l0 in range(0, GMLP_WIDTH, LANES):
            d = gbuf[r0:r0 + nr, l0:l0 + LANES] - mu
            sq = sq + d * d
        scale = lax.rsqrt(jnp.sum(sq, axis=-1, keepdims=True) * inv_n + LN_EPS)
        for c0 in range(0, GMLP_WIDTH, IN_COLS):
            g = gbuf[r0:r0 + nr, c0:c0 + IN_COLS]
            v_ref[r0:r0 + nr, c0:c0 + IN_COLS] = (
                (g - mu) * scale * lng_ref[:, c0:c0 + IN_COLS] + lnb_ref[:, c0:c0 + IN_COLS]).astype(BF16)


def _inproj(x2d, seq_len, p):
    t = x2d.shape[0]
    tm = TM_IN
    tiles_per_seq = seq_len // tm
    nh = tm // HALO
    last_halo = t // HALO - 1
    kern = functools.partial(_inproj_kernel, tiles_per_seq=tiles_per_seq)
    row = lambda i: (i, 0)
    out_shape = (
        jax.ShapeDtypeStruct((t, SSD_WIDTH), BF16),
        jax.ShapeDtypeStruct((t, CONV_CH), BF16),
        jax.ShapeDtypeStruct((t, LANES), F32),
        jax.ShapeDtypeStruct((t, GMLP_WIDTH), BF16),
        jax.ShapeDtypeStruct((t, GMLP_WIDTH), BF16),
    )
    return pl.pallas_call(
        kern,
        out_shape=out_shape,
        grid=(t // tm,),
        in_specs=[
            pl.BlockSpec((tm, D_MODEL), row),
            pl.BlockSpec((HALO, D_MODEL), lambda i: (jnp.maximum(i * nh - 1, 0), 0)),
            pl.BlockSpec((HALO, D_MODEL), lambda i: (jnp.minimum((i + 1) * nh, last_halo), 0)),
            _const_spec((1, D_MODEL)),
            _const_spec((D_MODEL, SSD_WIDTH)),
            _const_spec((D_MODEL, CONV_CH)),
            _const_spec((D_MODEL, LANES)),
            _const_spec((D_MODEL, GMLP_WIDTH)),
            _const_spec((D_MODEL, GMLP_WIDTH)),
            _const_spec((8, CONV_CH)),
            _const_spec((1, CONV_CH)),
            _const_spec((1, LANES)),
            _const_spec((1, GMLP_WIDTH)),
            _const_spec((1, GMLP_WIDTH)),
        ],
        out_specs=(
            pl.BlockSpec((tm, SSD_WIDTH), row),
            pl.BlockSpec((tm, CONV_CH), row),
            pl.BlockSpec((tm, LANES), row),
            pl.BlockSpec((tm, GMLP_WIDTH), row),
            pl.BlockSpec((tm, GMLP_WIDTH), row),
        ),
        scratch_shapes=[pltpu.VMEM((tm + 2 * HALO, IN_COLS), F32)] * 3 + [pltpu.VMEM((tm, GMLP_WIDTH), F32)],
        compiler_params=pltpu.CompilerParams(
            dimension_semantics=("arbitrary",), vmem_limit_bytes=VMEM_LIMIT),
        name="inproj",
    )(x2d, x2d, x2d, p["g1"], p["w_z"], p["w_xbc"], p["w_dt"], p["w_u"], p["w_v"],
      p["conv_w"], p["conv_b"], p["dt_bias"], p["ln_g"], p["ln_b"])


def _split2(q):
    hi = q.astype(BF16)
    lo = (q - hi.astype(F32)).astype(BF16)
    return hi, lo


def _ssd_role(xbc_ref, dt_ref, arow, tri_ref, rexp_ref, dskip_ref, s_ref, y_ref, *, backward):
    lane0 = HEADS if backward else 0
    dt_all = dt_ref[...]
    adt = dt_all * arow
    a1 = adt.astype(BF16)
    r1 = adt - a1.astype(F32)
    a2 = r1.astype(BF16)
    a3 = (r1 - a2.astype(F32)).astype(BF16)
    acs3 = _dot(tri_ref[...], jnp.concatenate([a1, a2, a3], axis=1))
    acs = acs3[:, 0:LANES] + acs3[:, LANES:2 * LANES] + acs3[:, 2 * LANES:]
    edge = 0 if backward else CHUNK - 1
    tot = acs[edge:edge + 1, :]
    decay = jnp.exp(tot - acs)
    eacs = jnp.exp(acs)
    acs_t = acs.T

    hi, lo = _split2(jnp.concatenate([dt_all, decay, eacs], axis=0))
    expanded = _dot(jnp.concatenate([hi, lo], axis=1), rexp_ref[...])
    dt_e = expanded[0:CHUNK]
    decay_e = expanded[CHUNK:2 * CHUNK]
    eacs_e = expanded[2 * CHUNK:]
    etot_e = eacs_e[edge:edge + 1, :]

    xf = xbc_ref[:, 0:SSD_WIDTH].astype(F32)
    xdt = xf * dt_e
    xdd = (xdt * decay_e).astype(BF16)
    xdt_b = xdt.astype(BF16)

    li = lax.broadcasted_iota(I32, (CHUNK, CHUNK), 0)
    si = lax.broadcasted_iota(I32, (CHUNK, CHUNK), 1)
    keep = (si >= li) if backward else (si <= li)
    left = si < HEAD_DIM
    zero_b = jnp.zeros((CHUNK, LANES), BF16)

    for g in range(GROUPS):
        b_g = xbc_ref[:, SSD_WIDTH + g * STATE:SSD_WIDTH + (g + 1) * STATE]
        c_g = xbc_ref[:, SSD_WIDTH + (GROUPS + g) * STATE:SSD_WIDTH + (GROUPS + g + 1) * STATE]
        b_t = b_g.astype(F32).T.astype(BF16)
        cb = _dot(c_g, b_t)
        gc0 = g * HEADS_PER_GROUP * HEAD_DIM
        gw = HEADS_PER_GROUP * HEAD_DIM
        s_old = s_ref[:, gc0:gc0 + gw]
        y_off = _dot(c_g, s_old.astype(BF16))
        s_new = _dot(b_t, xdd[:, gc0:gc0 + gw])
        s_ref[:, gc0:gc0 + gw] = s_old * etot_e[:, gc0:gc0 + gw] + s_new
        for pp in range(HEADS_PER_GROUP // 2):
            c0 = gc0 + pp * LANES
            ms = []
            for q in range(2):
                hl = lane0 + g * HEADS_PER_GROUP + 2 * pp + q
                seg = acs[:, hl:hl + 1] - acs_t[hl:hl + 1, :]
                lmat = jnp.exp(jnp.where(keep, seg, -jnp.inf))
                ms.append((cb * lmat).astype(BF16))
            lhs = jnp.concatenate(ms, axis=1)
            xp = xdt_b[:, c0:c0 + LANES]
            rhs = jnp.concatenate([jnp.where(left, xp, zero_b), jnp.where(left, zero_b, xp)], axis=0)
            y = _dot(lhs, rhs) + y_off[:, pp * LANES:(pp + 1) * LANES] * eacs_e[:, c0:c0 + LANES]
            if not backward:
                y = y + xf[:, c0:c0 + LANES] * dskip_ref[:, c0:c0 + LANES]
            y_ref[:, c0:c0 + LANES] = y.astype(BF16)


def _ssd_kernel(xf_ref, dtf_ref, xb_ref, dtb_ref, alog_ref, trilo_ref, triup_ref, rf_ref, rb_ref,
                dskip_ref, yf_ref, yb_ref, sf_ref, sb_ref):
    @pl.when(pl.program_id(1) == 0)
    def _():
        sf_ref[...] = jnp.zeros_like(sf_ref)
        sb_ref[...] = jnp.zeros_like(sb_ref)

    arow = -jnp.exp(alog_ref[...])
    _ssd_role(xf_ref, dtf_ref, arow, trilo_ref, rf_ref, dskip_ref, sf_ref, yf_ref, backward=False)
    _ssd_role(xb_ref, dtb_ref, arow, triup_ref, rb_ref, dskip_ref, sb_ref, yb_ref, backward=True)


def _ssd(xbc, dt, batch, seq_len, p):
    t = xbc.shape[0]
    nc = seq_len // CHUNK
    fwd = lambda b, c: (b * nc + c, 0)
    bwd = lambda b, c: (b * nc + nc - 1 - c, 0)
    return pl.pallas_call(
        _ssd_kernel,
        out_shape=(jax.ShapeDtypeStruct((t, SSD_WIDTH), BF16), jax.ShapeDtypeStruct((t, SSD_WIDTH), BF16)),
        grid=(batch, nc),
        in_specs=[
            pl.BlockSpec((CHUNK, CONV_CH), fwd),
            pl.BlockSpec((CHUNK, LANES), fwd),
            pl.BlockSpec((CHUNK, CONV_CH), bwd),
            pl.BlockSpec((CHUNK, LANES), bwd),
            _const_spec((1, LANES)),
            _const_spec((CHUNK, CHUNK)),
            _const_spec((CHUNK, CHUNK)),
            _const_spec((2 * LANES, SSD_WIDTH)),
            _const_spec((2 * LANES, SSD_WIDTH)),
            _const_spec((1, SSD_WIDTH)),
        ],
        out_specs=(pl.BlockSpec((CHUNK, SSD_WIDTH), fwd), pl.BlockSpec((CHUNK, SSD_WIDTH), bwd)),
        scratch_shapes=[pltpu.VMEM((STATE, SSD_WIDTH), F32), pltpu.VMEM((STATE, SSD_WIDTH), F32)],
        compiler_params=pltpu.CompilerParams(
            dimension_semantics=("arbitrary", "arbitrary"), vmem_limit_bytes=VMEM_LIMIT),
        name="ssd",
    )(xbc, dt, xbc, dt, p["a_log"], p["tri_lo"], p["tri_up"], p["rexp_f"], p["rexp_b"], p["d_skip"])


def _outproj_kernel(yf_ref, yb_ref, sz_ref, u_ref, v_ref, x_ref, gs_ref, wsp_ref, bsp_ref, wo_ref,
                    g2_ref, wr_ref, rb_ref, tri_ref, triu_ref,
                    x1_ref, h2_ref, dl_ref, dlt_ref, mg_ref, tn_ref, tc_ref, tl_ref, cnt_ref,
                    gm_buf, carry):
    i = pl.program_id(0)
    tm = x_ref.shape[0]

    @pl.when(i == 0)
    def _():
        carry[...] = jnp.zeros_like(carry)

    gated = (yf_ref[...].astype(F32) + yb_ref[...].astype(F32)) * sz_ref[...].astype(F32)
    ms = jnp.mean(gated * gated, axis=-1, keepdims=True)
    ssd = (gated * lax.rsqrt(ms + RMS_EPS) * gs_ref[...]).astype(BF16)

    for c in range(tm // CHUNK):
        r0 = c * CHUNK
        for hh in range(GMLP_HEADS):
            c0 = hh * GMLP_HEAD_DIM
            sv = _dot(wsp_ref[hh], v_ref[r0:r0 + CHUNK, c0:c0 + GMLP_HEAD_DIM]) + bsp_ref[hh]
            gm_buf[r0:r0 + CHUNK, c0:c0 + GMLP_HEAD_DIM] = (
                u_ref[r0:r0 + CHUNK, c0:c0 + GMLP_HEAD_DIM].astype(F32) * sv).astype(BF16)

    x1 = x_ref[...] + _dot(ssd, wo_ref[0:SSD_WIDTH, :]) + _dot(gm_buf[...], wo_ref[SSD_WIDTH:, :])
    x1_ref[...] = x1

    ms2 = jnp.mean(x1 * x1, axis=-1, keepdims=True)
    h2 = (x1 * lax.rsqrt(ms2 + RMS_EPS) * g2_ref[...]).astype(BF16)
    h2_ref[...] = h2

    logits = _dot(h2, wr_ref[...]) + rb_ref[...]
    lane = lax.broadcasted_iota(I32, (tm, LANES), 1).astype(F32)
    work = logits
    sel_all = jnp.zeros((tm, LANES), F32)
    idxs, vals = [], []
    for _ in range(TOP_K):
        m = jnp.max(work, axis=-1, keepdims=True)
        idx = jnp.min(jnp.where(work == m, lane, float(LANES)), axis=-1, keepdims=True)
        sel = lane == idx
        sel_all = jnp.where(sel, 1.0, sel_all)
        work = jnp.where(sel, -jnp.inf, work)
        idxs.append(idx)
        vals.append(m)
    exps = [jnp.exp(v - vals[0]) for v in vals]
    denom = exps[0] + exps[1] + exps[2] + exps[3]

    lrank = _dot(tri_ref[...], sel_all.astype(BF16))
    n_e = jnp.sum(sel_all, axis=0, keepdims=True)
    n_al = jnp.floor((n_e + (SEG - 1)) * (1.0 / SEG)) * SEG
    lstart = _dot(jnp.broadcast_to(n_al, (8, LANES)).astype(BF16), triu_ref[...])[0:1, :]
    old_cnt = carry[0:1, :]
    new_cnt = old_cnt + n_al
    carry[...] = jnp.broadcast_to(new_cnt, carry.shape)
    cnt_ref[...] = jnp.broadcast_to(new_cnt, cnt_ref.shape)
    tn_ref[0] = n_al.astype(I32)
    tc_ref[0] = old_cnt.astype(I32)
    tl_ref[0] = lstart.astype(I32)

    local_row = lstart + lrank
    dl = jnp.zeros((tm, LANES), F32)
    mg = jnp.zeros((tm, LANES), F32)
    for k in range(TOP_K):
        dk = jnp.sum(jnp.where(lane == idxs[k], local_row, 0.0), axis=-1, keepdims=True)
        dl = jnp.where(lane == k, dk, dl)
        mg = jnp.where(lane == k, exps[k] / denom, mg)
    dl_ref[...] = dl
    mg_ref[...] = mg
    dlt_ref[...] = dl.T[0:8, :]


def _outproj(yf, yb, sz, u, v, x2d, p):
    t = x2d.shape[0]
    tm = TM_OUT
    row = lambda i: (i, 0)
    nt = t // tm
    out_shape = (
        jax.ShapeDtypeStruct((t, D_MODEL), F32),
        jax.ShapeDtypeStruct((t, D_MODEL), BF16),
        jax.ShapeDtypeStruct((t, LANES), F32),
        jax.ShapeDtypeStruct((nt * 8, tm), F32),
        jax.ShapeDtypeStruct((t, LANES), F32),
        jax.ShapeDtypeStruct((nt, 1, LANES), I32),
        jax.ShapeDtypeStruct((nt, 1, LANES), I32),
        jax.ShapeDtypeStruct((nt, 1, LANES), I32),
        jax.ShapeDtypeStruct((8, LANES), F32),
    )
    bf_row = pl.BlockSpec((tm, D_MODEL), row)
    tile_meta = pl.BlockSpec((1, 1, LANES), lambda i: (i, 0, 0))
    return pl.pallas_call(
        _outproj_kernel,
        out_shape=out_shape,
        grid=(t // tm,),
        in_specs=[
            bf_row, bf_row, bf_row, bf_row, bf_row, bf_row,
            _const_spec((1, SSD_WIDTH)),
            _const_spec((GMLP_HEADS, CHUNK, CHUNK)),
            _const_spec((GMLP_HEADS, CHUNK, GMLP_HEAD_DIM)),
            _const_spec((SSD_WIDTH + GMLP_WIDTH, D_MODEL)),
            _const_spec((1, D_MODEL)),
            _const_spec((D_MODEL, LANES)),
            _const_spec((1, LANES)),
            _const_spec((tm, tm)),
            _const_spec((LANES, LANES)),
        ],
        out_specs=(
            pl.BlockSpec((tm, D_MODEL), row),
            pl.BlockSpec((tm, D_MODEL), row),
            pl.BlockSpec((tm, LANES), row),
            pl.BlockSpec((8, tm), row),
            pl.BlockSpec((tm, LANES), row),
            tile_meta, tile_meta, tile_meta,
            _const_spec((8, LANES)),
        ),
        scratch_shapes=[pltpu.VMEM((tm, GMLP_WIDTH), BF16), pltpu.VMEM((8, LANES), F32)],
        compiler_params=pltpu.CompilerParams(
            dimension_semantics=("arbitrary",), vmem_limit_bytes=VMEM_LIMIT),
        name="outproj",
    )(yf, yb, sz, u, v, x2d, p["gs"], p["w_sp"], p["b_sp"], p["w_out"], p["g2"], p["w_r"], p["b_r"],
      p["tri_strict"], p["triu_strict"])


def _copy_segment(make_copy, n, src_row, dst_row):
    big, mid, small = COPY_ROWS

    def body(c, carry_):
        make_copy(pl.multiple_of(src_row + c * big, SEG), pl.multiple_of(dst_row + c * big, SEG), big).start()
        return carry_
    nbig = n // big
    lax.fori_loop(0, nbig, body, 0)
    done = nbig * big

    @pl.when((n & mid) != 0)
    def _():
        make_copy(pl.multiple_of(src_row + done, SEG), pl.multiple_of(dst_row + done, SEG), mid).start()

    done2 = done + (n & mid)

    @pl.when((n & small) != 0)
    def _():
        make_copy(pl.multiple_of(src_row + done2, SEG), pl.multiple_of(dst_row + done2, SEG), small).start()


def _wait_rows(make_copy, nrows, limit=R_LOCAL):
    w = 1 << (limit.bit_length() - 1)
    while w >= SEG:
        @pl.when((nrows & w) != 0)
        def _(w=w):
            make_copy(0, 0, w).wait()
        w //= 2


def _dispatch_kernel(ps_ref, tn_ref, tc_ref, tl_ref, cn_ref, nu_ref, h2_ref, dlt_ref, xs_ref,
                     sbuf, zbuf, sem, nprev):
    i = pl.program_id(0)
    nt = pl.num_programs(0)
    tm = h2_ref.shape[0]
    slot = i % 2

    h2 = h2_ref[...]
    riota = lax.broadcasted_iota(I32, (SORT_BLK, tm), 0).astype(F32)
    d = [dlt_ref[k:k + 1, :] for k in range(TOP_K)]
    for blk in range(R_LOCAL // SORT_BLK):
        r = riota + float(blk * SORT_BLK)
        pm = jnp.where(r == d[0], 1.0, jnp.where(r == d[1], 1.0, jnp.where(r == d[2], 1.0,
                       jnp.where(r == d[3], 1.0, 0.0)))).astype(BF16)
        s = _dot(pm, h2)
        sbuf[slot, blk * SORT_BLK:(blk + 1) * SORT_BLK, :] = _pack_bf16_pair(s[:, 0:HALF], s[:, HALF:])

    def rows_copy(src_slot):
        def make(src_row, dst_row, rows):
            return pltpu.make_async_copy(sbuf.at[src_slot, pl.ds(src_row, rows)],
                                         xs_ref.at[pl.ds(dst_row, rows)], sem)
        return make

    @pl.when(i > 0)
    def _():
        _wait_rows(rows_copy(0), nprev[0])

    def per_expert(e, total):
        j = i * N_EXPERTS + e
        n = tn_ref[j]
        _copy_segment(rows_copy(slot), n, tl_ref[j], ps_ref[e] + tc_ref[j])
        return total + n

    total = lax.fori_loop(0, N_EXPERTS, per_expert, 0)
    nprev[0] = total

    @pl.when(i == nt - 1)
    def _():
        _wait_rows(rows_copy(0), total)
        zbuf[...] = jnp.zeros_like(zbuf)

        def zero_copy(src_row, dst_row, rows):
            return pltpu.make_async_copy(zbuf.at[pl.ds(src_row, rows)], xs_ref.at[pl.ds(dst_row, rows)], sem)

        def tail_rows(e):
            return (cn_ref[e] + (BM - 1)) // BM * BM - cn_ref[e]

        def start_tail(e, c):
            _copy_segment(zero_copy, tail_rows(e), 0, ps_ref[e] + cn_ref[e])
            return c
        lax.fori_loop(0, N_EXPERTS, start_tail, 0)

        def start_block(b, c):
            zero_copy(0, pl.multiple_of(b * BM, BM), BM).start()
            return c
        n_rows_total = xs_ref.shape[0]
        lax.fori_loop(nu_ref[0], n_rows_total // BM, start_block, 0)

        def wait_tail(e, c):
            _wait_rows(zero_copy, tail_rows(e), limit=BM - SEG)
            return c
        lax.fori_loop(0, N_EXPERTS, wait_tail, 0)

        def wait_block(b, c):
            zero_copy(0, 0, BM).wait()
            return c
        lax.fori_loop(nu_ref[0], n_rows_total // BM, wait_block, 0)


def _dispatch(pad_start, tn, tc, tl, counts, n_used, h2, dlt, rows):
    t = h2.shape[0]
    tm = TM_OUT
    grid_spec = pltpu.PrefetchScalarGridSpec(
        num_scalar_prefetch=6,
        grid=(t // tm,),
        in_specs=[
            pl.BlockSpec((tm, D_MODEL), lambda i, *_: (i, 0)),
            pl.BlockSpec((8, tm), lambda i, *_: (i, 0)),
        ],
        out_specs=pl.BlockSpec(memory_space=pl.ANY),
        scratch_shapes=[pltpu.VMEM((2, R_LOCAL, HALF), U32), pltpu.VMEM((BM, HALF), U32),
                        pltpu.SemaphoreType.DMA(()), pltpu.SMEM((1,), I32)],
    )
    return pl.pallas_call(
        _dispatch_kernel,
        out_shape=jax.ShapeDtypeStruct((rows, HALF), U32),
        grid_spec=grid_spec,
        compiler_params=pltpu.CompilerParams(
            dimension_semantics=("arbitrary",), vmem_limit_bytes=VMEM_LIMIT, has_side_effects=True),
        name="dispatch",
    )(pad_start, tn, tc, tl, counts, n_used, h2, dlt)


def _experts_kernel(be_ref, nu_ref, xs_ref, wgu_ref, bgu_ref, wd_ref, bd_ref, ys_ref):
    i = pl.program_id(0)

    @pl.when(i < nu_ref[0])
    def _():
        lo, hi = _unpack_bf16_pair(xs_ref[...])
        xb = jnp.concatenate([lo.astype(BF16), hi.astype(BF16)], axis=1)
        gu = _dot(xb, wgu_ref[0]) + bgu_ref[0]
        gate = jnp.minimum(gu[:, 0:D_FF], SWIGLU_LIMIT)
        up = jnp.clip(gu[:, D_FF:], -SWIGLU_LIMIT, SWIGLU_LIMIT)
        act = (up + 1.0) * (gate * jax.nn.sigmoid(gate * SWIGLU_ALPHA))
        y = _dot(act.astype(BF16), wd_ref[0]) + bd_ref[0]
        ys_ref[...] = _pack_bf16_pair(y[:, 0:HALF], y[:, HALF:])

    @pl.when(i >= nu_ref[0])
    def _():
        ys_ref[...] = jnp.zeros_like(ys_ref)


def _experts(block_e, n_used, xs, p):
    rows = xs.shape[0]
    nb = rows // BM
    grid_spec = pltpu.PrefetchScalarGridSpec(
        num_scalar_prefetch=2,
        grid=(nb,),
        in_specs=[
            pl.BlockSpec((BM, HALF), lambda i, be, *_: (i, 0)),
            pl.BlockSpec((1, D_MODEL, 2 * D_FF), lambda i, be, *_: (be[i], 0, 0)),
            pl.BlockSpec((1, 1, 2 * D_FF), lambda i, be, *_: (be[i], 0, 0)),
            pl.BlockSpec((1, D_FF, D_MODEL), lambda i, be, *_: (be[i], 0, 0)),
            pl.BlockSpec((1, 1, D_MODEL), lambda i, be, *_: (be[i], 0, 0)),
        ],
        out_specs=pl.BlockSpec((BM, HALF), lambda i, be, *_: (i, 0)),
    )
    return pl.pallas_call(
        _experts_kernel,
        out_shape=jax.ShapeDtypeStruct((rows, HALF), U32),
        grid_spec=grid_spec,
        compiler_params=pltpu.CompilerParams(
            dimension_semantics=("arbitrary",), vmem_limit_bytes=VMEM_LIMIT),
        name="experts",
    )(block_e, n_used, xs, p["w_gu"], p["b_gu"], p["w_d"], p["b_d"])


def _combine_kernel(ps_ref, tn_ref, tc_ref, tl_ref, dl_ref, mg_ref, x1_ref, gf_ref, ys_ref, o_ref,
                    ybuf, sems):
    i = pl.program_id(0)
    nt = pl.num_programs(0)
    tm = x1_ref.shape[0]

    def rows_copy(slot):
        def make(src_row, dst_row, rows):
            return pltpu.make_async_copy(ys_ref.at[pl.ds(src_row, rows)],
                                         ybuf.at[slot, pl.ds(dst_row, rows)], sems.at[slot])
        return make

    def issue(tile, slot):
        def per_expert(e, c0):
            j = tile * N_EXPERTS + e
            _copy_segment(rows_copy(slot), tn_ref[j], ps_ref[e] + tc_ref[j], tl_ref[j])
            return c0
        lax.fori_loop(0, N_EXPERTS, per_expert, 0)

    @pl.when(i == 0)
    def _():
        ybuf[...] = jnp.zeros_like(ybuf)
        issue(0, 0)

    @pl.when(i + 1 < nt)
    def _():
        issue(i + 1, (i + 1) % 2)

    slot = i % 2
    n_mine = lax.fori_loop(0, N_EXPERTS, lambda e, c: c + tn_ref[i * N_EXPERTS + e], 0)
    _wait_rows(rows_copy(slot), n_mine)

    dl = dl_ref[...]
    mg = mg_ref[...]
    dk = [jnp.broadcast_to(dl[:, k:k + 1], (tm, SORT_BLK)) for k in range(TOP_K)]
    gk = [jnp.broadcast_to(mg[:, k:k + 1], (tm, SORT_BLK)) for k in range(TOP_K)]
    ciota = lax.broadcasted_iota(I32, (tm, SORT_BLK), 1).astype(F32)
    x1 = x1_ref[...]
    acc_lo = x1[:, 0:HALF]
    acc_hi = x1[:, HALF:]
    for blk in range(R_LOCAL // SORT_BLK):
        r = ciota + float(blk * SORT_BLK)
        g = jnp.where(r == dk[0], gk[0], jnp.where(r == dk[1], gk[1], jnp.where(r == dk[2], gk[2],
                      jnp.where(r == dk[3], gk[3], 0.0)))).astype(BF16)
        lo, hi = _unpack_bf16_pair(ybuf[slot, blk * SORT_BLK:(blk + 1) * SORT_BLK, :])
        acc_lo = acc_lo + _dot(g, lo.astype(BF16))
        acc_hi = acc_hi + _dot(g, hi.astype(BF16))
    acc = jnp.concatenate([acc_lo, acc_hi], axis=1)
    ms = jnp.mean(acc * acc, axis=-1, keepdims=True)
    o_ref[...] = acc * lax.rsqrt(ms + RMS_EPS) * gf_ref[...]


def _combine(pad_start, tn, tc, tl, dl, mg, x1, ys, p):
    t = x1.shape[0]
    tm = TM_OUT
    grid_spec = pltpu.PrefetchScalarGridSpec(
        num_scalar_prefetch=4,
        grid=(t // tm,),
        in_specs=[
            pl.BlockSpec((tm, LANES), lambda i, *_: (i, 0)),
            pl.BlockSpec((tm, LANES), lambda i, *_: (i, 0)),
            pl.BlockSpec((tm, D_MODEL), lambda i, *_: (i, 0)),
            pl.BlockSpec((1, D_MODEL), lambda i, *_: (0, 0)),
            pl.BlockSpec(memory_space=pl.ANY),
        ],
        out_specs=pl.BlockSpec((tm, D_MODEL), lambda i, *_: (i, 0)),
        scratch_shapes=[pltpu.VMEM((2, R_LOCAL, HALF), U32), pltpu.SemaphoreType.DMA((2,))],
    )
    return pl.pallas_call(
        _combine_kernel,
        out_shape=jax.ShapeDtypeStruct((t, D_MODEL), F32),
        grid_spec=grid_spec,
        compiler_params=pltpu.CompilerParams(
            dimension_semantics=("arbitrary",), vmem_limit_bytes=VMEM_LIMIT),
        name="combine",
    )(pad_start, tn, tc, tl, dl, mg, x1, p["gf"], ys)


def _prepare_params(norm1_g, w_in, conv_w, conv_b, dt_bias_f, dt_bias_b, a_log_f, a_log_b, d_skip,
                    ssd_norm_g, gmlp_ln_g, gmlp_ln_b, w_spatial, b_spatial, w_out, norm2_g,
                    router_w, router_b, w_gate_up, b_gate_up, w_down, b_down, final_g):
    z_end = SSD_WIDTH
    xbc_end = z_end + CONV_CH
    dt_end = xbc_end + 2 * HEADS
    u_end = dt_end + GMLP_WIDTH
    w = w_in[0]
    pad_lanes = lambda a, fill=0.0: jnp.pad(a, ((0, 0), (0, LANES - a.shape[1])), constant_values=fill)
    head_lane = jnp.arange(LANES)[:, None]
    chan_head = (jnp.arange(SSD_WIDTH) // HEAD_DIM)[None, :]
    ar = jnp.arange(CHUNK)
    art = jnp.arange(TM_OUT)
    return {
        "g1": norm1_g[0][None, :],
        "w_z": w[:, :z_end].astype(BF16),
        "w_xbc": w[:, z_end:xbc_end].astype(BF16),
        "w_dt": pad_lanes(w[:, xbc_end:dt_end]).astype(BF16),
        "w_u": w[:, dt_end:u_end].astype(BF16),
        "w_v": w[:, u_end:].astype(BF16),
        "conv_w": jnp.pad(conv_w[0], ((0, 8 - CONV_WIDTH), (0, 0))),
        "conv_b": conv_b[0][None, :],
        "dt_bias": pad_lanes(jnp.concatenate([dt_bias_f[0], dt_bias_b[0]])[None, :]),
        "ln_g": gmlp_ln_g[0][None, :],
        "ln_b": gmlp_ln_b[0][None, :],
        "a_log": pad_lanes(jnp.concatenate([a_log_f[0], a_log_b[0]])[None, :]),
        "tri_lo": (ar[:, None] >= ar[None, :]).astype(BF16),
        "tri_up": (ar[:, None] <= ar[None, :]).astype(BF16),
        "rexp_f": jnp.tile((head_lane == chan_head).astype(BF16), (2, 1)),
        "rexp_b": jnp.tile((head_lane == chan_head + HEADS).astype(BF16), (2, 1)),
        "d_skip": jnp.repeat(d_skip[0], HEAD_DIM)[None, :],
        "gs": ssd_norm_g[0][None, :],
        "w_sp": w_spatial[0].astype(BF16),
        "b_sp": jnp.broadcast_to(b_spatial[0][:, :, None], (GMLP_HEADS, CHUNK, GMLP_HEAD_DIM)),
        "w_out": w_out[0].astype(BF16),
        "g2": norm2_g[0][None, :],
        "w_r": pad_lanes(router_w[0]).astype(BF16),
        "b_r": pad_lanes(router_b[0][None, :], NEG_BIG),
        "tri_strict": (art[:, None] > art[None, :]).astype(BF16),
        "triu_strict": (jnp.arange(LANES)[:, None] < jnp.arange(LANES)[None, :]).astype(BF16),
        "w_gu": w_gate_up[0].astype(BF16),
        "b_gu": b_gate_up[0][:, None, :],
        "w_d": w_down[0].astype(BF16),
        "b_d": b_down[0][:, None, :],
        "gf": final_g[None, :],
    }


def _encoder(x, p):
    batch, seq_len, _ = x.shape
    t = batch * seq_len
    x2d = x.reshape(t, D_MODEL)
    sz, xbc, dt, u, v = _inproj(x2d, seq_len, p)
    yf, yb = _ssd(xbc, dt, batch, seq_len, p)
    x1, h2, dl, dlt, mg, tn, tc, tl, cnt = _outproj(yf, yb, sz, u, v, x2d, p)

    counts = cnt[0, :N_EXPERTS].astype(I32)
    padded = (counts + BM - 1) // BM * BM
    pad_end = jnp.cumsum(padded)
    pad_start = (pad_end - padded).astype(I32)
    n_tiles = t // TM_OUT
    n_blocks = -(-(t * TOP_K + n_tiles * N_EXPERTS * (SEG - 1) + N_EXPERTS * (BM - 1)) // BM)
    n_used = (pad_end[-1] // BM).astype(I32)
    blk_row = jnp.minimum(jnp.arange(n_blocks, dtype=I32), n_used - 1) * BM
    onehot_e = (pad_end[None, :] <= blk_row[:, None]).astype(I32)
    block_e = jnp.minimum(jnp.sum(onehot_e, axis=1), N_EXPERTS - 1).astype(I32)
    tiles = lambda a: a.reshape(-1, LANES)[:, :N_EXPERTS].reshape(-1)
    tn, tc, tl = tiles(tn), tiles(tc), tiles(tl)
    n_used = n_used.reshape(1)

    xs = _dispatch(pad_start, tn, tc, tl, counts, n_used, h2, dlt, n_blocks * BM)
    ys = _experts(block_e, n_used, xs, p)
    y = _combine(pad_start, tn, tc, tl, dl, mg, x1, ys, p)
    return y.reshape(batch, seq_len, D_MODEL)


def kernel(x_prompt, x_sample, norm1_g, w_in, conv_w, conv_b, dt_bias_f, dt_bias_b, a_log_f, a_log_b,
           d_skip, ssd_norm_g, gmlp_ln_g, gmlp_ln_b, w_spatial, b_spatial, w_out, norm2_g, router_w,
           router_b, w_gate_up, b_gate_up, w_down, b_down, final_g):
    p = _prepare_params(norm1_g, w_in, conv_w, conv_b, dt_bias_f, dt_bias_b, a_log_f, a_log_b, d_skip,
                        ssd_norm_g, gmlp_ln_g, gmlp_ln_b, w_spatial, b_spatial, w_out, norm2_g,
                        router_w, router_b, w_gate_up, b_gate_up, w_down, b_down, final_g)
    return (_encoder(x_prompt, p), _encoder(x_sample, p))
```

```python
import functools

import jax
import jax.numpy as jnp
from jax import lax
from jax.experimental import pallas as pl
from jax.experimental.pallas import tpu as pltpu

F32 = jnp.float32
BF16 = jnp.bfloat16
U32 = jnp.uint32
I32 = jnp.int32

D_MODEL = 1024
SSD_WIDTH = 1024
HEAD_DIM = 64
HEADS = 16
GROUPS = 2
HEADS_PER_GROUP = 8
STATE = 128
CHUNK = 128
CONV_WIDTH = 5
CONV_HALF = 2
CONV_CH = SSD_WIDTH + 2 * GROUPS * STATE
GMLP_WIDTH = 1024
GMLP_HEADS = 8
GMLP_HEAD_DIM = 128
N_EXPERTS = 32
TOP_K = 4
D_FF = 1024
SWIGLU_LIMIT = 7.0
SWIGLU_ALPHA = 1.702
RMS_EPS = 1e-5
LN_EPS = 1e-5

LANES = 128
HALO = 16
TM_IN = 512
IN_COLS = 256
IN_ROWS = 128
TM_OUT = 512
BM = 512
SEG = 8
COPY_ROWS = (32, 16, 8)
SORT_BLK = 256
R_LOCAL = -(-(TM_OUT * TOP_K + N_EXPERTS * (SEG - 1)) // SORT_BLK) * SORT_BLK
HALF = D_MODEL // 2
VMEM_LIMIT = 56 * 1024 * 1024
NEG_BIG = -1e30


def _dot(a, b):
    return jnp.dot(a, b, preferred_element_type=F32)


def _gelu(x):
    return 0.5 * x * (1.0 + lax.erf(x * (2.0 ** -0.5)))


def _pack_bf16_pair(lo, hi):
    lo_bits = lax.bitcast_convert_type(lo.astype(BF16).astype(F32), U32)
    hi_bits = lax.bitcast_convert_type(hi.astype(BF16).astype(F32), U32)
    return (lo_bits >> 16) | hi_bits


def _unpack_bf16_pair(w):
    lo = lax.bitcast_convert_type(w << 16, F32)
    hi = lax.bitcast_convert_type(w & jnp.uint32(0xFFFF0000), F32)
    return lo, hi


def _const_spec(shape):
    nd = len(shape)
    return pl.BlockSpec(shape, lambda *_: (0,) * nd)


def _inproj_kernel(x_ref, xp_ref, xn_ref, g1_ref, wz_ref, wxbc_ref, wdt_ref, wu_ref, wv_ref,
                   cw_ref, cb_ref, dtb_ref, lng_ref, lnb_ref,
                   sz_ref, xbc_ref, dt_ref, u_ref, v_ref, buf0, buf1, buf2, gbuf, *, tiles_per_seq):
    i = pl.program_id(0)
    tm = x_ref.shape[0]
    g1 = g1_ref[...]
    row_blocks = [(r0, IN_ROWS) for r0 in range(0, tm, IN_ROWS)]

    def norm(xv):
        ms = jnp.mean(xv * xv, axis=-1, keepdims=True)
        return xv * lax.rsqrt(ms + RMS_EPS) * g1

    keep_prev = jnp.where(i % tiles_per_seq == 0, 0.0, 1.0)
    keep_next = jnp.where(i % tiles_per_seq == tiles_per_seq - 1, 0.0, 1.0)
    h = norm(x_ref[...]).astype(BF16)
    h_all = jnp.concatenate([(norm(xp_ref[...]) * keep_prev).astype(BF16), h,
                             (norm(xn_ref[...]) * keep_next).astype(BF16)], axis=0)

    def silu_to(out_ref):
        def epilogue(buf, c0, wd):
            for r0, nr in row_blocks:
                r = buf[r0:r0 + nr, 0:wd]
                out_ref[r0:r0 + nr, c0:c0 + wd] = (r * jax.nn.sigmoid(r)).astype(BF16)
        return epilogue

    def conv_silu(buf, c0, wd):
        for r0, nr in row_blocks:
            acc = jnp.broadcast_to(cb_ref[:, c0:c0 + wd], (nr, wd))
            for k in range(CONV_WIDTH):
                acc = acc + cw_ref[k:k + 1, c0:c0 + wd] * buf[pl.ds(r0 + HALO - CONV_HALF + k, nr), 0:wd]
            xbc_ref[r0:r0 + nr, c0:c0 + wd] = (acc * jax.nn.sigmoid(acc)).astype(BF16)

    def softplus_dt(buf, c0, wd):
        dt_ref[...] = jax.nn.softplus(buf[0:tm, 0:wd] + dtb_ref[...])

    def gelu_u(buf, c0, wd):
        for r0, nr in row_blocks:
            u_ref[r0:r0 + nr, c0:c0 + wd] = _gelu(buf[r0:r0 + nr, 0:wd]).astype(BF16)

    sums = [jnp.zeros((nr, LANES), F32) for _, nr in row_blocks]

    def gelu_v(buf, c0, wd):
        for b, (r0, nr) in enumerate(row_blocks):
            g = _gelu(buf[r0:r0 + nr, 0:wd])
            gbuf[r0:r0 + nr, c0:c0 + wd] = g
            for l0 in range(0, wd, LANES):
                sums[b] = sums[b] + g[:, l0:l0 + LANES]

    heavy = [(h_all, wxbc_ref, c0, IN_COLS, conv_silu) for c0 in range(0, CONV_CH, IN_COLS)]
    light = [(h, wz_ref, c0, IN_COLS, silu_to(sz_ref)) for c0 in range(0, SSD_WIDTH, IN_COLS)]
    light += [(h, wu_ref, c0, IN_COLS, gelu_u) for c0 in range(0, GMLP_WIDTH, IN_COLS)]
    light += [(h, wv_ref, c0, IN_COLS, gelu_v) for c0 in range(0, GMLP_WIDTH, IN_COLS)]
    light.append((h, wdt_ref, 0, LANES, softplus_dt))
    stages = []
    while heavy or light:
        if heavy:
            stages.append(heavy.pop(0))
        if light:
            stages.append(light.pop(0))
    bufs = (buf0, buf1, buf2)
    ahead = len(bufs) - 1

    def matmul(j):
        lhs, w_ref, c0, wd, _ = stages[j]
        bufs[j % len(bufs)][0:lhs.shape[0], 0:wd] = _dot(lhs, w_ref[:, c0:c0 + wd])

    for j in range(ahead):
        matmul(j)
    for j, (_, _, c0, wd, epilogue) in enumerate(stages):
        if j + ahead < len(stages):
            matmul(j + ahead)
        epilogue(bufs[j % len(bufs)], c0, wd)

    inv_n = 1.0 / GMLP_WIDTH
    for b, (r0, nr) in enumerate(row_blocks):
        mu = jnp.sum(sums[b], axis=-1, keepdims=True) * inv_n
        sq = jnp.zeros((nr, LANES), F32)
        for l0 in range(0, GMLP_WIDTH, LANES):
            d = gbuf[r0:r0 + nr, l0:l0 + LANES] - mu
            sq = sq + d * d
        scale = lax.rsqrt(jnp.sum(sq, axis=-1, keepdims=True) * inv_n + LN_EPS)
        for c0 in range(0, GMLP_WIDTH, IN_COLS):
            g = gbuf[r0:r0 + nr, c0:c0 + IN_COLS]
            v_ref[r0:r0 + nr, c0:c0 + IN_COLS] = (
                (g - mu) * scale * lng_ref[:, c0:c0 + IN_COLS] + lnb_ref[:, c0:c0 + IN_COLS]).astype(BF16)


def _inproj(x2d, seq_len, p):
    t = x2d.shape[0]
    tm = TM_IN
    tiles_per_seq = seq_len // tm
    nh = tm // HALO
    last_halo = t // HALO - 1
    kern = functools.partial(_inproj_kernel, tiles_per_seq=tiles_per_seq)
    row = lambda i: (i, 0)
    out_shape = (
        jax.ShapeDtypeStruct((t, SSD_WIDTH), BF16),
        jax.ShapeDtypeStruct((t, CONV_CH), BF16),
        jax.ShapeDtypeStruct((t, LANES), F32),
        jax.ShapeDtypeStruct((t, GMLP_WIDTH), BF16),
        jax.ShapeDtypeStruct((t, GMLP_WIDTH), BF16),
    )
    return pl.pallas_call(
        kern,
        out_shape=out_shape,
        grid=(t // tm,),
        in_specs=[
            pl.BlockSpec((tm, D_MODEL), row),
            pl.BlockSpec((HALO, D_MODEL), lambda i: (jnp.maximum(i * nh - 1, 0), 0)),
            pl.BlockSpec((HALO, D_MODEL), lambda i: (jnp.minimum((i + 1) * nh, last_halo), 0)),
            _const_spec((1, D_MODEL)),
            _const_spec((D_MODEL, SSD_WIDTH)),
            _const_spec((D_MODEL, CONV_CH)),
            _const_spec((D_MODEL, LANES)),
            _const_spec((D_MODEL, GMLP_WIDTH)),
            _const_spec((D_MODEL, GMLP_WIDTH)),
            _const_spec((8, CONV_CH)),
            _const_spec((1, CONV_CH)),
            _const_spec((1, LANES)),
            _const_spec((1, GMLP_WIDTH)),
            _const_spec((1, GMLP_WIDTH)),
        ],
        out_specs=(
            pl.BlockSpec((tm, SSD_WIDTH), row),
            pl.BlockSpec((tm, CONV_CH), row),
            pl.BlockSpec((tm, LANES), row),
            pl.BlockSpec((tm, GMLP_WIDTH), row),
            pl.BlockSpec((tm, GMLP_WIDTH), row),
        ),
        scratch_shapes=[pltpu.VMEM((tm + 2 * HALO, IN_COLS), F32)] * 3 + [pltpu.VMEM((tm, GMLP_WIDTH), F32)],
        compiler_params=pltpu.CompilerParams(
            dimension_semantics=("arbitrary",), vmem_limit_bytes=VMEM_LIMIT),
        name="inproj",
    )(x2d, x2d, x2d, p["g1"], p["w_z"], p["w_xbc"], p["w_dt"], p["w_u"], p["w_v"],
      p["conv_w"], p["conv_b"], p["dt_bias"], p["ln_g"], p["ln_b"])


def _split2(q):
    hi = q.astype(BF16)
    lo = (q - hi.astype(F32)).astype(BF16)
    return hi, lo


def _ssd_role(xbc_ref, dt_ref, arow, tri_ref, rexp_ref, dskip_ref, s_ref, y_ref, *, backward):
    lane0 = HEADS if backward else 0
    dt_all = dt_ref[...]
    adt = dt_all * arow
    a1 = adt.astype(BF16)
    r1 = adt - a1.astype(F32)
    a2 = r1.astype(BF16)
    a3 = (r1 - a2.astype(F32)).astype(BF16)
    acs3 = _dot(tri_ref[...], jnp.concatenate([a1, a2, a3], axis=1))
    acs = acs3[:, 0:LANES] + acs3[:, LANES:2 * LANES] + acs3[:, 2 * LANES:]
    edge = 0 if backward else CHUNK - 1
    tot = acs[edge:edge + 1, :]
    decay = jnp.exp(tot - acs)
    eacs = jnp.exp(acs)
    acs_t = acs.T

    hi, lo = _split2(jnp.concatenate([dt_all, decay, eacs], axis=0))
    expanded = _dot(jnp.concatenate([hi, lo], axis=1), rexp_ref[...])
    dt_e = expanded[0:CHUNK]
    decay_e = expanded[CHUNK:2 * CHUNK]
    eacs_e = expanded[2 * CHUNK:]
    etot_e = eacs_e[edge:edge + 1, :]

    xf = xbc_ref[:, 0:SSD_WIDTH].astype(F32)
    xdt = xf * dt_e
    xdd = (xdt * decay_e).astype(BF16)
    xdt_b = xdt.astype(BF16)

    li = lax.broadcasted_iota(I32, (CHUNK, CHUNK), 0)
    si = lax.broadcasted_iota(I32, (CHUNK, CHUNK), 1)
    keep = (si >= li) if backward else (si <= li)
    left = si < HEAD_DIM
    zero_b = jnp.zeros((CHUNK, LANES), BF16)

    for g in range(GROUPS):
        b_g = xbc_ref[:, SSD_WIDTH + g * STATE:SSD_WIDTH + (g + 1) * STATE]
        c_g = xbc_ref[:, SSD_WIDTH + (GROUPS + g) * STATE:SSD_WIDTH + (GROUPS + g + 1) * STATE]
        b_t = b_g.astype(F32).T.astype(BF16)
        cb = _dot(c_g, b_t)
        gc0 = g * HEADS_PER_GROUP * HEAD_DIM
        gw = HEADS_PER_GROUP * HEAD_DIM
        s_old = s_ref[:, gc0:gc0 + gw]
        y_off = _dot(c_g, s_old.astype(BF16))
        s_new = _dot(b_t, xdd[:, gc0:gc0 + gw])
        s_ref[:, gc0:gc0 + gw] = s_old * etot_e[:, gc0:gc0 + gw] + s_new
        for pp in range(HEADS_PER_GROUP // 2):
            c0 = gc0 + pp * LANES
            ms = []
            for q in range(2):
                hl = lane0 + g * HEADS_PER_GROUP + 2 * pp + q
                seg = acs[:, hl:hl + 1] - acs_t[hl:hl + 1, :]
                lmat = jnp.exp(jnp.where(keep, seg, -jnp.inf))
                ms.append((cb * lmat).astype(BF16))
            lhs = jnp.concatenate(ms, axis=1)
            xp = xdt_b[:, c0:c0 + LANES]
            rhs = jnp.concatenate([jnp.where(left, xp, zero_b), jnp.where(left, zero_b, xp)], axis=0)
            y = _dot(lhs, rhs) + y_off[:, pp * LANES:(pp + 1) * LANES] * eacs_e[:, c0:c0 + LANES]
            if not backward:
                y = y + xf[:, c0:c0 + LANES] * dskip_ref[:, c0:c0 + LANES]
            y_ref[:, c0:c0 + LANES] = y.astype(BF16)


def _ssd_kernel(xf_ref, dtf_ref, xb_ref, dtb_ref, alog_ref, trilo_ref, triup_ref, rf_ref, rb_ref,
                dskip_ref, yf_ref, yb_ref, sf_ref, sb_ref):
    @pl.when(pl.program_id(1) == 0)
    def _():
        sf_ref[...] = jnp.zeros_like(sf_ref)
        sb_ref[...] = jnp.zeros_like(sb_ref)

    arow = -jnp.exp(alog_ref[...])
    _ssd_role(xf_ref, dtf_ref, arow, trilo_ref, rf_ref, dskip_ref, sf_ref, yf_ref, backward=False)
    _ssd_role(xb_ref, dtb_ref, arow, triup_ref, rb_ref, dskip_ref, sb_ref, yb_ref, backward=True)


def _ssd(xbc, dt, batch, seq_len, p):
    t = xbc.shape[0]
    nc = seq_len // CHUNK
    fwd = lambda b, c: (b * nc + c, 0)
    bwd = lambda b, c: (b * nc + nc - 1 - c, 0)
    return pl.pallas_call(
        _ssd_kernel,
        out_shape=(jax.ShapeDtypeStruct((t, SSD_WIDTH), BF16), jax.ShapeDtypeStruct((t, SSD_WIDTH), BF16)),
        grid=(batch, nc),
        in_specs=[
            pl.BlockSpec((CHUNK, CONV_CH), fwd),
            pl.BlockSpec((CHUNK, LANES), fwd),
            pl.BlockSpec((CHUNK, CONV_CH), bwd),
            pl.BlockSpec((CHUNK, LANES), bwd),
            _const_spec((1, LANES)),
            _const_spec((CHUNK, CHUNK)),
            _const_spec((CHUNK, CHUNK)),
            _const_spec((2 * LANES, SSD_WIDTH)),
            _const_spec((2 * LANES, SSD_WIDTH)),
            _const_spec((1, SSD_WIDTH)),
        ],
        out_specs=(pl.BlockSpec((CHUNK, SSD_WIDTH), fwd), pl.BlockSpec((CHUNK, SSD_WIDTH), bwd)),
        scratch_shapes=[pltpu.VMEM((STATE, SSD_WIDTH), F32), pltpu.VMEM((STATE, SSD_WIDTH), F32)],
        compiler_params=pltpu.CompilerParams(
            dimension_semantics=("arbitrary", "arbitrary"), vmem_limit_bytes=VMEM_LIMIT),
        name="ssd",
    )(xbc, dt, xbc, dt, p["a_log"], p["tri_lo"], p["tri_up"], p["rexp_f"], p["rexp_b"], p["d_skip"])


def _outproj_kernel(yf_ref, yb_ref, sz_ref, u_ref, v_ref, x_ref, gs_ref, wsp_ref, bsp_ref, wo_ref,
                    g2_ref, wr_ref, rb_ref, tri_ref, triu_ref,
                    x1_ref, h2_ref, mt_ref, tn_ref, tc_ref, tl_ref, cnt_ref,
                    gm_buf, carry):
    i = pl.program_id(0)
    tm = x_ref.shape[0]

    @pl.when(i == 0)
    def _():
        carry[...] = jnp.zeros_like(carry)

    gated = (yf_ref[...].astype(F32) + yb_ref[...].astype(F32)) * sz_ref[...].astype(F32)
    ms = jnp.mean(gated * gated, axis=-1, keepdims=True)
    ssd = (gated * lax.rsqrt(ms + RMS_EPS) * gs_ref[...]).astype(BF16)

    for c in range(tm // CHUNK):
        r0 = c * CHUNK
        for hh in range(GMLP_HEADS):
            c0 = hh * GMLP_HEAD_DIM
            sv = _dot(wsp_ref[hh], v_ref[r0:r0 + CHUNK, c0:c0 + GMLP_HEAD_DIM]) + bsp_ref[hh]
            gm_buf[r0:r0 + CHUNK, c0:c0 + GMLP_HEAD_DIM] = (
                u_ref[r0:r0 + CHUNK, c0:c0 + GMLP_HEAD_DIM].astype(F32) * sv).astype(BF16)

    x1 = x_ref[...] + _dot(ssd, wo_ref[0:SSD_WIDTH, :]) + _dot(gm_buf[...], wo_ref[SSD_WIDTH:, :])
    x1_ref[...] = x1

    ms2 = jnp.mean(x1 * x1, axis=-1, keepdims=True)
    h2 = (x1 * lax.rsqrt(ms2 + RMS_EPS) * g2_ref[...]).astype(BF16)
    h2_ref[...] = h2

    logits = _dot(h2, wr_ref[...]) + rb_ref[...]
    lane = lax.broadcasted_iota(I32, (tm, LANES), 1).astype(F32)
    work = logits
    sel_all = jnp.zeros((tm, LANES), F32)
    idxs, vals = [], []
    for _ in range(TOP_K):
        m = jnp.max(work, axis=-1, keepdims=True)
        idx = jnp.min(jnp.where(work == m, lane, float(LANES)), axis=-1, keepdims=True)
        sel = lane == idx
        sel_all = jnp.where(sel, 1.0, sel_all)
        work = jnp.where(sel, -jnp.inf, work)
        idxs.append(idx)
        vals.append(m)
    exps = [jnp.exp(v - vals[0]) for v in vals]
    denom = exps[0] + exps[1] + exps[2] + exps[3]

    lrank = _dot(tri_ref[...], sel_all.astype(BF16))
    n_e = jnp.sum(sel_all, axis=0, keepdims=True)
    n_al = jnp.floor((n_e + (SEG - 1)) * (1.0 / SEG)) * SEG
    lstart = _dot(jnp.broadcast_to(n_al, (8, LANES)).astype(BF16), triu_ref[...])[0:1, :]
    old_cnt = carry[0:1, :]
    new_cnt = old_cnt + n_al
    carry[...] = jnp.broadcast_to(new_cnt, carry.shape)
    cnt_ref[...] = jnp.broadcast_to(new_cnt, cnt_ref.shape)
    tn_ref[0] = n_al.astype(I32)
    tc_ref[0] = old_cnt.astype(I32)
    tl_ref[0] = lstart.astype(I32)

    local_row = lstart + lrank
    meta = jnp.zeros((tm, LANES), F32)
    for k in range(TOP_K):
        dk = jnp.sum(jnp.where(lane == idxs[k], local_row, 0.0), axis=-1, keepdims=True)
        meta = jnp.where(lane == k, dk, meta)
        meta = jnp.where(lane == TOP_K + k, exps[k] / denom, meta)
    mt_ref[...] = meta.T[0:2 * TOP_K, :]


def _outproj(yf, yb, sz, u, v, x2d, p):
    t = x2d.shape[0]
    tm = TM_OUT
    row = lambda i: (i, 0)
    nt = t // tm
    out_shape = (
        jax.ShapeDtypeStruct((t, D_MODEL), F32),
        jax.ShapeDtypeStruct((t, D_MODEL), BF16),
        jax.ShapeDtypeStruct((nt * 2 * TOP_K, tm), F32),
        jax.ShapeDtypeStruct((nt, 1, LANES), I32),
        jax.ShapeDtypeStruct((nt, 1, LANES), I32),
        jax.ShapeDtypeStruct((nt, 1, LANES), I32),
        jax.ShapeDtypeStruct((8, LANES), F32),
    )
    bf_row = pl.BlockSpec((tm, D_MODEL), row)
    tile_meta = pl.BlockSpec((1, 1, LANES), lambda i: (i, 0, 0))
    return pl.pallas_call(
        _outproj_kernel,
        out_shape=out_shape,
        grid=(t // tm,),
        in_specs=[
            bf_row, bf_row, bf_row, bf_row, bf_row, bf_row,
            _const_spec((1, SSD_WIDTH)),
            _const_spec((GMLP_HEADS, CHUNK, CHUNK)),
            _const_spec((GMLP_HEADS, CHUNK, GMLP_HEAD_DIM)),
            _const_spec((SSD_WIDTH + GMLP_WIDTH, D_MODEL)),
            _const_spec((1, D_MODEL)),
            _const_spec((D_MODEL, LANES)),
            _const_spec((1, LANES)),
            _const_spec((tm, tm)),
            _const_spec((LANES, LANES)),
        ],
        out_specs=(
            pl.BlockSpec((tm, D_MODEL), row),
            pl.BlockSpec((tm, D_MODEL), row),
            pl.BlockSpec((2 * TOP_K, tm), row),
            tile_meta, tile_meta, tile_meta,
            _const_spec((8, LANES)),
        ),
        scratch_shapes=[pltpu.VMEM((tm, GMLP_WIDTH), BF16), pltpu.VMEM((8, LANES), F32)],
        compiler_params=pltpu.CompilerParams(
            dimension_semantics=("arbitrary",), vmem_limit_bytes=VMEM_LIMIT),
        name="outproj",
    )(yf, yb, sz, u, v, x2d, p["gs"], p["w_sp"], p["b_sp"], p["w_out"], p["g2"], p["w_r"], p["b_r"],
      p["tri_strict"], p["triu_strict"])


def _copy_segment(make_copy, n, src_row, dst_row):
    big, mid, small = COPY_ROWS

    def body(c, carry_):
        make_copy(pl.multiple_of(src_row + c * big, SEG), pl.multiple_of(dst_row + c * big, SEG), big).start()
        return carry_
    nbig = n // big
    lax.fori_loop(0, nbig, body, 0)
    done = nbig * big

    @pl.when((n & mid) != 0)
    def _():
        make_copy(pl.multiple_of(src_row + done, SEG), pl.multiple_of(dst_row + done, SEG), mid).start()

    done2 = done + (n & mid)

    @pl.when((n & small) != 0)
    def _():
        make_copy(pl.multiple_of(src_row + done2, SEG), pl.multiple_of(dst_row + done2, SEG), small).start()


def _wait_rows(make_copy, nrows, limit=R_LOCAL):
    w = 1 << (limit.bit_length() - 1)
    while w >= SEG:
        @pl.when((nrows & w) != 0)
        def _(w=w):
            make_copy(0, 0, w).wait()
        w //= 2


def _onehot_rows(mt_ref, blk, values):
    tm = mt_ref.shape[1]
    riota = lax.broadcasted_iota(I32, (SORT_BLK, tm), 0).astype(F32).astype(BF16)
    out = jnp.zeros((SORT_BLK, tm), BF16)
    for k in reversed(range(TOP_K)):
        off = (mt_ref[k:k + 1, :] - float(blk * SORT_BLK)).astype(BF16)
        out = jnp.where(riota == off, values[k], out)
    return out


def _dispatch_kernel(ps_ref, tn_ref, tc_ref, tl_ref, cn_ref, nu_ref, h2_ref, mt_ref, xs_ref,
                     sbuf, zbuf, sem, nprev):
    i = pl.program_id(0)
    nt = pl.num_programs(0)
    tm = h2_ref.shape[0]
    slot = i % 2

    h2 = h2_ref[...]
    ones = [jnp.ones((1, tm), BF16)] * TOP_K
    for blk in range(R_LOCAL // SORT_BLK):
        s = _dot(_onehot_rows(mt_ref, blk, ones), h2)
        lo_bits = lax.bitcast_convert_type(s[:, 0:HALF], U32)
        hi_bits = lax.bitcast_convert_type(s[:, HALF:], U32)
        sbuf[slot, blk * SORT_BLK:(blk + 1) * SORT_BLK, :] = (lo_bits >> 16) | hi_bits

    def rows_copy(src_slot):
        def make(src_row, dst_row, rows):
            return pltpu.make_async_copy(sbuf.at[src_slot, pl.ds(src_row, rows)],
                                         xs_ref.at[pl.ds(dst_row, rows)], sem)
        return make

    @pl.when(i > 0)
    def _():
        _wait_rows(rows_copy(0), nprev[0])

    def per_expert(e, total):
        j = i * N_EXPERTS + e
        n = tn_ref[j]
        _copy_segment(rows_copy(slot), n, tl_ref[j], ps_ref[e] + tc_ref[j])
        return total + n

    total = lax.fori_loop(0, N_EXPERTS, per_expert, 0)
    nprev[0] = total

    @pl.when(i == nt - 1)
    def _():
        _wait_rows(rows_copy(0), total)
        zbuf[...] = jnp.zeros_like(zbuf)

        def zero_copy(src_row, dst_row, rows):
            return pltpu.make_async_copy(zbuf.at[pl.ds(src_row, rows)], xs_ref.at[pl.ds(dst_row, rows)], sem)

        def tail_rows(e):
            return (cn_ref[e] + (BM - 1)) // BM * BM - cn_ref[e]

        def start_tail(e, c):
            _copy_segment(zero_copy, tail_rows(e), 0, ps_ref[e] + cn_ref[e])
            return c
        lax.fori_loop(0, N_EXPERTS, start_tail, 0)

        def start_block(b, c):
            zero_copy(0, pl.multiple_of(b * BM, BM), BM).start()
            return c
        n_rows_total = xs_ref.shape[0]
        lax.fori_loop(nu_ref[0], n_rows_total // BM, start_block, 0)

        def wait_tail(e, c):
            _wait_rows(zero_copy, tail_rows(e), limit=BM - SEG)
            return c
        lax.fori_loop(0, N_EXPERTS, wait_tail, 0)

        def wait_block(b, c):
            zero_copy(0, 0, BM).wait()
            return c
        lax.fori_loop(nu_ref[0], n_rows_total // BM, wait_block, 0)


def _dispatch(pad_start, tn, tc, tl, counts, n_used, h2, mt, rows):
    t = h2.shape[0]
    tm = TM_OUT
    grid_spec = pltpu.PrefetchScalarGridSpec(
        num_scalar_prefetch=6,
        grid=(t // tm,),
        in_specs=[
            pl.BlockSpec((tm, D_MODEL), lambda i, *_: (i, 0)),
            pl.BlockSpec((2 * TOP_K, tm), lambda i, *_: (i, 0)),
        ],
        out_specs=pl.BlockSpec(memory_space=pl.ANY),
        scratch_shapes=[pltpu.VMEM((2, R_LOCAL, HALF), U32), pltpu.VMEM((BM, HALF), U32),
                        pltpu.SemaphoreType.DMA(()), pltpu.SMEM((1,), I32)],
    )
    return pl.pallas_call(
        _dispatch_kernel,
        out_shape=jax.ShapeDtypeStruct((rows, HALF), U32),
        grid_spec=grid_spec,
        compiler_params=pltpu.CompilerParams(
            dimension_semantics=("arbitrary",), vmem_limit_bytes=VMEM_LIMIT, has_side_effects=True),
        name="dispatch",
    )(pad_start, tn, tc, tl, counts, n_used, h2, mt)


def _experts_kernel(be_ref, nu_ref, xs_ref, wgu_ref, bgu_ref, wd_ref, bd_ref, ys_ref):
    i = pl.program_id(0)

    @pl.when(i < nu_ref[0])
    def _():
        lo, hi = _unpack_bf16_pair(xs_ref[...])
        xb = jnp.concatenate([lo.astype(BF16), hi.astype(BF16)], axis=1)
        gu = _dot(xb, wgu_ref[0]) + bgu_ref[0]
        gate = jnp.minimum(gu[:, 0:D_FF], SWIGLU_LIMIT)
        up = jnp.clip(gu[:, D_FF:], -SWIGLU_LIMIT, SWIGLU_LIMIT)
        act = (up + 1.0) * (gate * jax.nn.sigmoid(gate * SWIGLU_ALPHA))
        y = _dot(act.astype(BF16), wd_ref[0]) + bd_ref[0]
        ys_ref[...] = _pack_bf16_pair(y[:, 0:HALF], y[:, HALF:])

    @pl.when(i >= nu_ref[0])
    def _():
        ys_ref[...] = jnp.zeros_like(ys_ref)


def _experts(block_e, n_used, xs, p):
    rows = xs.shape[0]
    nb = rows // BM
    grid_spec = pltpu.PrefetchScalarGridSpec(
        num_scalar_prefetch=2,
        grid=(nb,),
        in_specs=[
            pl.BlockSpec((BM, HALF), lambda i, be, *_: (i, 0)),
            pl.BlockSpec((1, D_MODEL, 2 * D_FF), lambda i, be, *_: (be[i], 0, 0)),
            pl.BlockSpec((1, 1, 2 * D_FF), lambda i, be, *_: (be[i], 0, 0)),
            pl.BlockSpec((1, D_FF, D_MODEL), lambda i, be, *_: (be[i], 0, 0)),
            pl.BlockSpec((1, 1, D_MODEL), lambda i, be, *_: (be[i], 0, 0)),
        ],
        out_specs=pl.BlockSpec((BM, HALF), lambda i, be, *_: (i, 0)),
    )
    return pl.pallas_call(
        _experts_kernel,
        out_shape=jax.ShapeDtypeStruct((rows, HALF), U32),
        grid_spec=grid_spec,
        compiler_params=pltpu.CompilerParams(
            dimension_semantics=("arbitrary",), vmem_limit_bytes=VMEM_LIMIT),
        name="experts",
    )(block_e, n_used, xs, p["w_gu"], p["b_gu"], p["w_d"], p["b_d"])


def _combine_kernel(ps_ref, tn_ref, tc_ref, tl_ref, mt_ref, x1_ref, gf_ref, ys_ref, o_ref,
                    ybuf, gt_buf, ylo_buf, yhi_buf, sems):
    i = pl.program_id(0)
    nt = pl.num_programs(0)
    tm = x1_ref.shape[0]

    def rows_copy(slot):
        def make(src_row, dst_row, rows):
            return pltpu.make_async_copy(ys_ref.at[pl.ds(src_row, rows)],
                                         ybuf.at[slot, pl.ds(dst_row, rows)], sems.at[slot])
        return make

    def issue(tile, slot):
        def per_expert(e, c0):
            j = tile * N_EXPERTS + e
            _copy_segment(rows_copy(slot), tn_ref[j], ps_ref[e] + tc_ref[j], tl_ref[j])
            return c0
        lax.fori_loop(0, N_EXPERTS, per_expert, 0)

    @pl.when(i == 0)
    def _():
        ybuf[...] = jnp.zeros_like(ybuf)
        issue(0, 0)

    @pl.when(i + 1 < nt)
    def _():
        issue(i + 1, (i + 1) % 2)

    slot = i % 2
    n_mine = lax.fori_loop(0, N_EXPERTS, lambda e, c: c + tn_ref[i * N_EXPERTS + e], 0)
    _wait_rows(rows_copy(slot), n_mine)

    gates = [mt_ref[TOP_K + k:TOP_K + k + 1, :].astype(BF16) for k in range(TOP_K)]
    for blk in range(R_LOCAL // SORT_BLK):
        rows = slice(blk * SORT_BLK, (blk + 1) * SORT_BLK)
        gt_buf[rows, :] = _onehot_rows(mt_ref, blk, gates)
        lo, hi = _unpack_bf16_pair(ybuf[slot, rows, :])
        ylo_buf[rows, :] = lo.astype(BF16)
        yhi_buf[rows, :] = hi.astype(BF16)
    contract_rows = (((0,), (0,)), ((), ()))
    g_t = gt_buf[...]
    x1 = x1_ref[...]
    acc = jnp.concatenate(
        [x1[:, 0:HALF] + lax.dot_general(g_t, ylo_buf[...], contract_rows, preferred_element_type=F32),
         x1[:, HALF:] + lax.dot_general(g_t, yhi_buf[...], contract_rows, preferred_element_type=F32)], axis=1)
    ms = jnp.mean(acc * acc, axis=-1, keepdims=True)
    o_ref[...] = acc * lax.rsqrt(ms + RMS_EPS) * gf_ref[...]


def _combine(pad_start, tn, tc, tl, mt, x1, ys, p):
    t = x1.shape[0]
    tm = TM_OUT
    grid_spec = pltpu.PrefetchScalarGridSpec(
        num_scalar_prefetch=4,
        grid=(t // tm,),
        in_specs=[
            pl.BlockSpec((2 * TOP_K, tm), lambda i, *_: (i, 0)),
            pl.BlockSpec((tm, D_MODEL), lambda i, *_: (i, 0)),
            pl.BlockSpec((1, D_MODEL), lambda i, *_: (0, 0)),
            pl.BlockSpec(memory_space=pl.ANY),
        ],
        out_specs=pl.BlockSpec((tm, D_MODEL), lambda i, *_: (i, 0)),
        scratch_shapes=[pltpu.VMEM((2, R_LOCAL, HALF), U32), pltpu.VMEM((R_LOCAL, tm), BF16),
                        pltpu.VMEM((R_LOCAL, HALF), BF16), pltpu.VMEM((R_LOCAL, HALF), BF16),
                        pltpu.SemaphoreType.DMA((2,))],
    )
    return pl.pallas_call(
        _combine_kernel,
        out_shape=jax.ShapeDtypeStruct((t, D_MODEL), F32),
        grid_spec=grid_spec,
        compiler_params=pltpu.CompilerParams(
            dimension_semantics=("arbitrary",), vmem_limit_bytes=VMEM_LIMIT),
        name="combine",
    )(pad_start, tn, tc, tl, mt, x1, p["gf"], ys)


def _prepare_params(norm1_g, w_in, conv_w, conv_b, dt_bias_f, dt_bias_b, a_log_f, a_log_b, d_skip,
                    ssd_norm_g, gmlp_ln_g, gmlp_ln_b, w_spatial, b_spatial, w_out, norm2_g,
                    router_w, router_b, w_gate_up, b_gate_up, w_down, b_down, final_g):
    z_end = SSD_WIDTH
    xbc_end = z_end + CONV_CH
    dt_end = xbc_end + 2 * HEADS
    u_end = dt_end + GMLP_WIDTH
    w = w_in[0]
    pad_lanes = lambda a, fill=0.0: jnp.pad(a, ((0, 0), (0, LANES - a.shape[1])), constant_values=fill)
    head_lane = jnp.arange(LANES)[:, None]
    chan_head = (jnp.arange(SSD_WIDTH) // HEAD_DIM)[None, :]
    ar = jnp.arange(CHUNK)
    art = jnp.arange(TM_OUT)
    return {
        "g1": norm1_g[0][None, :],
        "w_z": w[:, :z_end].astype(BF16),
        "w_xbc": w[:, z_end:xbc_end].astype(BF16),
        "w_dt": pad_lanes(w[:, xbc_end:dt_end]).astype(BF16),
        "w_u": w[:, dt_end:u_end].astype(BF16),
        "w_v": w[:, u_end:].astype(BF16),
        "conv_w": jnp.pad(conv_w[0], ((0, 8 - CONV_WIDTH), (0, 0))),
        "conv_b": conv_b[0][None, :],
        "dt_bias": pad_lanes(jnp.concatenate([dt_bias_f[0], dt_bias_b[0]])[None, :]),
        "ln_g": gmlp_ln_g[0][None, :],
        "ln_b": gmlp_ln_b[0][None, :],
        "a_log": pad_lanes(jnp.concatenate([a_log_f[0], a_log_b[0]])[None, :]),
        "tri_lo": (ar[:, None] >= ar[None, :]).astype(BF16),
        "tri_up": (ar[:, None] <= ar[None, :]).astype(BF16),
        "rexp_f": jnp.tile((head_lane == chan_head).astype(BF16), (2, 1)),
        "rexp_b": jnp.tile((head_lane == chan_head + HEADS).astype(BF16), (2, 1)),
        "d_skip": jnp.repeat(d_skip[0], HEAD_DIM)[None, :],
        "gs": ssd_norm_g[0][None, :],
        "w_sp": w_spatial[0].astype(BF16),
        "b_sp": jnp.broadcast_to(b_spatial[0][:, :, None], (GMLP_HEADS, CHUNK, GMLP_HEAD_DIM)),
        "w_out": w_out[0].astype(BF16),
        "g2": norm2_g[0][None, :],
        "w_r": pad_lanes(router_w[0]).astype(BF16),
        "b_r": pad_lanes(router_b[0][None, :], NEG_BIG),
        "tri_strict": (art[:, None] > art[None, :]).astype(BF16),
        "triu_strict": (jnp.arange(LANES)[:, None] < jnp.arange(LANES)[None, :]).astype(BF16),
        "w_gu": w_gate_up[0].astype(BF16),
        "b_gu": b_gate_up[0][:, None, :],
        "w_d": w_down[0].astype(BF16),
        "b_d": b_down[0][:, None, :],
        "gf": final_g[None, :],
    }


def _encoder(x, p):
    batch, seq_len, _ = x.shape
    t = batch * seq_len
    x2d = x.reshape(t, D_MODEL)
    sz, xbc, dt, u, v = _inproj(x2d, seq_len, p)
    yf, yb = _ssd(xbc, dt, batch, seq_len, p)
    x1, h2, mt, tn, tc, tl, cnt = _outproj(yf, yb, sz, u, v, x2d, p)

    counts = cnt[0, :N_EXPERTS].astype(I32)
    padded = (counts + BM - 1) // BM * BM
    pad_end = jnp.cumsum(padded)
    pad_start = (pad_end - padded).astype(I32)
    n_tiles = t // TM_OUT
    n_blocks = -(-(t * TOP_K + n_tiles * N_EXPERTS * (SEG - 1) + N_EXPERTS * (BM - 1)) // BM)
    n_used = (pad_end[-1] // BM).astype(I32)
    blk_row = jnp.minimum(jnp.arange(n_blocks, dtype=I32), n_used - 1) * BM
    onehot_e = (pad_end[None, :] <= blk_row[:, None]).astype(I32)
    block_e = jnp.minimum(jnp.sum(onehot_e, axis=1), N_EXPERTS - 1).astype(I32)
    tiles = lambda a: a.reshape(-1, LANES)[:, :N_EXPERTS].reshape(-1)
    tn, tc, tl = tiles(tn), tiles(tc), tiles(tl)
    n_used = n_used.reshape(1)

    xs = _dispatch(pad_start, tn, tc, tl, counts, n_used, h2, mt, n_blocks * BM)
    ys = _experts(block_e, n_used, xs, p)
    y = _combine(pad_start, tn, tc, tl, mt, x1, ys, p)
    return y.reshape(batch, seq_len, D_MODEL)


def kernel(x_prompt, x_sample, norm1_g, w_in, conv_w, conv_b, dt_bias_f, dt_bias_b, a_log_f, a_log_b,
           d_skip, ssd_norm_g, gmlp_ln_g, gmlp_ln_b, w_spatial, b_spatial, w_out, norm2_g, router_w,
           router_b, w_gate_up, b_gate_up, w_down, b_down, final_g):
    p = _prepare_params(norm1_g, w_in, conv_w, conv_b, dt_bias_f, dt_bias_b, a_log_f, a_log_b, d_skip,
                        ssd_norm_g, gmlp_ln_g, gmlp_ln_b, w_spatial, b_spatial, w_out, norm2_g,
                        router_w, router_b, w_gate_up, b_gate_up, w_down, b_down, final_g)
    return (_encoder(x_prompt, p), _encoder(x_sample, p))
```

```python
import functools

import jax
import jax.numpy as jnp
from jax import lax
from jax.experimental import pallas as pl
from jax.experimental.pallas import tpu as pltpu

F32 = jnp.float32
BF16 = jnp.bfloat16
U32 = jnp.uint32
I32 = jnp.int32

D_MODEL = 1024
SSD_WIDTH = 1024
HEAD_DIM = 64
HEADS = 16
GROUPS = 2
HEADS_PER_GROUP = 8
STATE = 128
CHUNK = 128
CONV_WIDTH = 5
CONV_HALF = 2
CONV_CH = SSD_WIDTH + 2 * GROUPS * STATE
GMLP_WIDTH = 1024
GMLP_HEADS = 8
GMLP_HEAD_DIM = 128
N_EXPERTS = 32
TOP_K = 4
D_FF = 1024
SWIGLU_LIMIT = 7.0
SWIGLU_ALPHA = 1.702
RMS_EPS = 1e-5
LN_EPS = 1e-5

LANES = 128
HALO = 16
TM_IN = 512
IN_COLS = 256
IN_ROWS = 128
TM_OUT = 512
BM = 512
SEG = 8
COPY_ROWS = (32, 16, 8)
SORT_BLK = 256
R_LOCAL = -(-(TM_OUT * TOP_K + N_EXPERTS * (SEG - 1)) // SORT_BLK) * SORT_BLK
HALF = D_MODEL // 2
VMEM_LIMIT = 56 * 1024 * 1024
NEG_BIG = -1e30
LOG2_E = 1.4426950408889634


def _dot(a, b):
    return jnp.dot(a, b, preferred_element_type=F32)


def _gelu(x):
    return 0.5 * x * (1.0 + lax.erf(x * (2.0 ** -0.5)))


def _pack_bf16_pair(lo, hi):
    lo_bits = lax.bitcast_convert_type(lo.astype(BF16).astype(F32), U32)
    hi_bits = lax.bitcast_convert_type(hi.astype(BF16).astype(F32), U32)
    return (lo_bits >> 16) | hi_bits


def _unpack_bf16_pair(w):
    lo = lax.bitcast_convert_type(w << 16, F32)
    hi = lax.bitcast_convert_type(w & jnp.uint32(0xFFFF0000), F32)
    return lo, hi


def _const_spec(shape):
    nd = len(shape)
    return pl.BlockSpec(shape, lambda *_: (0,) * nd)


def _inproj_kernel(x_ref, xp_ref, xn_ref, g1_ref, wz_ref, wxbc_ref, wdt_ref, wu_ref, wv_ref,
                   cw_ref, cb_ref, dtb_ref, lng_ref, lnb_ref,
                   sz_ref, xbc_ref, dt_ref, u_ref, v_ref, buf0, buf1, buf2, gbuf, *, tiles_per_seq):
    i = pl.program_id(0)
    tm = x_ref.shape[0]
    g1 = g1_ref[...]
    row_blocks = [(r0, IN_ROWS) for r0 in range(0, tm, IN_ROWS)]

    def norm(xv):
        ms = jnp.mean(xv * xv, axis=-1, keepdims=True)
        return xv * lax.rsqrt(ms + RMS_EPS) * g1

    keep_prev = jnp.where(i % tiles_per_seq == 0, 0.0, 1.0)
    keep_next = jnp.where(i % tiles_per_seq == tiles_per_seq - 1, 0.0, 1.0)
    h = norm(x_ref[...]).astype(BF16)
    h_all = jnp.concatenate([(norm(xp_ref[...]) * keep_prev).astype(BF16), h,
                             (norm(xn_ref[...]) * keep_next).astype(BF16)], axis=0)

    def silu_to(out_ref):
        def epilogue(buf, c0, wd):
            for r0, nr in row_blocks:
                r = buf[r0:r0 + nr, 0:wd]
                out_ref[r0:r0 + nr, c0:c0 + wd] = (r * jax.nn.sigmoid(r)).astype(BF16)
        return epilogue

    def conv_silu(buf, c0, wd):
        for r0, nr in row_blocks:
            acc = jnp.broadcast_to(cb_ref[:, c0:c0 + wd], (nr, wd))
            for k in range(CONV_WIDTH):
                acc = acc + cw_ref[k:k + 1, c0:c0 + wd] * buf[pl.ds(r0 + HALO - CONV_HALF + k, nr), 0:wd]
            xbc_ref[r0:r0 + nr, c0:c0 + wd] = (acc * jax.nn.sigmoid(acc)).astype(BF16)

    def softplus_dt(buf, c0, wd):
        dt_ref[...] = jax.nn.softplus(buf[0:tm, 0:wd] + dtb_ref[...])

    def gelu_u(buf, c0, wd):
        for r0, nr in row_blocks:
            u_ref[r0:r0 + nr, c0:c0 + wd] = _gelu(buf[r0:r0 + nr, 0:wd]).astype(BF16)

    sums = [jnp.zeros((nr, LANES), F32) for _, nr in row_blocks]

    def gelu_v(buf, c0, wd):
        for b, (r0, nr) in enumerate(row_blocks):
            g = _gelu(buf[r0:r0 + nr, 0:wd])
            gbuf[r0:r0 + nr, c0:c0 + wd] = g
            for l0 in range(0, wd, LANES):
                sums[b] = sums[b] + g[:, l0:l0 + LANES]

    heavy = [(h_all, wxbc_ref, c0, IN_COLS, conv_silu) for c0 in range(0, CONV_CH, IN_COLS)]
    light = [(h, wz_ref, c0, IN_COLS, silu_to(sz_ref)) for c0 in range(0, SSD_WIDTH, IN_COLS)]
    light += [(h, wu_ref, c0, IN_COLS, gelu_u) for c0 in range(0, GMLP_WIDTH, IN_COLS)]
    light += [(h, wv_ref, c0, IN_COLS, gelu_v) for c0 in range(0, GMLP_WIDTH, IN_COLS)]
    light.append((h, wdt_ref, 0, LANES, softplus_dt))
    stages = []
    while heavy or light:
        if heavy:
            stages.append(heavy.pop(0))
        if light:
            stages.append(light.pop(0))
    bufs = (buf0, buf1, buf2)
    ahead = len(bufs) - 1

    def matmul(j):
        lhs, w_ref, c0, wd, _ = stages[j]
        bufs[j % len(bufs)][0:lhs.shape[0], 0:wd] = _dot(lhs, w_ref[:, c0:c0 + wd])

    for j in range(ahead):
        matmul(j)
    for j, (_, _, c0, wd, epilogue) in enumerate(stages):
        if j + ahead < len(stages):
            matmul(j + ahead)
        epilogue(bufs[j % len(bufs)], c0, wd)

    inv_n = 1.0 / GMLP_WIDTH
    for b, (r0, nr) in enumerate(row_blocks):
        mu = jnp.sum(sums[b], axis=-1, keepdims=True) * inv_n
        sq = jnp.zeros((nr, LANES), F32)
        for l0 in range(0, GMLP_WIDTH, LANES):
            d = gbuf[r0:r0 + nr, l0:l0 + LANES] - mu
            sq = sq + d * d
        scale = lax.rsqrt(jnp.sum(sq, axis=-1, keepdims=True) * inv_n + LN_EPS)
        for c0 in range(0, GMLP_WIDTH, IN_COLS):
            g = gbuf[r0:r0 + nr, c0:c0 + IN_COLS]
            v_ref[r0:r0 + nr, c0:c0 + IN_COLS] = (
                (g - mu) * scale * lng_ref[:, c0:c0 + IN_COLS] + lnb_ref[:, c0:c0 + IN_COLS]).astype(BF16)


def _inproj(x2d, seq_len, p):
    t = x2d.shape[0]
    tm = TM_IN
    tiles_per_seq = seq_len // tm
    nh = tm // HALO
    last_halo = t // HALO - 1
    kern = functools.partial(_inproj_kernel, tiles_per_seq=tiles_per_seq)
    row = lambda i: (i, 0)
    out_shape = (
        jax.ShapeDtypeStruct((t, SSD_WIDTH), BF16),
        jax.ShapeDtypeStruct((t, CONV_CH), BF16),
        jax.ShapeDtypeStruct((t, LANES), F32),
        jax.ShapeDtypeStruct((t, GMLP_WIDTH), BF16),
        jax.ShapeDtypeStruct((t, GMLP_WIDTH), BF16),
    )
    return pl.pallas_call(
        kern,
        out_shape=out_shape,
        grid=(t // tm,),
        in_specs=[
            pl.BlockSpec((tm, D_MODEL), row),
            pl.BlockSpec((HALO, D_MODEL), lambda i: (jnp.maximum(i * nh - 1, 0), 0)),
            pl.BlockSpec((HALO, D_MODEL), lambda i: (jnp.minimum((i + 1) * nh, last_halo), 0)),
            _const_spec((1, D_MODEL)),
            _const_spec((D_MODEL, SSD_WIDTH)),
            _const_spec((D_MODEL, CONV_CH)),
            _const_spec((D_MODEL, LANES)),
            _const_spec((D_MODEL, GMLP_WIDTH)),
            _const_spec((D_MODEL, GMLP_WIDTH)),
            _const_spec((8, CONV_CH)),
            _const_spec((1, CONV_CH)),
            _const_spec((1, LANES)),
            _const_spec((1, GMLP_WIDTH)),
            _const_spec((1, GMLP_WIDTH)),
        ],
        out_specs=(
            pl.BlockSpec((tm, SSD_WIDTH), row),
            pl.BlockSpec((tm, CONV_CH), row),
            pl.BlockSpec((tm, LANES), row),
            pl.BlockSpec((tm, GMLP_WIDTH), row),
            pl.BlockSpec((tm, GMLP_WIDTH), row),
        ),
        scratch_shapes=[pltpu.VMEM((tm + 2 * HALO, IN_COLS), F32)] * 3 + [pltpu.VMEM((tm, GMLP_WIDTH), F32)],
        compiler_params=pltpu.CompilerParams(
            dimension_semantics=("arbitrary",), vmem_limit_bytes=VMEM_LIMIT),
        name="inproj",
    )(x2d, x2d, x2d, p["g1"], p["w_z"], p["w_xbc"], p["w_dt"], p["w_u"], p["w_v"],
      p["conv_w"], p["conv_b"], p["dt_bias"], p["ln_g"], p["ln_b"])


def _split2(q):
    hi = q.astype(BF16)
    lo = (q - hi.astype(F32)).astype(BF16)
    return hi, lo


def _ssd_role(xbc_ref, dt_ref, arow, tri_ref, rexp_ref, dskip_ref, s_ref, y_ref, *, backward):
    lane0 = HEADS if backward else 0
    dt_all = dt_ref[...]
    adt = dt_all * arow
    a1 = adt.astype(BF16)
    r1 = adt - a1.astype(F32)
    a2 = r1.astype(BF16)
    a3 = (r1 - a2.astype(F32)).astype(BF16)
    acs3 = _dot(tri_ref[...], jnp.concatenate([a1, a2, a3], axis=1))
    acs = acs3[:, 0:LANES] + acs3[:, LANES:2 * LANES] + acs3[:, 2 * LANES:]
    edge = 0 if backward else CHUNK - 1
    tot = acs[edge:edge + 1, :]
    decay = jnp.exp(tot - acs)
    eacs = jnp.exp(acs)
    acs2 = acs * LOG2_E
    acs2_t = acs2.T

    hi, lo = _split2(jnp.concatenate([dt_all, decay, eacs], axis=0))
    expanded = _dot(jnp.concatenate([hi, lo], axis=1), rexp_ref[...])
    dt_e = expanded[0:CHUNK]
    decay_e = expanded[CHUNK:2 * CHUNK]
    eacs_e = expanded[2 * CHUNK:]
    etot_e = eacs_e[edge:edge + 1, :]

    xf = xbc_ref[:, 0:SSD_WIDTH].astype(F32)
    xdt = xf * dt_e
    xdd = (xdt * decay_e).astype(BF16)
    xdt_b = xdt.astype(BF16)

    li = lax.broadcasted_iota(I32, (CHUNK, CHUNK), 0)
    si = lax.broadcasted_iota(I32, (CHUNK, CHUNK), 1)
    keep = (si >= li) if backward else (si <= li)
    left = si < HEAD_DIM
    zero_b = jnp.zeros((CHUNK, LANES), BF16)

    for g in range(GROUPS):
        b_g = xbc_ref[:, SSD_WIDTH + g * STATE:SSD_WIDTH + (g + 1) * STATE]
        c_g = xbc_ref[:, SSD_WIDTH + (GROUPS + g) * STATE:SSD_WIDTH + (GROUPS + g + 1) * STATE]
        b_t = b_g.astype(F32).T.astype(BF16)
        cb = _dot(c_g, b_t)
        gc0 = g * HEADS_PER_GROUP * HEAD_DIM
        gw = HEADS_PER_GROUP * HEAD_DIM
        s_old = s_ref[:, gc0:gc0 + gw]
        y_off = _dot(c_g, s_old.astype(BF16))
        s_new = _dot(b_t, xdd[:, gc0:gc0 + gw])
        s_ref[:, gc0:gc0 + gw] = s_old * etot_e[:, gc0:gc0 + gw] + s_new
        for pp in range(HEADS_PER_GROUP // 2):
            c0 = gc0 + pp * LANES
            ms = []
            for q in range(2):
                hl = lane0 + g * HEADS_PER_GROUP + 2 * pp + q
                seg = acs2[:, hl:hl + 1] - acs2_t[hl:hl + 1, :]
                lmat = jnp.exp2(jnp.where(keep, seg, -jnp.inf))
                ms.append((cb * lmat).astype(BF16))
            lhs = jnp.concatenate(ms, axis=1)
            xp = xdt_b[:, c0:c0 + LANES]
            rhs = jnp.concatenate([jnp.where(left, xp, zero_b), jnp.where(left, zero_b, xp)], axis=0)
            y = _dot(lhs, rhs) + y_off[:, pp * LANES:(pp + 1) * LANES] * eacs_e[:, c0:c0 + LANES]
            if not backward:
                y = y + xf[:, c0:c0 + LANES] * dskip_ref[:, c0:c0 + LANES]
            y_ref[:, c0:c0 + LANES] = y.astype(BF16)


def _ssd_kernel(xf_ref, dtf_ref, xb_ref, dtb_ref, alog_ref, trilo_ref, triup_ref, rf_ref, rb_ref,
                dskip_ref, yf_ref, yb_ref, sf_ref, sb_ref):
    @pl.when(pl.program_id(1) == 0)
    def _():
        sf_ref[...] = jnp.zeros_like(sf_ref)
        sb_ref[...] = jnp.zeros_like(sb_ref)

    arow = -jnp.exp(alog_ref[...])
    _ssd_role(xf_ref, dtf_ref, arow, trilo_ref, rf_ref, dskip_ref, sf_ref, yf_ref, backward=False)
    _ssd_role(xb_ref, dtb_ref, arow, triup_ref, rb_ref, dskip_ref, sb_ref, yb_ref, backward=True)


def _ssd(xbc, dt, batch, seq_len, p):
    t = xbc.shape[0]
    nc = seq_len // CHUNK
    fwd = lambda b, c: (b * nc + c, 0)
    bwd = lambda b, c: (b * nc + nc - 1 - c, 0)
    return pl.pallas_call(
        _ssd_kernel,
        out_shape=(jax.ShapeDtypeStruct((t, SSD_WIDTH), BF16), jax.ShapeDtypeStruct((t, SSD_WIDTH), BF16)),
        grid=(batch, nc),
        in_specs=[
            pl.BlockSpec((CHUNK, CONV_CH), fwd),
            pl.BlockSpec((CHUNK, LANES), fwd),
            pl.BlockSpec((CHUNK, CONV_CH), bwd),
            pl.BlockSpec((CHUNK, LANES), bwd),
            _const_spec((1, LANES)),
            _const_spec((CHUNK, CHUNK)),
            _const_spec((CHUNK, CHUNK)),
            _const_spec((2 * LANES, SSD_WIDTH)),
            _const_spec((2 * LANES, SSD_WIDTH)),
            _const_spec((1, SSD_WIDTH)),
        ],
        out_specs=(pl.BlockSpec((CHUNK, SSD_WIDTH), fwd), pl.BlockSpec((CHUNK, SSD_WIDTH), bwd)),
        scratch_shapes=[pltpu.VMEM((STATE, SSD_WIDTH), F32), pltpu.VMEM((STATE, SSD_WIDTH), F32)],
        compiler_params=pltpu.CompilerParams(
            dimension_semantics=("arbitrary", "arbitrary"), vmem_limit_bytes=VMEM_LIMIT),
        name="ssd",
    )(xbc, dt, xbc, dt, p["a_log"], p["tri_lo"], p["tri_up"], p["rexp_f"], p["rexp_b"], p["d_skip"])


def _outproj_kernel(yf_ref, yb_ref, sz_ref, u_ref, v_ref, x_ref, gs_ref, wsp_ref, bsp_ref, wo_ref,
                    g2_ref, wr_ref, rb_ref, tri_ref, trie_ref,
                    x1_ref, h2_ref, mt_ref, tm_ref,
                    gm_buf, carry):
    i = pl.program_id(0)
    tm = x_ref.shape[0]

    @pl.when(i == 0)
    def _():
        carry[...] = jnp.zeros_like(carry)

    gated = (yf_ref[...].astype(F32) + yb_ref[...].astype(F32)) * sz_ref[...].astype(F32)
    ms = jnp.mean(gated * gated, axis=-1, keepdims=True)
    ssd = (gated * lax.rsqrt(ms + RMS_EPS) * gs_ref[...]).astype(BF16)

    for c in range(tm // CHUNK):
        r0 = c * CHUNK
        for hh in range(GMLP_HEADS):
            c0 = hh * GMLP_HEAD_DIM
            sv = _dot(wsp_ref[hh], v_ref[r0:r0 + CHUNK, c0:c0 + GMLP_HEAD_DIM]) + bsp_ref[hh]
            gm_buf[r0:r0 + CHUNK, c0:c0 + GMLP_HEAD_DIM] = (
                u_ref[r0:r0 + CHUNK, c0:c0 + GMLP_HEAD_DIM].astype(F32) * sv).astype(BF16)

    x1 = x_ref[...] + _dot(ssd, wo_ref[0:SSD_WIDTH, :]) + _dot(gm_buf[...], wo_ref[SSD_WIDTH:, :])
    x1_ref[...] = x1

    ms2 = jnp.mean(x1 * x1, axis=-1, keepdims=True)
    h2 = (x1 * lax.rsqrt(ms2 + RMS_EPS) * g2_ref[...]).astype(BF16)
    h2_ref[...] = h2

    logits = _dot(h2, wr_ref[...]) + rb_ref[...]
    work = logits.T[0:N_EXPERTS, :]
    erow = lax.broadcasted_iota(I32, (N_EXPERTS, tm), 0).astype(F32)
    sel_all = jnp.zeros((N_EXPERTS, tm), F32)
    sels, vals = [], []
    for _ in range(TOP_K):
        m = jnp.max(work, axis=0, keepdims=True)
        idx = jnp.min(jnp.where(work == m, erow, float(N_EXPERTS)), axis=0, keepdims=True)
        sel = erow == idx
        sel_all = jnp.where(sel, 1.0, sel_all)
        work = jnp.where(sel, -jnp.inf, work)
        sels.append(sel)
        vals.append(m)
    exps = [jnp.exp(v - vals[0]) for v in vals]
    denom = exps[0] + exps[1] + exps[2] + exps[3]

    lrank = _dot(sel_all.astype(BF16), tri_ref[...])
    n_e = jnp.sum(sel_all, axis=1, keepdims=True)
    n_al = jnp.broadcast_to(jnp.floor((n_e + (SEG - 1)) * (1.0 / SEG)) * SEG, (N_EXPERTS, LANES))
    lstart = _dot(trie_ref[...], n_al.astype(BF16))
    old_cnt = carry[...]
    new_cnt = old_cnt + n_al
    carry[...] = new_cnt
    tile_row = jnp.concatenate([n_al, old_cnt, lstart, new_cnt], axis=0).T[0:1, :]
    tm_ref[0] = tile_row.astype(I32)

    local_row = lstart[:, 0:1] + lrank
    out_row = lax.broadcasted_iota(I32, (2 * TOP_K, tm), 0)
    meta = jnp.zeros((2 * TOP_K, tm), F32)
    for k in range(TOP_K):
        dk = jnp.sum(jnp.where(sels[k], local_row, 0.0), axis=0, keepdims=True)
        meta = jnp.where(out_row == k, dk, meta)
        meta = jnp.where(out_row == TOP_K + k, exps[k] / denom, meta)
    mt_ref[...] = meta


def _outproj(yf, yb, sz, u, v, x2d, p):
    t = x2d.shape[0]
    tm = TM_OUT
    row = lambda i: (i, 0)
    nt = t // tm
    out_shape = (
        jax.ShapeDtypeStruct((t, D_MODEL), F32),
        jax.ShapeDtypeStruct((t, D_MODEL), BF16),
        jax.ShapeDtypeStruct((nt * 2 * TOP_K, tm), F32),
        jax.ShapeDtypeStruct((nt, 1, LANES), I32),
    )
    bf_row = pl.BlockSpec((tm, D_MODEL), row)
    return pl.pallas_call(
        _outproj_kernel,
        out_shape=out_shape,
        grid=(t // tm,),
        in_specs=[
            bf_row, bf_row, bf_row, bf_row, bf_row, bf_row,
            _const_spec((1, SSD_WIDTH)),
            _const_spec((GMLP_HEADS, CHUNK, CHUNK)),
            _const_spec((GMLP_HEADS, CHUNK, GMLP_HEAD_DIM)),
            _const_spec((SSD_WIDTH + GMLP_WIDTH, D_MODEL)),
            _const_spec((1, D_MODEL)),
            _const_spec((D_MODEL, LANES)),
            _const_spec((1, LANES)),
            _const_spec((tm, tm)),
            _const_spec((N_EXPERTS, N_EXPERTS)),
        ],
        out_specs=(
            pl.BlockSpec((tm, D_MODEL), row),
            pl.BlockSpec((tm, D_MODEL), row),
            pl.BlockSpec((2 * TOP_K, tm), row),
            pl.BlockSpec((1, 1, LANES), lambda i: (i, 0, 0)),
        ),
        scratch_shapes=[pltpu.VMEM((tm, GMLP_WIDTH), BF16), pltpu.VMEM((N_EXPERTS, LANES), F32)],
        compiler_params=pltpu.CompilerParams(
            dimension_semantics=("arbitrary",), vmem_limit_bytes=VMEM_LIMIT),
        name="outproj",
    )(yf, yb, sz, u, v, x2d, p["gs"], p["w_sp"], p["b_sp"], p["w_out"], p["g2"], p["w_r"], p["b_r"],
      p["tok_before"], p["expert_before"])


def _copy_segment(make_copy, n, src_row, dst_row):
    big, mid, small = COPY_ROWS

    def body(c, carry_):
        make_copy(pl.multiple_of(src_row + c * big, SEG), pl.multiple_of(dst_row + c * big, SEG), big).start()
        return carry_
    nbig = n // big
    lax.fori_loop(0, nbig, body, 0)
    done = nbig * big

    @pl.when((n & mid) != 0)
    def _():
        make_copy(pl.multiple_of(src_row + done, SEG), pl.multiple_of(dst_row + done, SEG), mid).start()

    done2 = done + (n & mid)

    @pl.when((n & small) != 0)
    def _():
        make_copy(pl.multiple_of(src_row + done2, SEG), pl.multiple_of(dst_row + done2, SEG), small).start()


def _wait_rows(make_copy, nrows, limit=R_LOCAL):
    w = 1 << (limit.bit_length() - 1)
    while w >= SEG:
        @pl.when((nrows & w) != 0)
        def _(w=w):
            make_copy(0, 0, w).wait()
        w //= 2


def _onehot_rows(mt_ref, blk, values):
    tm = mt_ref.shape[1]
    riota = lax.broadcasted_iota(I32, (SORT_BLK, tm), 0).astype(F32).astype(BF16)
    out = jnp.zeros((SORT_BLK, tm), BF16)
    for k in reversed(range(TOP_K)):
        off = (mt_ref[k:k + 1, :] - float(blk * SORT_BLK)).astype(BF16)
        out = jnp.where(riota == off, values[k], out)
    return out


def _dispatch_kernel(ps_ref, tn_ref, tc_ref, tl_ref, cn_ref, nu_ref, h2_ref, mt_ref, xs_ref,
                     sbuf, zbuf, sem, nprev):
    i = pl.program_id(0)
    nt = pl.num_programs(0)
    tm = h2_ref.shape[0]
    slot = i % 2

    h2 = h2_ref[...]
    ones = [jnp.ones((1, tm), BF16)] * TOP_K
    for blk in range(R_LOCAL // SORT_BLK):
        s = _dot(_onehot_rows(mt_ref, blk, ones), h2)
        lo_bits = lax.bitcast_convert_type(s[:, 0:HALF], U32)
        hi_bits = lax.bitcast_convert_type(s[:, HALF:], U32)
        sbuf[slot, blk * SORT_BLK:(blk + 1) * SORT_BLK, :] = (lo_bits >> 16) | hi_bits

    def rows_copy(src_slot):
        def make(src_row, dst_row, rows):
            return pltpu.make_async_copy(sbuf.at[src_slot, pl.ds(src_row, rows)],
                                         xs_ref.at[pl.ds(dst_row, rows)], sem)
        return make

    @pl.when(i > 0)
    def _():
        _wait_rows(rows_copy(0), nprev[0])

    def per_expert(e, total):
        j = i * N_EXPERTS + e
        n = tn_ref[j]
        _copy_segment(rows_copy(slot), n, tl_ref[j], ps_ref[e] + tc_ref[j])
        return total + n

    total = lax.fori_loop(0, N_EXPERTS, per_expert, 0)
    nprev[0] = total

    @pl.when(i == nt - 1)
    def _():
        _wait_rows(rows_copy(0), total)
        zbuf[...] = jnp.zeros_like(zbuf)

        def zero_copy(src_row, dst_row, rows):
            return pltpu.make_async_copy(zbuf.at[pl.ds(src_row, rows)], xs_ref.at[pl.ds(dst_row, rows)], sem)

        def tail_rows(e):
            return (cn_ref[e] + (BM - 1)) // BM * BM - cn_ref[e]

        def start_tail(e, c):
            _copy_segment(zero_copy, tail_rows(e), 0, ps_ref[e] + cn_ref[e])
            return c
        lax.fori_loop(0, N_EXPERTS, start_tail, 0)

        def start_block(b, c):
            zero_copy(0, pl.multiple_of(b * BM, BM), BM).start()
            return c
        n_rows_total = xs_ref.shape[0]
        lax.fori_loop(nu_ref[0], n_rows_total // BM, start_block, 0)

        def wait_tail(e, c):
            _wait_rows(zero_copy, tail_rows(e), limit=BM - SEG)
            return c
        lax.fori_loop(0, N_EXPERTS, wait_tail, 0)

        def wait_block(b, c):
            zero_copy(0, 0, BM).wait()
            return c
        lax.fori_loop(nu_ref[0], n_rows_total // BM, wait_block, 0)


def _dispatch(pad_start, tn, tc, tl, counts, n_used, h2, mt, rows):
    t = h2.shape[0]
    tm = TM_OUT
    grid_spec = pltpu.PrefetchScalarGridSpec(
        num_scalar_prefetch=6,
        grid=(t // tm,),
        in_specs=[
            pl.BlockSpec((tm, D_MODEL), lambda i, *_: (i, 0)),
            pl.BlockSpec((2 * TOP_K, tm), lambda i, *_: (i, 0)),
        ],
        out_specs=pl.BlockSpec(memory_space=pl.ANY),
        scratch_shapes=[pltpu.VMEM((2, R_LOCAL, HALF), U32), pltpu.VMEM((BM, HALF), U32),
                        pltpu.SemaphoreType.DMA(()), pltpu.SMEM((1,), I32)],
    )
    return pl.pallas_call(
        _dispatch_kernel,
        out_shape=jax.ShapeDtypeStruct((rows, HALF), U32),
        grid_spec=grid_spec,
        compiler_params=pltpu.CompilerParams(
            dimension_semantics=("arbitrary",), vmem_limit_bytes=VMEM_LIMIT, has_side_effects=True),
        name="dispatch",
    )(pad_start, tn, tc, tl, counts, n_used, h2, mt)


def _experts_kernel(be_ref, nu_ref, xs_ref, wgu_ref, bgu_ref, wd_ref, bd_ref, ys_ref):
    i = pl.program_id(0)

    @pl.when(i < nu_ref[0])
    def _():
        lo, hi = _unpack_bf16_pair(xs_ref[...])
        xb = jnp.concatenate([lo.astype(BF16), hi.astype(BF16)], axis=1)
        gu = _dot(xb, wgu_ref[0]) + bgu_ref[0]
        gate = jnp.minimum(gu[:, 0:D_FF], SWIGLU_LIMIT)
        up = jnp.clip(gu[:, D_FF:], -SWIGLU_LIMIT, SWIGLU_LIMIT)
        act = (up + 1.0) * (gate * jax.nn.sigmoid(gate * SWIGLU_ALPHA))
        y = _dot(act.astype(BF16), wd_ref[0]) + bd_ref[0]
        ys_ref[...] = _pack_bf16_pair(y[:, 0:HALF], y[:, HALF:])

    @pl.when(i >= nu_ref[0])
    def _():
        ys_ref[...] = jnp.zeros_like(ys_ref)


def _experts(block_e, n_used, xs, p):
    rows = xs.shape[0]
    nb = rows // BM
    grid_spec = pltpu.PrefetchScalarGridSpec(
        num_scalar_prefetch=2,
        grid=(nb,),
        in_specs=[
            pl.BlockSpec((BM, HALF), lambda i, be, *_: (i, 0)),
            pl.BlockSpec((1, D_MODEL, 2 * D_FF), lambda i, be, *_: (be[i], 0, 0)),
            pl.BlockSpec((1, 1, 2 * D_FF), lambda i, be, *_: (be[i], 0, 0)),
            pl.BlockSpec((1, D_FF, D_MODEL), lambda i, be, *_: (be[i], 0, 0)),
            pl.BlockSpec((1, 1, D_MODEL), lambda i, be, *_: (be[i], 0, 0)),
        ],
        out_specs=pl.BlockSpec((BM, HALF), lambda i, be, *_: (i, 0)),
    )
    return pl.pallas_call(
        _experts_kernel,
        out_shape=jax.ShapeDtypeStruct((rows, HALF), U32),
        grid_spec=grid_spec,
        compiler_params=pltpu.CompilerParams(
            dimension_semantics=("arbitrary",), vmem_limit_bytes=VMEM_LIMIT),
        name="experts",
    )(block_e, n_used, xs, p["w_gu"], p["b_gu"], p["w_d"], p["b_d"])


def _combine_kernel(ps_ref, tn_ref, tc_ref, tl_ref, mt_ref, x1_ref, gf_ref, ys_ref, o_ref,
                    ybuf, gt_buf, ylo_buf, yhi_buf, sems):
    i = pl.program_id(0)
    nt = pl.num_programs(0)
    tm = x1_ref.shape[0]

    def rows_copy(slot):
        def make(src_row, dst_row, rows):
            return pltpu.make_async_copy(ys_ref.at[pl.ds(src_row, rows)],
                                         ybuf.at[slot, pl.ds(dst_row, rows)], sems.at[slot])
        return make

    def issue(tile, slot):
        def per_expert(e, c0):
            j = tile * N_EXPERTS + e
            _copy_segment(rows_copy(slot), tn_ref[j], ps_ref[e] + tc_ref[j], tl_ref[j])
            return c0
        lax.fori_loop(0, N_EXPERTS, per_expert, 0)

    @pl.when(i == 0)
    def _():
        ybuf[...] = jnp.zeros_like(ybuf)
        issue(0, 0)

    @pl.when(i + 1 < nt)
    def _():
        issue(i + 1, (i + 1) % 2)

    slot = i % 2
    n_mine = lax.fori_loop(0, N_EXPERTS, lambda e, c: c + tn_ref[i * N_EXPERTS + e], 0)
    _wait_rows(rows_copy(slot), n_mine)

    gates = [mt_ref[TOP_K + k:TOP_K + k + 1, :].astype(BF16) for k in range(TOP_K)]
    for blk in range(R_LOCAL // SORT_BLK):
        rows = slice(blk * SORT_BLK, (blk + 1) * SORT_BLK)
        gt_buf[rows, :] = _onehot_rows(mt_ref, blk, gates)
        lo, hi = _unpack_bf16_pair(ybuf[slot, rows, :])
        ylo_buf[rows, :] = lo.astype(BF16)
        yhi_buf[rows, :] = hi.astype(BF16)
    contract_rows = (((0,), (0,)), ((), ()))
    g_t = gt_buf[...]
    x1 = x1_ref[...]
    acc = jnp.concatenate(
        [x1[:, 0:HALF] + lax.dot_general(g_t, ylo_buf[...], contract_rows, preferred_element_type=F32),
         x1[:, HALF:] + lax.dot_general(g_t, yhi_buf[...], contract_rows, preferred_element_type=F32)], axis=1)
    ms = jnp.mean(acc * acc, axis=-1, keepdims=True)
    o_ref[...] = acc * lax.rsqrt(ms + RMS_EPS) * gf_ref[...]


def _combine(pad_start, tn, tc, tl, mt, x1, ys, p):
    t = x1.shape[0]
    tm = TM_OUT
    grid_spec = pltpu.PrefetchScalarGridSpec(
        num_scalar_prefetch=4,
        grid=(t // tm,),
        in_specs=[
            pl.BlockSpec((2 * TOP_K, tm), lambda i, *_: (i, 0)),
            pl.BlockSpec((tm, D_MODEL), lambda i, *_: (i, 0)),
            pl.BlockSpec((1, D_MODEL), lambda i, *_: (0, 0)),
            pl.BlockSpec(memory_space=pl.ANY),
        ],
        out_specs=pl.BlockSpec((tm, D_MODEL), lambda i, *_: (i, 0)),
        scratch_shapes=[pltpu.VMEM((2, R_LOCAL, HALF), U32), pltpu.VMEM((R_LOCAL, tm), BF16),
                        pltpu.VMEM((R_LOCAL, HALF), BF16), pltpu.VMEM((R_LOCAL, HALF), BF16),
                        pltpu.SemaphoreType.DMA((2,))],
    )
    return pl.pallas_call(
        _combine_kernel,
        out_shape=jax.ShapeDtypeStruct((t, D_MODEL), F32),
        grid_spec=grid_spec,
        compiler_params=pltpu.CompilerParams(
            dimension_semantics=("arbitrary",), vmem_limit_bytes=VMEM_LIMIT),
        name="combine",
    )(pad_start, tn, tc, tl, mt, x1, p["gf"], ys)


def _prepare_params(norm1_g, w_in, conv_w, conv_b, dt_bias_f, dt_bias_b, a_log_f, a_log_b, d_skip,
                    ssd_norm_g, gmlp_ln_g, gmlp_ln_b, w_spatial, b_spatial, w_out, norm2_g,
                    router_w, router_b, w_gate_up, b_gate_up, w_down, b_down, final_g):
    z_end = SSD_WIDTH
    xbc_end = z_end + CONV_CH
    dt_end = xbc_end + 2 * HEADS
    u_end = dt_end + GMLP_WIDTH
    w = w_in[0]
    pad_lanes = lambda a, fill=0.0: jnp.pad(a, ((0, 0), (0, LANES - a.shape[1])), constant_values=fill)
    head_lane = jnp.arange(LANES)[:, None]
    chan_head = (jnp.arange(SSD_WIDTH) // HEAD_DIM)[None, :]
    ar = jnp.arange(CHUNK)
    art = jnp.arange(TM_OUT)
    are = jnp.arange(N_EXPERTS)
    return {
        "g1": norm1_g[0][None, :],
        "w_z": w[:, :z_end].astype(BF16),
        "w_xbc": w[:, z_end:xbc_end].astype(BF16),
        "w_dt": pad_lanes(w[:, xbc_end:dt_end]).astype(BF16),
        "w_u": w[:, dt_end:u_end].astype(BF16),
        "w_v": w[:, u_end:].astype(BF16),
        "conv_w": jnp.pad(conv_w[0], ((0, 8 - CONV_WIDTH), (0, 0))),
        "conv_b": conv_b[0][None, :],
        "dt_bias": pad_lanes(jnp.concatenate([dt_bias_f[0], dt_bias_b[0]])[None, :]),
        "ln_g": gmlp_ln_g[0][None, :],
        "ln_b": gmlp_ln_b[0][None, :],
        "a_log": pad_lanes(jnp.concatenate([a_log_f[0], a_log_b[0]])[None, :]),
        "tri_lo": (ar[:, None] >= ar[None, :]).astype(BF16),
        "tri_up": (ar[:, None] <= ar[None, :]).astype(BF16),
        "rexp_f": jnp.tile((head_lane == chan_head).astype(BF16), (2, 1)),
        "rexp_b": jnp.tile((head_lane == chan_head + HEADS).astype(BF16), (2, 1)),
        "d_skip": jnp.repeat(d_skip[0], HEAD_DIM)[None, :],
        "gs": ssd_norm_g[0][None, :],
        "w_sp": w_spatial[0].astype(BF16),
        "b_sp": jnp.broadcast_to(b_spatial[0][:, :, None], (GMLP_HEADS, CHUNK, GMLP_HEAD_DIM)),
        "w_out": w_out[0].astype(BF16),
        "g2": norm2_g[0][None, :],
        "w_r": pad_lanes(router_w[0]).astype(BF16),
        "b_r": pad_lanes(router_b[0][None, :], NEG_BIG),
        "tok_before": (art[:, None] < art[None, :]).astype(BF16),
        "expert_before": (are[None, :] < are[:, None]).astype(BF16),
        "w_gu": w_gate_up[0].astype(BF16),
        "b_gu": b_gate_up[0][:, None, :],
        "w_d": w_down[0].astype(BF16),
        "b_d": b_down[0][:, None, :],
        "gf": final_g[None, :],
    }


def _encoder(x, p):
    batch, seq_len, _ = x.shape
    t = batch * seq_len
    x2d = x.reshape(t, D_MODEL)
    sz, xbc, dt, u, v = _inproj(x2d, seq_len, p)
    yf, yb = _ssd(xbc, dt, batch, seq_len, p)
    x1, h2, mt, tile_rows = _outproj(yf, yb, sz, u, v, x2d, p)
    tile_rows = tile_rows[:, 0, :]
    tn, tc, tl = (tile_rows[:, k * N_EXPERTS:(k + 1) * N_EXPERTS].reshape(-1) for k in range(3))

    counts = tile_rows[-1, 3 * N_EXPERTS:]
    padded = (counts + BM - 1) // BM * BM
    pad_end = jnp.cumsum(padded)
    pad_start = (pad_end - padded).astype(I32)
    n_tiles = t // TM_OUT
    n_blocks = -(-(t * TOP_K + n_tiles * N_EXPERTS * (SEG - 1) + N_EXPERTS * (BM - 1)) // BM)
    n_used = (pad_end[-1] // BM).astype(I32)
    blk_row = jnp.minimum(jnp.arange(n_blocks, dtype=I32), n_used - 1) * BM
    onehot_e = (pad_end[None, :] <= blk_row[:, None]).astype(I32)
    block_e = jnp.minimum(jnp.sum(onehot_e, axis=1), N_EXPERTS - 1).astype(I32)
    n_used = n_used.reshape(1)

    xs = _dispatch(pad_start, tn, tc, tl, counts, n_used, h2, mt, n_blocks * BM)
    ys = _experts(block_e, n_used, xs, p)
    y = _combine(pad_start, tn, tc, tl, mt, x1, ys, p)
    return y.reshape(batch, seq_len, D_MODEL)


def kernel(x_prompt, x_sample, norm1_g, w_in, conv_w, conv_b, dt_bias_f, dt_bias_b, a_log_f, a_log_b,
           d_skip, ssd_norm_g, gmlp_ln_g, gmlp_ln_b, w_spatial, b_spatial, w_out, norm2_g, router_w,
           router_b, w_gate_up, b_gate_up, w_down, b_down, final_g):
    p = _prepare_params(norm1_g, w_in, conv_w, conv_b, dt_bias_f, dt_bias_b, a_log_f, a_log_b, d_skip,
                        ssd_norm_g, gmlp_ln_g, gmlp_ln_b, w_spatial, b_spatial, w_out, norm2_g,
                        router_w, router_b, w_gate_up, b_gate_up, w_down, b_down, final_g)
    return (_encoder(x_prompt, p), _encoder(x_sample, p))
```

```python
import functools

import jax
import jax.numpy as jnp
from jax import lax
from jax.experimental import pallas as pl
from jax.experimental.pallas import tpu as pltpu

F32 = jnp.float32
BF16 = jnp.bfloat16
U32 = jnp.uint32
I32 = jnp.int32

D_MODEL = 1024
SSD_WIDTH = 1024
HEAD_DIM = 64
HEADS = 16
GROUPS = 2
HEADS_PER_GROUP = 8
STATE = 128
CHUNK = 128
CONV_WIDTH = 5
CONV_HALF = 2
CONV_CH = SSD_WIDTH + 2 * GROUPS * STATE
GMLP_WIDTH = 1024
GMLP_HEADS = 8
GMLP_HEAD_DIM = 128
N_EXPERTS = 32
TOP_K = 4
D_FF = 1024
SWIGLU_LIMIT = 7.0
SWIGLU_ALPHA = 1.702
RMS_EPS = 1e-5
LN_EPS = 1e-5

LANES = 128
HALO = 16
TM_IN = 512
IN_COLS = 256
IN_ROWS = 128
TM_OUT = 512
BM = 512
SEG = 8
COPY_ROWS = (32, 16, 8)
SORT_BLK = 256
R_LOCAL = -(-(TM_OUT * TOP_K + N_EXPERTS * (SEG - 1)) // SORT_BLK) * SORT_BLK
ROW_TABLE = 1024
SPARE_BLOCKS = -(-R_LOCAL // BM)
HALF = D_MODEL // 2
VMEM_LIMIT = 56 * 1024 * 1024
NEG_BIG = -1e30
LOG2_E = 1.4426950408889634


def _dot(a, b):
    return jnp.dot(a, b, preferred_element_type=F32)


def _gelu(x):
    return 0.5 * x * (1.0 + lax.erf(x * (2.0 ** -0.5)))


def _pack_bf16_pair(lo, hi):
    lo_bits = lax.bitcast_convert_type(lo.astype(BF16).astype(F32), U32)
    hi_bits = lax.bitcast_convert_type(hi.astype(BF16).astype(F32), U32)
    return (lo_bits >> 16) | hi_bits


def _unpack_bf16_pair(w):
    lo = lax.bitcast_convert_type(w << 16, F32)
    hi = lax.bitcast_convert_type(w & jnp.uint32(0xFFFF0000), F32)
    return lo, hi


def _const_spec(shape):
    nd = len(shape)
    return pl.BlockSpec(shape, lambda *_: (0,) * nd)


def _inproj_kernel(x_ref, xp_ref, xn_ref, g1_ref, wz_ref, wxbc_ref, wdt_ref, wu_ref, wv_ref,
                   cw_ref, cb_ref, dtb_ref, lng_ref, lnb_ref,
                   sz_ref, xbc_ref, dt_ref, u_ref, v_ref, buf0, buf1, buf2, gbuf, *, tiles_per_seq):
    i = pl.program_id(0)
    tm = x_ref.shape[0]
    g1 = g1_ref[...]
    row_blocks = [(r0, IN_ROWS) for r0 in range(0, tm, IN_ROWS)]

    def norm(xv):
        ms = jnp.mean(xv * xv, axis=-1, keepdims=True)
        return xv * lax.rsqrt(ms + RMS_EPS) * g1

    keep_prev = jnp.where(i % tiles_per_seq == 0, 0.0, 1.0)
    keep_next = jnp.where(i % tiles_per_seq == tiles_per_seq - 1, 0.0, 1.0)
    h = norm(x_ref[...]).astype(BF16)
    h_all = jnp.concatenate([(norm(xp_ref[...]) * keep_prev).astype(BF16), h,
                             (norm(xn_ref[...]) * keep_next).astype(BF16)], axis=0)

    def silu_to(out_ref):
        def epilogue(buf, c0, wd):
            for r0, nr in row_blocks:
                r = buf[r0:r0 + nr, 0:wd]
                out_ref[r0:r0 + nr, c0:c0 + wd] = (r * jax.nn.sigmoid(r)).astype(BF16)
        return epilogue

    def conv_silu(buf, c0, wd):
        for r0, nr in row_blocks:
            acc = jnp.broadcast_to(cb_ref[:, c0:c0 + wd], (nr, wd))
            for k in range(CONV_WIDTH):
                acc = acc + cw_ref[k:k + 1, c0:c0 + wd] * buf[pl.ds(r0 + HALO - CONV_HALF + k, nr), 0:wd]
            xbc_ref[r0:r0 + nr, c0:c0 + wd] = (acc * jax.nn.sigmoid(acc)).astype(BF16)

    def softplus_dt(buf, c0, wd):
        dt_ref[...] = jax.nn.softplus(buf[0:tm, 0:wd] + dtb_ref[...])

    def gelu_u(buf, c0, wd):
        for r0, nr in row_blocks:
            u_ref[r0:r0 + nr, c0:c0 + wd] = _gelu(buf[r0:r0 + nr, 0:wd]).astype(BF16)

    sums = [jnp.zeros((nr, LANES), F32) for _, nr in row_blocks]

    def gelu_v(buf, c0, wd):
        for b, (r0, nr) in enumerate(row_blocks):
            g = _gelu(buf[r0:r0 + nr, 0:wd])
            gbuf[r0:r0 + nr, c0:c0 + wd] = g
            for l0 in range(0, wd, LANES):
                sums[b] = sums[b] + g[:, l0:l0 + LANES]

    heavy = [(h_all, wxbc_ref, c0, IN_COLS, conv_silu) for c0 in range(0, CONV_CH, IN_COLS)]
    light = [(h, wz_ref, c0, IN_COLS, silu_to(sz_ref)) for c0 in range(0, SSD_WIDTH, IN_COLS)]
    light += [(h, wu_ref, c0, IN_COLS, gelu_u) for c0 in range(0, GMLP_WIDTH, IN_COLS)]
    light += [(h, wv_ref, c0, IN_COLS, gelu_v) for c0 in range(0, GMLP_WIDTH, IN_COLS)]
    light.append((h, wdt_ref, 0, LANES, softplus_dt))
    stages = []
    while heavy or light:
        if heavy:
            stages.append(heavy.pop(0))
        if light:
            stages.append(light.pop(0))
    bufs = (buf0, buf1, buf2)
    ahead = len(bufs) - 1

    def matmul(j):
        lhs, w_ref, c0, wd, _ = stages[j]
        bufs[j % len(bufs)][0:lhs.shape[0], 0:wd] = _dot(lhs, w_ref[:, c0:c0 + wd])

    for j in range(ahead):
        matmul(j)
    for j, (_, _, c0, wd, epilogue) in enumerate(stages):
        if j + ahead < len(stages):
            matmul(j + ahead)
        epilogue(bufs[j % len(bufs)], c0, wd)

    inv_n = 1.0 / GMLP_WIDTH
    for b, (r0, nr) in enumerate(row_blocks):
        mu = jnp.sum(sums[b], axis=-1, keepdims=True) * inv_n
        sq = jnp.zeros((nr, LANES), F32)
        for l0 in range(0, GMLP_WIDTH, LANES):
            d = gbuf[r0:r0 + nr, l0:l0 + LANES] - mu
            sq = sq + d * d
        scale = lax.rsqrt(jnp.sum(sq, axis=-1, keepdims=True) * inv_n + LN_EPS)
        for c0 in range(0, GMLP_WIDTH, IN_COLS):
            g = gbuf[r0:r0 + nr, c0:c0 + IN_COLS]
            v_ref[r0:r0 + nr, c0:c0 + IN_COLS] = (
                (g - mu) * scale * lng_ref[:, c0:c0 + IN_COLS] + lnb_ref[:, c0:c0 + IN_COLS]).astype(BF16)


def _inproj(x2d, seq_len, p):
    t = x2d.shape[0]
    tm = TM_IN
    tiles_per_seq = seq_len // tm
    nh = tm // HALO
    last_halo = t // HALO - 1
    kern = functools.partial(_inproj_kernel, tiles_per_seq=tiles_per_seq)
    row = lambda i: (i, 0)
    out_shape = (
        jax.ShapeDtypeStruct((t, SSD_WIDTH), BF16),
        jax.ShapeDtypeStruct((t, CONV_CH), BF16),
        jax.ShapeDtypeStruct((t, LANES), F32),
        jax.ShapeDtypeStruct((t, GMLP_WIDTH), BF16),
        jax.ShapeDtypeStruct((t, GMLP_WIDTH), BF16),
    )
    return pl.pallas_call(
        kern,
        out_shape=out_shape,
        grid=(t // tm,),
        in_specs=[
            pl.BlockSpec((tm, D_MODEL), row),
            pl.BlockSpec((HALO, D_MODEL), lambda i: (jnp.maximum(i * nh - 1, 0), 0)),
            pl.BlockSpec((HALO, D_MODEL), lambda i: (jnp.minimum((i + 1) * nh, last_halo), 0)),
            _const_spec((1, D_MODEL)),
            _const_spec((D_MODEL, SSD_WIDTH)),
            _const_spec((D_MODEL, CONV_CH)),
            _const_spec((D_MODEL, LANES)),
            _const_spec((D_MODEL, GMLP_WIDTH)),
            _const_spec((D_MODEL, GMLP_WIDTH)),
            _const_spec((8, CONV_CH)),
            _const_spec((1, CONV_CH)),
            _const_spec((1, LANES)),
            _const_spec((1, GMLP_WIDTH)),
            _const_spec((1, GMLP_WIDTH)),
        ],
        out_specs=(
            pl.BlockSpec((tm, SSD_WIDTH), row),
            pl.BlockSpec((tm, CONV_CH), row),
            pl.BlockSpec((tm, LANES), row),
            pl.BlockSpec((tm, GMLP_WIDTH), row),
            pl.BlockSpec((tm, GMLP_WIDTH), row),
        ),
        scratch_shapes=[pltpu.VMEM((tm + 2 * HALO, IN_COLS), F32)] * 3 + [pltpu.VMEM((tm, GMLP_WIDTH), F32)],
        compiler_params=pltpu.CompilerParams(
            dimension_semantics=("arbitrary",), vmem_limit_bytes=VMEM_LIMIT),
        name="inproj",
    )(x2d, x2d, x2d, p["g1"], p["w_z"], p["w_xbc"], p["w_dt"], p["w_u"], p["w_v"],
      p["conv_w"], p["conv_b"], p["dt_bias"], p["ln_g"], p["ln_b"])


def _split2(q):
    hi = q.astype(BF16)
    lo = (q - hi.astype(F32)).astype(BF16)
    return hi, lo


def _ssd_role(xbc_ref, dt_ref, arow, tri_ref, rexp_ref, dskip_ref, s_ref, y_ref, *, backward):
    lane0 = HEADS if backward else 0
    dt_all = dt_ref[...]
    adt = dt_all * arow
    a1 = adt.astype(BF16)
    r1 = adt - a1.astype(F32)
    a2 = r1.astype(BF16)
    a3 = (r1 - a2.astype(F32)).astype(BF16)
    acs3 = _dot(tri_ref[...], jnp.concatenate([a1, a2, a3], axis=1))
    acs = acs3[:, 0:LANES] + acs3[:, LANES:2 * LANES] + acs3[:, 2 * LANES:]
    edge = 0 if backward else CHUNK - 1
    tot = acs[edge:edge + 1, :]
    decay = jnp.exp(tot - acs)
    eacs = jnp.exp(acs)
    acs2 = acs * LOG2_E
    acs2_t = acs2.T

    hi, lo = _split2(jnp.concatenate([dt_all, decay, eacs], axis=0))
    expanded = _dot(jnp.concatenate([hi, lo], axis=1), rexp_ref[...])
    dt_e = expanded[0:CHUNK]
    decay_e = expanded[CHUNK:2 * CHUNK]
    eacs_e = expanded[2 * CHUNK:]
    etot_e = eacs_e[edge:edge + 1, :]

    xf = xbc_ref[:, 0:SSD_WIDTH].astype(F32)
    xdt = xf * dt_e
    xdd = (xdt * decay_e).astype(BF16)
    xdt_b = xdt.astype(BF16)

    li = lax.broadcasted_iota(I32, (CHUNK, CHUNK), 0)
    si = lax.broadcasted_iota(I32, (CHUNK, CHUNK), 1)
    keep = (si >= li) if backward else (si <= li)
    left = si < HEAD_DIM
    zero_b = jnp.zeros((CHUNK, LANES), BF16)

    for g in range(GROUPS):
        b_g = xbc_ref[:, SSD_WIDTH + g * STATE:SSD_WIDTH + (g + 1) * STATE]
        c_g = xbc_ref[:, SSD_WIDTH + (GROUPS + g) * STATE:SSD_WIDTH + (GROUPS + g + 1) * STATE]
        b_t = b_g.astype(F32).T.astype(BF16)
        cb = _dot(c_g, b_t)
        gc0 = g * HEADS_PER_GROUP * HEAD_DIM
        gw = HEADS_PER_GROUP * HEAD_DIM
        s_old = s_ref[:, gc0:gc0 + gw]
        y_off = _dot(c_g, s_old.astype(BF16))
        s_new = _dot(b_t, xdd[:, gc0:gc0 + gw])
        s_ref[:, gc0:gc0 + gw] = s_old * etot_e[:, gc0:gc0 + gw] + s_new
        for pp in range(HEADS_PER_GROUP // 2):
            c0 = gc0 + pp * LANES
            ms = []
            for q in range(2):
                hl = lane0 + g * HEADS_PER_GROUP + 2 * pp + q
                seg = acs2[:, hl:hl + 1] - acs2_t[hl:hl + 1, :]
                lmat = jnp.exp2(jnp.where(keep, seg, -jnp.inf))
                ms.append((cb * lmat).astype(BF16))
            lhs = jnp.concatenate(ms, axis=1)
            xp = xdt_b[:, c0:c0 + LANES]
            rhs = jnp.concatenate([jnp.where(left, xp, zero_b), jnp.where(left, zero_b, xp)], axis=0)
            y = _dot(lhs, rhs) + y_off[:, pp * LANES:(pp + 1) * LANES] * eacs_e[:, c0:c0 + LANES]
            if not backward:
                y = y + xf[:, c0:c0 + LANES] * dskip_ref[:, c0:c0 + LANES]
            y_ref[:, c0:c0 + LANES] = y.astype(BF16)


def _ssd_kernel(xf_ref, dtf_ref, xb_ref, dtb_ref, alog_ref, trilo_ref, triup_ref, rf_ref, rb_ref,
                dskip_ref, yf_ref, yb_ref, sf_ref, sb_ref):
    @pl.when(pl.program_id(1) == 0)
    def _():
        sf_ref[...] = jnp.zeros_like(sf_ref)
        sb_ref[...] = jnp.zeros_like(sb_ref)

    arow = -jnp.exp(alog_ref[...])
    _ssd_role(xf_ref, dtf_ref, arow, trilo_ref, rf_ref, dskip_ref, sf_ref, yf_ref, backward=False)
    _ssd_role(xb_ref, dtb_ref, arow, triup_ref, rb_ref, dskip_ref, sb_ref, yb_ref, backward=True)


def _ssd(xbc, dt, batch, seq_len, p):
    t = xbc.shape[0]
    nc = seq_len // CHUNK
    fwd = lambda b, c: (b * nc + c, 0)
    bwd = lambda b, c: (b * nc + nc - 1 - c, 0)
    return pl.pallas_call(
        _ssd_kernel,
        out_shape=(jax.ShapeDtypeStruct((t, SSD_WIDTH), BF16), jax.ShapeDtypeStruct((t, SSD_WIDTH), BF16)),
        grid=(batch, nc),
        in_specs=[
            pl.BlockSpec((CHUNK, CONV_CH), fwd),
            pl.BlockSpec((CHUNK, LANES), fwd),
            pl.BlockSpec((CHUNK, CONV_CH), bwd),
            pl.BlockSpec((CHUNK, LANES), bwd),
            _const_spec((1, LANES)),
            _const_spec((CHUNK, CHUNK)),
            _const_spec((CHUNK, CHUNK)),
            _const_spec((2 * LANES, SSD_WIDTH)),
            _const_spec((2 * LANES, SSD_WIDTH)),
            _const_spec((1, SSD_WIDTH)),
        ],
        out_specs=(pl.BlockSpec((CHUNK, SSD_WIDTH), fwd), pl.BlockSpec((CHUNK, SSD_WIDTH), bwd)),
        scratch_shapes=[pltpu.VMEM((STATE, SSD_WIDTH), F32), pltpu.VMEM((STATE, SSD_WIDTH), F32)],
        compiler_params=pltpu.CompilerParams(
            dimension_semantics=("arbitrary", "arbitrary"), vmem_limit_bytes=VMEM_LIMIT),
        name="ssd",
    )(xbc, dt, xbc, dt, p["a_log"], p["tri_lo"], p["tri_up"], p["rexp_f"], p["rexp_b"], p["d_skip"])


def _outproj_kernel(yf_ref, yb_ref, sz_ref, u_ref, v_ref, x_ref, gs_ref, wsp_ref, bsp_ref, wo_ref,
                    g2_ref, wr_ref, rb_ref, tri_ref, trie_ref,
                    x1_ref, h2_ref, mt_ref, tm_ref,
                    gm_buf, carry):
    i = pl.program_id(0)
    tm = x_ref.shape[0]

    @pl.when(i == 0)
    def _():
        carry[...] = jnp.zeros_like(carry)

    gated = (yf_ref[...].astype(F32) + yb_ref[...].astype(F32)) * sz_ref[...].astype(F32)
    ms = jnp.mean(gated * gated, axis=-1, keepdims=True)
    ssd = (gated * lax.rsqrt(ms + RMS_EPS) * gs_ref[...]).astype(BF16)

    for c in range(tm // CHUNK):
        r0 = c * CHUNK
        for hh in range(GMLP_HEADS):
            c0 = hh * GMLP_HEAD_DIM
            sv = _dot(wsp_ref[hh], v_ref[r0:r0 + CHUNK, c0:c0 + GMLP_HEAD_DIM]) + bsp_ref[hh]
            gm_buf[r0:r0 + CHUNK, c0:c0 + GMLP_HEAD_DIM] = (
                u_ref[r0:r0 + CHUNK, c0:c0 + GMLP_HEAD_DIM].astype(F32) * sv).astype(BF16)

    x1 = x_ref[...] + _dot(ssd, wo_ref[0:SSD_WIDTH, :]) + _dot(gm_buf[...], wo_ref[SSD_WIDTH:, :])
    x1_ref[...] = x1

    ms2 = jnp.mean(x1 * x1, axis=-1, keepdims=True)
    h2 = (x1 * lax.rsqrt(ms2 + RMS_EPS) * g2_ref[...]).astype(BF16)
    h2_ref[...] = h2

    logits = _dot(h2, wr_ref[...]) + rb_ref[...]
    work = logits.T[0:N_EXPERTS, :]
    erow = lax.broadcasted_iota(I32, (N_EXPERTS, tm), 0).astype(F32)
    sel_all = jnp.zeros((N_EXPERTS, tm), F32)
    sels, vals = [], []
    for _ in range(TOP_K):
        m = jnp.max(work, axis=0, keepdims=True)
        idx = jnp.min(jnp.where(work == m, erow, float(N_EXPERTS)), axis=0, keepdims=True)
        sel = erow == idx
        sel_all = jnp.where(sel, 1.0, sel_all)
        work = jnp.where(sel, -jnp.inf, work)
        sels.append(sel)
        vals.append(m)
    exps = [jnp.exp(v - vals[0]) for v in vals]
    denom = exps[0] + exps[1] + exps[2] + exps[3]

    lrank = _dot(sel_all.astype(BF16), tri_ref[...])
    n_e = jnp.sum(sel_all, axis=1, keepdims=True)
    n_al = jnp.broadcast_to(jnp.floor((n_e + (SEG - 1)) * (1.0 / SEG)) * SEG, (N_EXPERTS, LANES))
    lstart = _dot(trie_ref[...], n_al.astype(BF16))
    old_cnt = carry[...]
    new_cnt = old_cnt + n_al
    carry[...] = new_cnt
    tile_row = jnp.concatenate([n_al, old_cnt, lstart, new_cnt], axis=0).T[0:1, :]
    tm_ref[0] = tile_row.astype(I32)

    local_row = lstart[:, 0:1] + lrank
    out_row = lax.broadcasted_iota(I32, (2 * TOP_K, tm), 0)
    meta = jnp.zeros((2 * TOP_K, tm), F32)
    for k in range(TOP_K):
        dk = jnp.sum(jnp.where(sels[k], local_row, 0.0), axis=0, keepdims=True)
        meta = jnp.where(out_row == k, dk, meta)
        meta = jnp.where(out_row == TOP_K + k, exps[k] / denom, meta)
    mt_ref[...] = meta


def _outproj(yf, yb, sz, u, v, x2d, p):
    t = x2d.shape[0]
    tm = TM_OUT
    row = lambda i: (i, 0)
    nt = t // tm
    out_shape = (
        jax.ShapeDtypeStruct((t, D_MODEL), F32),
        jax.ShapeDtypeStruct((t, D_MODEL), BF16),
        jax.ShapeDtypeStruct((nt * 2 * TOP_K, tm), F32),
        jax.ShapeDtypeStruct((nt, 1, LANES), I32),
    )
    bf_row = pl.BlockSpec((tm, D_MODEL), row)
    return pl.pallas_call(
        _outproj_kernel,
        out_shape=out_shape,
        grid=(t // tm,),
        in_specs=[
            bf_row, bf_row, bf_row, bf_row, bf_row, bf_row,
            _const_spec((1, SSD_WIDTH)),
            _const_spec((GMLP_HEADS, CHUNK, CHUNK)),
            _const_spec((GMLP_HEADS, CHUNK, GMLP_HEAD_DIM)),
            _const_spec((SSD_WIDTH + GMLP_WIDTH, D_MODEL)),
            _const_spec((1, D_MODEL)),
            _const_spec((D_MODEL, LANES)),
            _const_spec((1, LANES)),
            _const_spec((tm, tm)),
            _const_spec((N_EXPERTS, N_EXPERTS)),
        ],
        out_specs=(
            pl.BlockSpec((tm, D_MODEL), row),
            pl.BlockSpec((tm, D_MODEL), row),
            pl.BlockSpec((2 * TOP_K, tm), row),
            pl.BlockSpec((1, 1, LANES), lambda i: (i, 0, 0)),
        ),
        scratch_shapes=[pltpu.VMEM((tm, GMLP_WIDTH), BF16), pltpu.VMEM((N_EXPERTS, LANES), F32)],
        compiler_params=pltpu.CompilerParams(
            dimension_semantics=("arbitrary",), vmem_limit_bytes=VMEM_LIMIT),
        name="outproj",
    )(yf, yb, sz, u, v, x2d, p["gs"], p["w_sp"], p["b_sp"], p["w_out"], p["g2"], p["w_r"], p["b_r"],
      p["tok_before"], p["expert_before"])


def _copy_segment(make_copy, n, src_row, dst_row):
    big, mid, small = COPY_ROWS

    def body(c, carry_):
        make_copy(pl.multiple_of(src_row + c * big, SEG), pl.multiple_of(dst_row + c * big, SEG), big).start()
        return carry_
    nbig = n // big
    lax.fori_loop(0, nbig, body, 0)
    done = nbig * big

    @pl.when((n & mid) != 0)
    def _():
        make_copy(pl.multiple_of(src_row + done, SEG), pl.multiple_of(dst_row + done, SEG), mid).start()

    done2 = done + (n & mid)

    @pl.when((n & small) != 0)
    def _():
        make_copy(pl.multiple_of(src_row + done2, SEG), pl.multiple_of(dst_row + done2, SEG), small).start()


def _wait_rows(make_copy, nrows, limit=R_LOCAL):
    w = 1 << (limit.bit_length() - 1)
    while w >= SEG:
        @pl.when((nrows & w) != 0)
        def _(w=w):
            make_copy(0, 0, w).wait()
        w //= 2


def _onehot_rows(mt_ref, blk, values):
    tm = mt_ref.shape[1]
    riota = lax.broadcasted_iota(I32, (SORT_BLK, tm), 0).astype(F32).astype(BF16)
    out = jnp.zeros((SORT_BLK, tm), BF16)
    for k in reversed(range(TOP_K)):
        off = (mt_ref[k:k + 1, :] - float(blk * SORT_BLK)).astype(BF16)
        out = jnp.where(riota == off, values[k], out)
    return out


def _dispatch_kernel(ps_ref, cn_ref, nu_ref, dst_ref, h2_ref, mt_ref, xs_ref, sbuf, zbuf, sem):
    i = pl.program_id(0)
    nt = pl.num_programs(0)
    tm = h2_ref.shape[0]
    groups_per_blk = SORT_BLK // SEG

    def wait_tile():
        pltpu.make_async_copy(sbuf, xs_ref.at[pl.ds(0, R_LOCAL)], sem).wait()

    h2 = h2_ref[...]
    ones = [jnp.ones((1, tm), BF16)] * TOP_K
    for blk in range(R_LOCAL // SORT_BLK):
        s = _dot(_onehot_rows(mt_ref, blk, ones), h2)
        lo_bits = lax.bitcast_convert_type(s[:, 0:HALF], U32)
        hi_bits = lax.bitcast_convert_type(s[:, HALF:], U32)
        packed = (lo_bits >> 16) | hi_bits
        if blk == 0:
            @pl.when(i > 0)
            def _():
                wait_tile()
        sbuf[blk * SORT_BLK:(blk + 1) * SORT_BLK, :] = packed
        for g in range(blk * groups_per_blk, (blk + 1) * groups_per_blk):
            pltpu.make_async_copy(sbuf.at[pl.ds(g * SEG, SEG)],
                                  xs_ref.at[pl.ds(pl.multiple_of(dst_ref[g], SEG), SEG)], sem).start()

    @pl.when(i == nt - 1)
    def _():
        wait_tile()
        zbuf[...] = jnp.zeros_like(zbuf)

        def zero_copy(src_row, dst_row, rows):
            return pltpu.make_async_copy(zbuf.at[pl.ds(src_row, rows)], xs_ref.at[pl.ds(dst_row, rows)], sem)

        def tail_rows(e):
            return (cn_ref[e] + (BM - 1)) // BM * BM - cn_ref[e]

        def start_tail(e, c):
            _copy_segment(zero_copy, tail_rows(e), 0, ps_ref[e] + cn_ref[e])
            return c
        lax.fori_loop(0, N_EXPERTS, start_tail, 0)

        def start_block(b, c):
            zero_copy(0, pl.multiple_of(b * BM, BM), BM).start()
            return c
        n_rows_total = xs_ref.shape[0]
        lax.fori_loop(nu_ref[0], n_rows_total // BM, start_block, 0)

        def wait_tail(e, c):
            _wait_rows(zero_copy, tail_rows(e), limit=BM - SEG)
            return c
        lax.fori_loop(0, N_EXPERTS, wait_tail, 0)

        def wait_block(b, c):
            zero_copy(0, 0, BM).wait()
            return c
        lax.fori_loop(nu_ref[0], n_rows_total // BM, wait_block, 0)


def _dispatch(pad_start, counts, n_used, dst_table, h2, mt, rows):
    t = h2.shape[0]
    tm = TM_OUT
    grid_spec = pltpu.PrefetchScalarGridSpec(
        num_scalar_prefetch=3,
        grid=(t // tm,),
        in_specs=[
            pl.BlockSpec((ROW_TABLE,), lambda i, *_: (i,), memory_space=pltpu.SMEM),
            pl.BlockSpec((tm, D_MODEL), lambda i, *_: (i, 0)),
            pl.BlockSpec((2 * TOP_K, tm), lambda i, *_: (i, 0)),
        ],
        out_specs=pl.BlockSpec(memory_space=pl.ANY),
        scratch_shapes=[pltpu.VMEM((R_LOCAL, HALF), U32), pltpu.VMEM((BM, HALF), U32),
                        pltpu.SemaphoreType.DMA(())],
    )
    return pl.pallas_call(
        _dispatch_kernel,
        out_shape=jax.ShapeDtypeStruct((rows, HALF), U32),
        grid_spec=grid_spec,
        compiler_params=pltpu.CompilerParams(
            dimension_semantics=("arbitrary",), vmem_limit_bytes=VMEM_LIMIT, has_side_effects=True),
        name="dispatch",
    )(pad_start, counts, n_used, dst_table, h2, mt)


def _experts_kernel(be_ref, nu_ref, xs_ref, wgu_ref, bgu_ref, wd_ref, bd_ref, ys_ref):
    i = pl.program_id(0)

    @pl.when(i < nu_ref[0])
    def _():
        lo, hi = _unpack_bf16_pair(xs_ref[...])
        xb = jnp.concatenate([lo.astype(BF16), hi.astype(BF16)], axis=1)
        gu = _dot(xb, wgu_ref[0]) + bgu_ref[0]
        gate = jnp.minimum(gu[:, 0:D_FF], SWIGLU_LIMIT)
        up = jnp.clip(gu[:, D_FF:], -SWIGLU_LIMIT, SWIGLU_LIMIT)
        act = (up + 1.0) * (gate * jax.nn.sigmoid(gate * SWIGLU_ALPHA))
        y = _dot(act.astype(BF16), wd_ref[0]) + bd_ref[0]
        ys_ref[...] = _pack_bf16_pair(y[:, 0:HALF], y[:, HALF:])

    @pl.when(i >= nu_ref[0])
    def _():
        ys_ref[...] = jnp.zeros_like(ys_ref)


def _experts(block_e, n_used, xs, p):
    rows = xs.shape[0]
    nb = rows // BM
    grid_spec = pltpu.PrefetchScalarGridSpec(
        num_scalar_prefetch=2,
        grid=(nb,),
        in_specs=[
            pl.BlockSpec((BM, HALF), lambda i, be, *_: (i, 0)),
            pl.BlockSpec((1, D_MODEL, 2 * D_FF), lambda i, be, *_: (be[i], 0, 0)),
            pl.BlockSpec((1, 1, 2 * D_FF), lambda i, be, *_: (be[i], 0, 0)),
            pl.BlockSpec((1, D_FF, D_MODEL), lambda i, be, *_: (be[i], 0, 0)),
            pl.BlockSpec((1, 1, D_MODEL), lambda i, be, *_: (be[i], 0, 0)),
        ],
        out_specs=pl.BlockSpec((BM, HALF), lambda i, be, *_: (i, 0)),
    )
    return pl.pallas_call(
        _experts_kernel,
        out_shape=jax.ShapeDtypeStruct((rows, HALF), U32),
        grid_spec=grid_spec,
        compiler_params=pltpu.CompilerParams(
            dimension_semantics=("arbitrary",), vmem_limit_bytes=VMEM_LIMIT),
        name="experts",
    )(block_e, n_used, xs, p["w_gu"], p["b_gu"], p["w_d"], p["b_d"])


def _combine_kernel(cur_ref, nxt_ref, mt_ref, x1_ref, gf_ref, ys_ref, o_ref,
                    ybuf, gt_buf, ylo_buf, yhi_buf, sems):
    i = pl.program_id(0)
    nt = pl.num_programs(0)

    def issue(table_ref, slot):
        for g in range(R_LOCAL // SEG):
            pltpu.make_async_copy(ys_ref.at[pl.ds(pl.multiple_of(table_ref[g], SEG), SEG)],
                                  ybuf.at[slot, pl.ds(g * SEG, SEG)], sems.at[slot]).start()

    def wait_slot(slot):
        pltpu.make_async_copy(ys_ref.at[pl.ds(0, R_LOCAL)], ybuf.at[slot], sems.at[slot]).wait()

    @pl.when(i == 0)
    def _():
        issue(cur_ref, 0)

    slot = i % 2
    issue(nxt_ref, 1 - slot)

    gates = [mt_ref[TOP_K + k:TOP_K + k + 1, :].astype(BF16) for k in range(TOP_K)]
    for blk in range(R_LOCAL // SORT_BLK):
        rows = slice(blk * SORT_BLK, (blk + 1) * SORT_BLK)
        gt_buf[rows, :] = _onehot_rows(mt_ref, blk, gates)
    wait_slot(slot)
    for blk in range(R_LOCAL // SORT_BLK):
        rows = slice(blk * SORT_BLK, (blk + 1) * SORT_BLK)
        lo, hi = _unpack_bf16_pair(ybuf[slot, rows, :])
        ylo_buf[rows, :] = lo.astype(BF16)
        yhi_buf[rows, :] = hi.astype(BF16)
    contract_rows = (((0,), (0,)), ((), ()))
    g_t = gt_buf[...]
    x1 = x1_ref[...]
    acc = jnp.concatenate(
        [x1[:, 0:HALF] + lax.dot_general(g_t, ylo_buf[...], contract_rows, preferred_element_type=F32),
         x1[:, HALF:] + lax.dot_general(g_t, yhi_buf[...], contract_rows, preferred_element_type=F32)], axis=1)
    ms = jnp.mean(acc * acc, axis=-1, keepdims=True)
    o_ref[...] = acc * lax.rsqrt(ms + RMS_EPS) * gf_ref[...]

    @pl.when(i == nt - 1)
    def _():
        wait_slot(1 - slot)


def _combine(src_table, mt, x1, ys, p):
    t = x1.shape[0]
    tm = TM_OUT
    nt = t // tm
    grid_spec = pltpu.PrefetchScalarGridSpec(
        num_scalar_prefetch=0,
        grid=(nt,),
        in_specs=[
            pl.BlockSpec((ROW_TABLE,), lambda i: (i,), memory_space=pltpu.SMEM),
            pl.BlockSpec((ROW_TABLE,), lambda i: (jnp.minimum(i + 1, nt - 1),), memory_space=pltpu.SMEM),
            pl.BlockSpec((2 * TOP_K, tm), lambda i: (i, 0)),
            pl.BlockSpec((tm, D_MODEL), lambda i: (i, 0)),
            pl.BlockSpec((1, D_MODEL), lambda i: (0, 0)),
            pl.BlockSpec(memory_space=pl.ANY),
        ],
        out_specs=pl.BlockSpec((tm, D_MODEL), lambda i: (i, 0)),
        scratch_shapes=[pltpu.VMEM((2, R_LOCAL, HALF), U32), pltpu.VMEM((R_LOCAL, tm), BF16),
                        pltpu.VMEM((R_LOCAL, HALF), BF16), pltpu.VMEM((R_LOCAL, HALF), BF16),
                        pltpu.SemaphoreType.DMA((2,))],
    )
    return pl.pallas_call(
        _combine_kernel,
        out_shape=jax.ShapeDtypeStruct((t, D_MODEL), F32),
        grid_spec=grid_spec,
        compiler_params=pltpu.CompilerParams(
            dimension_semantics=("arbitrary",), vmem_limit_bytes=VMEM_LIMIT),
        name="combine",
    )(src_table, src_table, mt, x1, p["gf"], ys)


def _prepare_params(norm1_g, w_in, conv_w, conv_b, dt_bias_f, dt_bias_b, a_log_f, a_log_b, d_skip,
                    ssd_norm_g, gmlp_ln_g, gmlp_ln_b, w_spatial, b_spatial, w_out, norm2_g,
                    router_w, router_b, w_gate_up, b_gate_up, w_down, b_down, final_g):
    z_end = SSD_WIDTH
    xbc_end = z_end + CONV_CH
    dt_end = xbc_end + 2 * HEADS
    u_end = dt_end + GMLP_WIDTH
    w = w_in[0]
    pad_lanes = lambda a, fill=0.0: jnp.pad(a, ((0, 0), (0, LANES - a.shape[1])), constant_values=fill)
    head_lane = jnp.arange(LANES)[:, None]
    chan_head = (jnp.arange(SSD_WIDTH) // HEAD_DIM)[None, :]
    ar = jnp.arange(CHUNK)
    art = jnp.arange(TM_OUT)
    are = jnp.arange(N_EXPERTS)
    return {
        "g1": norm1_g[0][None, :],
        "w_z": w[:, :z_end].astype(BF16),
        "w_xbc": w[:, z_end:xbc_end].astype(BF16),
        "w_dt": pad_lanes(w[:, xbc_end:dt_end]).astype(BF16),
        "w_u": w[:, dt_end:u_end].astype(BF16),
        "w_v": w[:, u_end:].astype(BF16),
        "conv_w": jnp.pad(conv_w[0], ((0, 8 - CONV_WIDTH), (0, 0))),
        "conv_b": conv_b[0][None, :],
        "dt_bias": pad_lanes(jnp.concatenate([dt_bias_f[0], dt_bias_b[0]])[None, :]),
        "ln_g": gmlp_ln_g[0][None, :],
        "ln_b": gmlp_ln_b[0][None, :],
        "a_log": pad_lanes(jnp.concatenate([a_log_f[0], a_log_b[0]])[None, :]),
        "tri_lo": (ar[:, None] >= ar[None, :]).astype(BF16),
        "tri_up": (ar[:, None] <= ar[None, :]).astype(BF16),
        "rexp_f": jnp.tile((head_lane == chan_head).astype(BF16), (2, 1)),
        "rexp_b": jnp.tile((head_lane == chan_head + HEADS).astype(BF16), (2, 1)),
        "d_skip": jnp.repeat(d_skip[0], HEAD_DIM)[None, :],
        "gs": ssd_norm_g[0][None, :],
        "w_sp": w_spatial[0].astype(BF16),
        "b_sp": jnp.broadcast_to(b_spatial[0][:, :, None], (GMLP_HEADS, CHUNK, GMLP_HEAD_DIM)),
        "w_out": w_out[0].astype(BF16),
        "g2": norm2_g[0][None, :],
        "w_r": pad_lanes(router_w[0]).astype(BF16),
        "b_r": pad_lanes(router_b[0][None, :], NEG_BIG),
        "tok_before": (art[:, None] < art[None, :]).astype(BF16),
        "expert_before": (are[None, :] < are[:, None]).astype(BF16),
        "w_gu": w_gate_up[0].astype(BF16),
        "b_gu": b_gate_up[0][:, None, :],
        "w_d": w_down[0].astype(BF16),
        "b_d": b_down[0][:, None, :],
        "gf": final_g[None, :],
    }


def _encoder(x, p):
    batch, seq_len, _ = x.shape
    t = batch * seq_len
    x2d = x.reshape(t, D_MODEL)
    sz, xbc, dt, u, v = _inproj(x2d, seq_len, p)
    yf, yb = _ssd(xbc, dt, batch, seq_len, p)
    x1, h2, mt, tile_rows = _outproj(yf, yb, sz, u, v, x2d, p)
    tile_rows = tile_rows[:, 0, :]
    tn, tc, tl = (tile_rows[:, k * N_EXPERTS:(k + 1) * N_EXPERTS] for k in range(3))

    counts = tile_rows[-1, 3 * N_EXPERTS:]
    padded = (counts + BM - 1) // BM * BM
    pad_end = jnp.cumsum(padded)
    pad_start = (pad_end - padded).astype(I32)
    n_tiles = t // TM_OUT
    n_blocks = -(-(t * TOP_K + n_tiles * N_EXPERTS * (SEG - 1) + N_EXPERTS * (BM - 1)) // BM)
    n_used = (pad_end[-1] // BM).astype(I32)
    blk_row = jnp.minimum(jnp.arange(n_blocks + SPARE_BLOCKS, dtype=I32), n_used - 1) * BM
    onehot_e = (pad_end[None, :] <= blk_row[:, None]).astype(I32)
    block_e = jnp.minimum(jnp.sum(onehot_e, axis=1), N_EXPERTS - 1).astype(I32)
    n_used = n_used.reshape(1)

    group_row = jnp.arange(R_LOCAL // SEG, dtype=I32) * SEG
    owner = jnp.sum(((tl + tn)[:, None, :] <= group_row[None, :, None]).astype(I32), axis=-1)
    is_owner = (owner[:, :, None] == jnp.arange(N_EXPERTS, dtype=I32)[None, None, :]).astype(I32)
    base = jnp.sum(is_owner * (pad_start[None, :] + tc - tl)[:, None, :], axis=-1)
    used = owner < N_EXPERTS
    pad_table = lambda a: jnp.pad(a, ((0, 0), (0, ROW_TABLE - a.shape[1]))).reshape(-1)
    dst_table = pad_table(jnp.where(used, base + group_row[None, :], n_blocks * BM + group_row[None, :]))
    src_table = pad_table(jnp.where(used, base + group_row[None, :], 0))

    xs = _dispatch(pad_start, counts, n_used, dst_table, h2, mt, (n_blocks + SPARE_BLOCKS) * BM)
    ys = _experts(block_e, n_used, xs, p)
    y = _combine(src_table, mt, x1, ys, p)
    return y.reshape(batch, seq_len, D_MODEL)


def kernel(x_prompt, x_sample, norm1_g, w_in, conv_w, conv_b, dt_bias_f, dt_bias_b, a_log_f, a_log_b,
           d_skip, ssd_norm_g, gmlp_ln_g, gmlp_ln_b, w_spatial, b_spatial, w_out, norm2_g, router_w,
           router_b, w_gate_up, b_gate_up, w_down, b_down, final_g):
    p = _prepare_params(norm1_g, w_in, conv_w, conv_b, dt_bias_f, dt_bias_b, a_log_f, a_log_b, d_skip,
                        ssd_norm_g, gmlp_ln_g, gmlp_ln_b, w_spatial, b_spatial, w_out, norm2_g,
                        router_w, router_b, w_gate_up, b_gate_up, w_down, b_down, final_g)
    return (_encoder(x_prompt, p), _encoder(x_sample, p))
```

```python
import functools

import jax
import jax.numpy as jnp
from jax import lax
from jax.experimental import pallas as pl
from jax.experimental.pallas import tpu as pltpu

F32 = jnp.float32
BF16 = jnp.bfloat16
U32 = jnp.uint32
I32 = jnp.int32

D_MODEL = 1024
SSD_WIDTH = 1024
HEAD_DIM = 64
HEADS = 16
GROUPS = 2
HEADS_PER_GROUP = 8
STATE = 128
CHUNK = 128
CONV_WIDTH = 5
CONV_HALF = 2
CONV_CH = SSD_WIDTH + 2 * GROUPS * STATE
GMLP_WIDTH = 1024
GMLP_HEADS = 8
GMLP_HEAD_DIM = 128
N_EXPERTS = 32
TOP_K = 4
D_FF = 1024
SWIGLU_LIMIT = 7.0
SWIGLU_ALPHA = 1.702
RMS_EPS = 1e-5
LN_EPS = 1e-5

LANES = 128
HALO = 16
SSD_CHUNKS_PER_STEP = 4
TM_IN = 512
IN_COLS = 256
IN_ROWS = 128
TM_OUT = 512
BM = 512
SEG = 8
COPY_ROWS = (32, 16, 8)
SORT_BLK = 256
R_LOCAL = -(-(TM_OUT * TOP_K + N_EXPERTS * (SEG - 1)) // SORT_BLK) * SORT_BLK
ROW_TABLE = 1024
SPARE_BLOCKS = -(-R_LOCAL // BM)
HALF = D_MODEL // 2
VMEM_LIMIT = 56 * 1024 * 1024
NEG_BIG = -1e30
LOG2_E = 1.4426950408889634


def _dot(a, b):
    return jnp.dot(a, b, preferred_element_type=F32)


def _gelu(x):
    return 0.5 * x * (1.0 + lax.erf(x * (2.0 ** -0.5)))


def _pack_bf16_pair(lo, hi):
    lo_bits = lax.bitcast_convert_type(lo.astype(BF16).astype(F32), U32)
    hi_bits = lax.bitcast_convert_type(hi.astype(BF16).astype(F32), U32)
    return (lo_bits >> 16) | hi_bits


def _unpack_bf16_pair(w):
    lo = lax.bitcast_convert_type(w << 16, F32)
    hi = lax.bitcast_convert_type(w & jnp.uint32(0xFFFF0000), F32)
    return lo, hi


def _const_spec(shape):
    nd = len(shape)
    return pl.BlockSpec(shape, lambda *_: (0,) * nd)


def _inproj_kernel(x_ref, xp_ref, xn_ref, g1_ref, wz_ref, wxbc_ref, wdt_ref, wu_ref, wv_ref,
                   cw_ref, cb_ref, dtb_ref, lng_ref, lnb_ref,
                   sz_ref, xbc_ref, dt_ref, u_ref, v_ref, buf0, buf1, buf2, gbuf, *, tiles_per_seq):
    i = pl.program_id(0)
    tm = x_ref.shape[0]
    g1 = g1_ref[...]
    row_blocks = [(r0, IN_ROWS) for r0 in range(0, tm, IN_ROWS)]

    def norm(xv):
        ms = jnp.mean(xv * xv, axis=-1, keepdims=True)
        return xv * lax.rsqrt(ms + RMS_EPS) * g1

    keep_prev = jnp.where(i % tiles_per_seq == 0, 0.0, 1.0)
    keep_next = jnp.where(i % tiles_per_seq == tiles_per_seq - 1, 0.0, 1.0)
    h = norm(x_ref[...]).astype(BF16)
    h_all = jnp.concatenate([(norm(xp_ref[...]) * keep_prev).astype(BF16), h,
                             (norm(xn_ref[...]) * keep_next).astype(BF16)], axis=0)

    def silu_to(out_ref):
        def epilogue(buf, c0, wd):
            for r0, nr in row_blocks:
                r = buf[r0:r0 + nr, 0:wd]
                out_ref[r0:r0 + nr, c0:c0 + wd] = (r * jax.nn.sigmoid(r)).astype(BF16)
        return epilogue

    def conv_silu(buf, c0, wd):
        for r0, nr in row_blocks:
            acc = jnp.broadcast_to(cb_ref[:, c0:c0 + wd], (nr, wd))
            for k in range(CONV_WIDTH):
                acc = acc + cw_ref[k:k + 1, c0:c0 + wd] * buf[pl.ds(r0 + HALO - CONV_HALF + k, nr), 0:wd]
            xbc_ref[r0:r0 + nr, c0:c0 + wd] = (acc * jax.nn.sigmoid(acc)).astype(BF16)

    def softplus_dt(buf, c0, wd):
        dt_ref[...] = jax.nn.softplus(buf[0:tm, 0:wd] + dtb_ref[...])

    def gelu_u(buf, c0, wd):
        for r0, nr in row_blocks:
            u_ref[r0:r0 + nr, c0:c0 + wd] = _gelu(buf[r0:r0 + nr, 0:wd]).astype(BF16)

    sums = [jnp.zeros((nr, LANES), F32) for _, nr in row_blocks]

    def gelu_v(buf, c0, wd):
        for b, (r0, nr) in enumerate(row_blocks):
            g = _gelu(buf[r0:r0 + nr, 0:wd])
            gbuf[r0:r0 + nr, c0:c0 + wd] = g
            for l0 in range(0, wd, LANES):
                sums[b] = sums[b] + g[:, l0:l0 + LANES]

    heavy = [(h_all, wxbc_ref, c0, IN_COLS, conv_silu) for c0 in range(0, CONV_CH, IN_COLS)]
    light = [(h, wz_ref, c0, IN_COLS, silu_to(sz_ref)) for c0 in range(0, SSD_WIDTH, IN_COLS)]
    light += [(h, wu_ref, c0, IN_COLS, gelu_u) for c0 in range(0, GMLP_WIDTH, IN_COLS)]
    light += [(h, wv_ref, c0, IN_COLS, gelu_v) for c0 in range(0, GMLP_WIDTH, IN_COLS)]
    light.append((h, wdt_ref, 0, LANES, softplus_dt))
    stages = []
    while heavy or light:
        if heavy:
            stages.append(heavy.pop(0))
        if light:
            stages.append(light.pop(0))
    bufs = (buf0, buf1, buf2)
    ahead = len(bufs) - 1

    def matmul(j):
        lhs, w_ref, c0, wd, _ = stages[j]
        bufs[j % len(bufs)][0:lhs.shape[0], 0:wd] = _dot(lhs, w_ref[:, c0:c0 + wd])

    for j in range(ahead):
        matmul(j)
    for j, (_, _, c0, wd, epilogue) in enumerate(stages):
        if j + ahead < len(stages):
            matmul(j + ahead)
        epilogue(bufs[j % len(bufs)], c0, wd)

    inv_n = 1.0 / GMLP_WIDTH
    for b, (r0, nr) in enumerate(row_blocks):
        mu = jnp.sum(sums[b], axis=-1, keepdims=True) * inv_n
        sq = jnp.zeros((nr, LANES), F32)
        for l0 in range(0, GMLP_WIDTH, LANES):
            d = gbuf[r0:r0 + nr, l0:l0 + LANES] - mu
            sq = sq + d * d
        scale = lax.rsqrt(jnp.sum(sq, axis=-1, keepdims=True) * inv_n + LN_EPS)
        for c0 in range(0, GMLP_WIDTH, IN_COLS):
            g = gbuf[r0:r0 + nr, c0:c0 + IN_COLS]
            v_ref[r0:r0 + nr, c0:c0 + IN_COLS] = (
                (g - mu) * scale * lng_ref[:, c0:c0 + IN_COLS] + lnb_ref[:, c0:c0 + IN_COLS]).astype(BF16)


def _inproj(x2d, seq_len, p):
    t = x2d.shape[0]
    tm = TM_IN
    tiles_per_seq = seq_len // tm
    nh = tm // HALO
    last_halo = t // HALO - 1
    kern = functools.partial(_inproj_kernel, tiles_per_seq=tiles_per_seq)
    row = lambda i: (i, 0)
    out_shape = (
        jax.ShapeDtypeStruct((t, SSD_WIDTH), BF16),
        jax.ShapeDtypeStruct((t, CONV_CH), BF16),
        jax.ShapeDtypeStruct((t, LANES), F32),
        jax.ShapeDtypeStruct((t, GMLP_WIDTH), BF16),
        jax.ShapeDtypeStruct((t, GMLP_WIDTH), BF16),
    )
    return pl.pallas_call(
        kern,
        out_shape=out_shape,
        grid=(t // tm,),
        in_specs=[
            pl.BlockSpec((tm, D_MODEL), row),
            pl.BlockSpec((HALO, D_MODEL), lambda i: (jnp.maximum(i * nh - 1, 0), 0)),
            pl.BlockSpec((HALO, D_MODEL), lambda i: (jnp.minimum((i + 1) * nh, last_halo), 0)),
            _const_spec((1, D_MODEL)),
            _const_spec((D_MODEL, SSD_WIDTH)),
            _const_spec((D_MODEL, CONV_CH)),
            _const_spec((D_MODEL, LANES)),
            _const_spec((D_MODEL, GMLP_WIDTH)),
            _const_spec((D_MODEL, GMLP_WIDTH)),
            _const_spec((8, CONV_CH)),
            _const_spec((1, CONV_CH)),
            _const_spec((1, LANES)),
            _const_spec((1, GMLP_WIDTH)),
            _const_spec((1, GMLP_WIDTH)),
        ],
        out_specs=(
            pl.BlockSpec((tm, SSD_WIDTH), row),
            pl.BlockSpec((tm, CONV_CH), row),
            pl.BlockSpec((tm, LANES), row),
            pl.BlockSpec((tm, GMLP_WIDTH), row),
            pl.BlockSpec((tm, GMLP_WIDTH), row),
        ),
        scratch_shapes=[pltpu.VMEM((tm + 2 * HALO, IN_COLS), F32)] * 3 + [pltpu.VMEM((tm, GMLP_WIDTH), F32)],
        compiler_params=pltpu.CompilerParams(
            dimension_semantics=("arbitrary",), vmem_limit_bytes=VMEM_LIMIT),
        name="inproj",
    )(x2d, x2d, x2d, p["g1"], p["w_z"], p["w_xbc"], p["w_dt"], p["w_u"], p["w_v"],
      p["conv_w"], p["conv_b"], p["dt_bias"], p["ln_g"], p["ln_b"])


def _split2(q):
    hi = q.astype(BF16)
    lo = (q - hi.astype(F32)).astype(BF16)
    return hi, lo


def _ssd_role(xbc_ref, dt_ref, arow, tri_ref, rexp_ref, dskip_ref, s_ref, y_ref, *, r0, backward):
    lane0 = HEADS if backward else 0
    rows = slice(r0, r0 + CHUNK)
    dt_all = dt_ref[rows, :]
    adt = dt_all * arow
    a1 = adt.astype(BF16)
    r1 = adt - a1.astype(F32)
    a2 = r1.astype(BF16)
    a3 = (r1 - a2.astype(F32)).astype(BF16)
    acs3 = _dot(tri_ref[...], jnp.concatenate([a1, a2, a3], axis=1))
    acs = acs3[:, 0:LANES] + acs3[:, LANES:2 * LANES] + acs3[:, 2 * LANES:]
    edge = 0 if backward else CHUNK - 1
    tot = acs[edge:edge + 1, :]
    decay = jnp.exp(tot - acs)
    eacs = jnp.exp(acs)
    acs2 = acs * LOG2_E
    acs2_t = acs2.T

    hi, lo = _split2(jnp.concatenate([dt_all, decay, eacs], axis=0))
    expanded = _dot(jnp.concatenate([hi, lo], axis=1), rexp_ref[...])
    dt_e = expanded[0:CHUNK]
    decay_e = expanded[CHUNK:2 * CHUNK]
    eacs_e = expanded[2 * CHUNK:]
    etot_e = eacs_e[edge:edge + 1, :]

    xf = xbc_ref[rows, 0:SSD_WIDTH].astype(F32)
    xdt = xf * dt_e
    xdd = (xdt * decay_e).astype(BF16)
    xdt_b = xdt.astype(BF16)

    li = lax.broadcasted_iota(I32, (CHUNK, CHUNK), 0)
    si = lax.broadcasted_iota(I32, (CHUNK, CHUNK), 1)
    keep = (si >= li) if backward else (si <= li)
    left = si < HEAD_DIM
    zero_b = jnp.zeros((CHUNK, LANES), BF16)

    for g in range(GROUPS):
        b_g = xbc_ref[rows, SSD_WIDTH + g * STATE:SSD_WIDTH + (g + 1) * STATE]
        c_g = xbc_ref[rows, SSD_WIDTH + (GROUPS + g) * STATE:SSD_WIDTH + (GROUPS + g + 1) * STATE]
        b_t = b_g.astype(F32).T.astype(BF16)
        cb = _dot(c_g, b_t)
        gc0 = g * HEADS_PER_GROUP * HEAD_DIM
        gw = HEADS_PER_GROUP * HEAD_DIM
        s_old = s_ref[:, gc0:gc0 + gw]
        y_off = _dot(c_g, s_old.astype(BF16))
        s_new = _dot(b_t, xdd[:, gc0:gc0 + gw])
        s_ref[:, gc0:gc0 + gw] = s_old * etot_e[:, gc0:gc0 + gw] + s_new
        for pp in range(HEADS_PER_GROUP // 2):
            c0 = gc0 + pp * LANES
            ms = []
            for q in range(2):
                hl = lane0 + g * HEADS_PER_GROUP + 2 * pp + q
                seg = acs2[:, hl:hl + 1] - acs2_t[hl:hl + 1, :]
                lmat = jnp.exp2(jnp.where(keep, seg, -jnp.inf))
                ms.append((cb * lmat).astype(BF16))
            lhs = jnp.concatenate(ms, axis=1)
            xp = xdt_b[:, c0:c0 + LANES]
            rhs = jnp.concatenate([jnp.where(left, xp, zero_b), jnp.where(left, zero_b, xp)], axis=0)
            y = _dot(lhs, rhs) + y_off[:, pp * LANES:(pp + 1) * LANES] * eacs_e[:, c0:c0 + LANES]
            if not backward:
                y = y + xf[:, c0:c0 + LANES] * dskip_ref[:, c0:c0 + LANES]
            y_ref[rows, c0:c0 + LANES] = y.astype(BF16)


def _ssd_kernel(xf_ref, dtf_ref, xb_ref, dtb_ref, alog_ref, trilo_ref, triup_ref, rf_ref, rb_ref,
                dskip_ref, yf_ref, yb_ref, sf_ref, sb_ref):
    @pl.when(pl.program_id(1) == 0)
    def _():
        sf_ref[...] = jnp.zeros_like(sf_ref)
        sb_ref[...] = jnp.zeros_like(sb_ref)

    arow = -jnp.exp(alog_ref[...])
    n_sub = xf_ref.shape[0] // CHUNK
    for j in range(n_sub):
        _ssd_role(xf_ref, dtf_ref, arow, trilo_ref, rf_ref, dskip_ref, sf_ref, yf_ref,
                  r0=j * CHUNK, backward=False)
        _ssd_role(xb_ref, dtb_ref, arow, triup_ref, rb_ref, dskip_ref, sb_ref, yb_ref,
                  r0=(n_sub - 1 - j) * CHUNK, backward=True)


def _ssd(xbc, dt, batch, seq_len, p):
    t = xbc.shape[0]
    rows = SSD_CHUNKS_PER_STEP * CHUNK
    nc = seq_len // rows
    fwd = lambda b, c: (b * nc + c, 0)
    bwd = lambda b, c: (b * nc + nc - 1 - c, 0)
    return pl.pallas_call(
        _ssd_kernel,
        out_shape=(jax.ShapeDtypeStruct((t, SSD_WIDTH), BF16), jax.ShapeDtypeStruct((t, SSD_WIDTH), BF16)),
        grid=(batch, nc),
        in_specs=[
            pl.BlockSpec((rows, CONV_CH), fwd),
            pl.BlockSpec((rows, LANES), fwd),
            pl.BlockSpec((rows, CONV_CH), bwd),
            pl.BlockSpec((rows, LANES), bwd),
            _const_spec((1, LANES)),
            _const_spec((CHUNK, CHUNK)),
            _const_spec((CHUNK, CHUNK)),
            _const_spec((2 * LANES, SSD_WIDTH)),
            _const_spec((2 * LANES, SSD_WIDTH)),
            _const_spec((1, SSD_WIDTH)),
        ],
        out_specs=(pl.BlockSpec((rows, SSD_WIDTH), fwd), pl.BlockSpec((rows, SSD_WIDTH), bwd)),
        scratch_shapes=[pltpu.VMEM((STATE, SSD_WIDTH), F32), pltpu.VMEM((STATE, SSD_WIDTH), F32)],
        compiler_params=pltpu.CompilerParams(
            dimension_semantics=("arbitrary", "arbitrary"), vmem_limit_bytes=VMEM_LIMIT),
        name="ssd",
    )(xbc, dt, xbc, dt, p["a_log"], p["tri_lo"], p["tri_up"], p["rexp_f"], p["rexp_b"], p["d_skip"])


def _outproj_kernel(yf_ref, yb_ref, sz_ref, u_ref, v_ref, x_ref, gs_ref, wsp_ref, bsp_ref, wo_ref,
                    g2_ref, wr_ref, rb_ref, tri_ref, trie_ref,
                    x1_ref, h2_ref, mt_ref, tm_ref,
                    gm_buf, carry):
    i = pl.program_id(0)
    tm = x_ref.shape[0]

    @pl.when(i == 0)
    def _():
        carry[...] = jnp.zeros_like(carry)

    gated = (yf_ref[...].astype(F32) + yb_ref[...].astype(F32)) * sz_ref[...].astype(F32)
    ms = jnp.mean(gated * gated, axis=-1, keepdims=True)
    ssd = (gated * lax.rsqrt(ms + RMS_EPS) * gs_ref[...]).astype(BF16)

    for c in range(tm // CHUNK):
        r0 = c * CHUNK
        for hh in range(GMLP_HEADS):
            c0 = hh * GMLP_HEAD_DIM
            sv = _dot(wsp_ref[hh], v_ref[r0:r0 + CHUNK, c0:c0 + GMLP_HEAD_DIM]) + bsp_ref[hh]
            gm_buf[r0:r0 + CHUNK, c0:c0 + GMLP_HEAD_DIM] = (
                u_ref[r0:r0 + CHUNK, c0:c0 + GMLP_HEAD_DIM].astype(F32) * sv).astype(BF16)

    x1 = x_ref[...] + _dot(ssd, wo_ref[0:SSD_WIDTH, :]) + _dot(gm_buf[...], wo_ref[SSD_WIDTH:, :])
    x1_ref[...] = x1

    ms2 = jnp.mean(x1 * x1, axis=-1, keepdims=True)
    h2 = (x1 * lax.rsqrt(ms2 + RMS_EPS) * g2_ref[...]).astype(BF16)
    h2_ref[...] = h2

    logits = _dot(h2, wr_ref[...]) + rb_ref[...]
    work = logits.T[0:N_EXPERTS, :]
    erow = lax.broadcasted_iota(I32, (N_EXPERTS, tm), 0).astype(F32)
    sel_all = jnp.zeros((N_EXPERTS, tm), F32)
    sels, vals = [], []
    for _ in range(TOP_K):
        m = jnp.max(work, axis=0, keepdims=True)
        idx = jnp.min(jnp.where(work == m, erow, float(N_EXPERTS)), axis=0, keepdims=True)
        sel = erow == idx
        sel_all = jnp.where(sel, 1.0, sel_all)
        work = jnp.where(sel, -jnp.inf, work)
        sels.append(sel)
        vals.append(m)
    exps = [jnp.exp(v - vals[0]) for v in vals]
    denom = exps[0] + exps[1] + exps[2] + exps[3]

    lrank = _dot(sel_all.astype(BF16), tri_ref[...])
    n_e = jnp.sum(sel_all, axis=1, keepdims=True)
    n_al = jnp.broadcast_to(jnp.floor((n_e + (SEG - 1)) * (1.0 / SEG)) * SEG, (N_EXPERTS, LANES))
    lstart = _dot(trie_ref[...], n_al.astype(BF16))
    old_cnt = carry[...]
    new_cnt = old_cnt + n_al
    carry[...] = new_cnt
    tile_row = jnp.concatenate([n_al, old_cnt, lstart, new_cnt], axis=0).T[0:1, :]
    tm_ref[0] = tile_row.astype(I32)

    local_row = lstart[:, 0:1] + lrank
    out_row = lax.broadcasted_iota(I32, (2 * TOP_K, tm), 0)
    meta = jnp.zeros((2 * TOP_K, tm), F32)
    for k in range(TOP_K):
        dk = jnp.sum(jnp.where(sels[k], local_row, 0.0), axis=0, keepdims=True)
        meta = jnp.where(out_row == k, dk, meta)
        meta = jnp.where(out_row == TOP_K + k, exps[k] / denom, meta)
    mt_ref[...] = meta


def _outproj(yf, yb, sz, u, v, x2d, p):
    t = x2d.shape[0]
    tm = TM_OUT
    row = lambda i: (i, 0)
    nt = t // tm
    out_shape = (
        jax.ShapeDtypeStruct((t, D_MODEL), F32),
        jax.ShapeDtypeStruct((t, D_MODEL), BF16),
        jax.ShapeDtypeStruct((nt * 2 * TOP_K, tm), F32),
        jax.ShapeDtypeStruct((nt, 1, LANES), I32),
    )
    bf_row = pl.BlockSpec((tm, D_MODEL), row)
    return pl.pallas_call(
        _outproj_kernel,
        out_shape=out_shape,
        grid=(t // tm,),
        in_specs=[
            bf_row, bf_row, bf_row, bf_row, bf_row, bf_row,
            _const_spec((1, SSD_WIDTH)),
            _const_spec((GMLP_HEADS, CHUNK, CHUNK)),
            _const_spec((GMLP_HEADS, CHUNK, GMLP_HEAD_DIM)),
            _const_spec((SSD_WIDTH + GMLP_WIDTH, D_MODEL)),
            _const_spec((1, D_MODEL)),
            _const_spec((D_MODEL, LANES)),
            _const_spec((1, LANES)),
            _const_spec((tm, tm)),
            _const_spec((N_EXPERTS, N_EXPERTS)),
        ],
        out_specs=(
            pl.BlockSpec((tm, D_MODEL), row),
            pl.BlockSpec((tm, D_MODEL), row),
            pl.BlockSpec((2 * TOP_K, tm), row),
            pl.BlockSpec((1, 1, LANES), lambda i: (i, 0, 0)),
        ),
        scratch_shapes=[pltpu.VMEM((tm, GMLP_WIDTH), BF16), pltpu.VMEM((N_EXPERTS, LANES), F32)],
        compiler_params=pltpu.CompilerParams(
            dimension_semantics=("arbitrary",), vmem_limit_bytes=VMEM_LIMIT),
        name="outproj",
    )(yf, yb, sz, u, v, x2d, p["gs"], p["w_sp"], p["b_sp"], p["w_out"], p["g2"], p["w_r"], p["b_r"],
      p["tok_before"], p["expert_before"])


def _copy_segment(make_copy, n, src_row, dst_row):
    big, mid, small = COPY_ROWS

    def body(c, carry_):
        make_copy(pl.multiple_of(src_row + c * big, SEG), pl.multiple_of(dst_row + c * big, SEG), big).start()
        return carry_
    nbig = n // big
    lax.fori_loop(0, nbig, body, 0)
    done = nbig * big

    @pl.when((n & mid) != 0)
    def _():
        make_copy(pl.multiple_of(src_row + done, SEG), pl.multiple_of(dst_row + done, SEG), mid).start()

    done2 = done + (n & mid)

    @pl.when((n & small) != 0)
    def _():
        make_copy(pl.multiple_of(src_row + done2, SEG), pl.multiple_of(dst_row + done2, SEG), small).start()


def _wait_rows(make_copy, nrows, limit=R_LOCAL):
    w = 1 << (limit.bit_length() - 1)
    while w >= SEG:
        @pl.when((nrows & w) != 0)
        def _(w=w):
            make_copy(0, 0, w).wait()
        w //= 2


def _onehot_rows(mt_ref, blk, values):
    tm = mt_ref.shape[1]
    riota = lax.broadcasted_iota(I32, (SORT_BLK, tm), 0).astype(F32).astype(BF16)
    out = jnp.zeros((SORT_BLK, tm), BF16)
    for k in reversed(range(TOP_K)):
        off = (mt_ref[k:k + 1, :] - float(blk * SORT_BLK)).astype(BF16)
        out = jnp.where(riota == off, values[k], out)
    return out


def _dispatch_kernel(ps_ref, cn_ref, nu_ref, dst_ref, h2_ref, mt_ref, xs_ref, sbuf, zbuf, sem):
    i = pl.program_id(0)
    nt = pl.num_programs(0)
    tm = h2_ref.shape[0]
    groups_per_blk = SORT_BLK // SEG

    def wait_tile():
        pltpu.make_async_copy(sbuf, xs_ref.at[pl.ds(0, R_LOCAL)], sem).wait()

    h2 = h2_ref[...]
    ones = [jnp.ones((1, tm), BF16)] * TOP_K
    for blk in range(R_LOCAL // SORT_BLK):
        s = _dot(_onehot_rows(mt_ref, blk, ones), h2)
        lo_bits = lax.bitcast_convert_type(s[:, 0:HALF], U32)
        hi_bits = lax.bitcast_convert_type(s[:, HALF:], U32)
        packed = (lo_bits >> 16) | hi_bits
        if blk == 0:
            @pl.when(i > 0)
            def _():
                wait_tile()
        sbuf[blk * SORT_BLK:(blk + 1) * SORT_BLK, :] = packed
        for g in range(blk * groups_per_blk, (blk + 1) * groups_per_blk):
            pltpu.make_async_copy(sbuf.at[pl.ds(g * SEG, SEG)],
                                  xs_ref.at[pl.ds(pl.multiple_of(dst_ref[g], SEG), SEG)], sem).start()

    @pl.when(i == nt - 1)
    def _():
        wait_tile()
        zbuf[...] = jnp.zeros_like(zbuf)

        def zero_copy(src_row, dst_row, rows):
            return pltpu.make_async_copy(zbuf.at[pl.ds(src_row, rows)], xs_ref.at[pl.ds(dst_row, rows)], sem)

        def tail_rows(e):
            return (cn_ref[e] + (BM - 1)) // BM * BM - cn_ref[e]

        def start_tail(e, c):
            _copy_segment(zero_copy, tail_rows(e), 0, ps_ref[e] + cn_ref[e])
            return c
        lax.fori_loop(0, N_EXPERTS, start_tail, 0)

        def start_block(b, c):
            zero_copy(0, pl.multiple_of(b * BM, BM), BM).start()
            return c
        n_rows_total = xs_ref.shape[0]
        lax.fori_loop(nu_ref[0], n_rows_total // BM, start_block, 0)

        def wait_tail(e, c):
            _wait_rows(zero_copy, tail_rows(e), limit=BM - SEG)
            return c
        lax.fori_loop(0, N_EXPERTS, wait_tail, 0)

        def wait_block(b, c):
            zero_copy(0, 0, BM).wait()
            return c
        lax.fori_loop(nu_ref[0], n_rows_total // BM, wait_block, 0)


def _dispatch(pad_start, counts, n_used, dst_table, h2, mt, rows):
    t = h2.shape[0]
    tm = TM_OUT
    grid_spec = pltpu.PrefetchScalarGridSpec(
        num_scalar_prefetch=3,
        grid=(t // tm,),
        in_specs=[
            pl.BlockSpec((ROW_TABLE,), lambda i, *_: (i,), memory_space=pltpu.SMEM),
            pl.BlockSpec((tm, D_MODEL), lambda i, *_: (i, 0)),
            pl.BlockSpec((2 * TOP_K, tm), lambda i, *_: (i, 0)),
        ],
        out_specs=pl.BlockSpec(memory_space=pl.ANY),
        scratch_shapes=[pltpu.VMEM((R_LOCAL, HALF), U32), pltpu.VMEM((BM, HALF), U32),
                        pltpu.SemaphoreType.DMA(())],
    )
    return pl.pallas_call(
        _dispatch_kernel,
        out_shape=jax.ShapeDtypeStruct((rows, HALF), U32),
        grid_spec=grid_spec,
        compiler_params=pltpu.CompilerParams(
            dimension_semantics=("arbitrary",), vmem_limit_bytes=VMEM_LIMIT, has_side_effects=True),
        name="dispatch",
    )(pad_start, counts, n_used, dst_table, h2, mt)


def _experts_kernel(be_ref, nu_ref, xs_ref, wgu_ref, bgu_ref, wd_ref, bd_ref, ys_ref):
    i = pl.program_id(0)

    @pl.when(i < nu_ref[0])
    def _():
        lo, hi = _unpack_bf16_pair(xs_ref[...])
        xb = jnp.concatenate([lo.astype(BF16), hi.astype(BF16)], axis=1)
        gu = _dot(xb, wgu_ref[0]) + bgu_ref[0]
        gate = jnp.minimum(gu[:, 0:D_FF], SWIGLU_LIMIT)
        up = jnp.clip(gu[:, D_FF:], -SWIGLU_LIMIT, SWIGLU_LIMIT)
        act = (up + 1.0) * (gate * jax.nn.sigmoid(gate * SWIGLU_ALPHA))
        y = _dot(act.astype(BF16), wd_ref[0]) + bd_ref[0]
        ys_ref[...] = _pack_bf16_pair(y[:, 0:HALF], y[:, HALF:])

    @pl.when(i >= nu_ref[0])
    def _():
        ys_ref[...] = jnp.zeros_like(ys_ref)


def _experts(block_e, n_used, xs, p):
    rows = xs.shape[0]
    nb = rows // BM
    grid_spec = pltpu.PrefetchScalarGridSpec(
        num_scalar_prefetch=2,
        grid=(nb,),
        in_specs=[
            pl.BlockSpec((BM, HALF), lambda i, be, *_: (i, 0)),
            pl.BlockSpec((1, D_MODEL, 2 * D_FF), lambda i, be, *_: (be[i], 0, 0)),
            pl.BlockSpec((1, 1, 2 * D_FF), lambda i, be, *_: (be[i], 0, 0)),
            pl.BlockSpec((1, D_FF, D_MODEL), lambda i, be, *_: (be[i], 0, 0)),
            pl.BlockSpec((1, 1, D_MODEL), lambda i, be, *_: (be[i], 0, 0)),
        ],
        out_specs=pl.BlockSpec((BM, HALF), lambda i, be, *_: (i, 0)),
    )
    return pl.pallas_call(
        _experts_kernel,
        out_shape=jax.ShapeDtypeStruct((rows, HALF), U32),
        grid_spec=grid_spec,
        compiler_params=pltpu.CompilerParams(
            dimension_semantics=("arbitrary",), vmem_limit_bytes=VMEM_LIMIT),
        name="experts",
    )(block_e, n_used, xs, p["w_gu"], p["b_gu"], p["w_d"], p["b_d"])


def _combine_kernel(cur_ref, nxt_ref, mt_ref, x1_ref, gf_ref, ys_ref, o_ref,
                    ybuf, gt_buf, ylo_buf, yhi_buf, sems):
    i = pl.program_id(0)
    nt = pl.num_programs(0)

    def issue(table_ref, slot):
        for g in range(R_LOCAL // SEG):
            pltpu.make_async_copy(ys_ref.at[pl.ds(pl.multiple_of(table_ref[g], SEG), SEG)],
                                  ybuf.at[slot, pl.ds(g * SEG, SEG)], sems.at[slot]).start()

    def wait_slot(slot):
        pltpu.make_async_copy(ys_ref.at[pl.ds(0, R_LOCAL)], ybuf.at[slot], sems.at[slot]).wait()

    @pl.when(i == 0)
    def _():
        issue(cur_ref, 0)

    slot = i % 2
    issue(nxt_ref, 1 - slot)

    gates = [mt_ref[TOP_K + k:TOP_K + k + 1, :].astype(BF16) for k in range(TOP_K)]
    for blk in range(R_LOCAL // SORT_BLK):
        rows = slice(blk * SORT_BLK, (blk + 1) * SORT_BLK)
        gt_buf[rows, :] = _onehot_rows(mt_ref, blk, gates)
    wait_slot(slot)
    for blk in range(R_LOCAL // SORT_BLK):
        rows = slice(blk * SORT_BLK, (blk + 1) * SORT_BLK)
        lo, hi = _unpack_bf16_pair(ybuf[slot, rows, :])
        ylo_buf[rows, :] = lo.astype(BF16)
        yhi_buf[rows, :] = hi.astype(BF16)
    contract_rows = (((0,), (0,)), ((), ()))
    g_t = gt_buf[...]
    x1 = x1_ref[...]
    acc = jnp.concatenate(
        [x1[:, 0:HALF] + lax.dot_general(g_t, ylo_buf[...], contract_rows, preferred_element_type=F32),
         x1[:, HALF:] + lax.dot_general(g_t, yhi_buf[...], contract_rows, preferred_element_type=F32)], axis=1)
    ms = jnp.mean(acc * acc, axis=-1, keepdims=True)
    o_ref[...] = acc * lax.rsqrt(ms + RMS_EPS) * gf_ref[...]

    @pl.when(i == nt - 1)
    def _():
        wait_slot(1 - slot)


def _combine(src_table, mt, x1, ys, p):
    t = x1.shape[0]
    tm = TM_OUT
    nt = t // tm
    grid_spec = pltpu.PrefetchScalarGridSpec(
        num_scalar_prefetch=0,
        grid=(nt,),
        in_specs=[
            pl.BlockSpec((ROW_TABLE,), lambda i: (i,), memory_space=pltpu.SMEM),
            pl.BlockSpec((ROW_TABLE,), lambda i: (jnp.minimum(i + 1, nt - 1),), memory_space=pltpu.SMEM),
            pl.BlockSpec((2 * TOP_K, tm), lambda i: (i, 0)),
            pl.BlockSpec((tm, D_MODEL), lambda i: (i, 0)),
            pl.BlockSpec((1, D_MODEL), lambda i: (0, 0)),
            pl.BlockSpec(memory_space=pl.ANY),
        ],
        out_specs=pl.BlockSpec((tm, D_MODEL), lambda i: (i, 0)),
        scratch_shapes=[pltpu.VMEM((2, R_LOCAL, HALF), U32), pltpu.VMEM((R_LOCAL, tm), BF16),
                        pltpu.VMEM((R_LOCAL, HALF), BF16), pltpu.VMEM((R_LOCAL, HALF), BF16),
                        pltpu.SemaphoreType.DMA((2,))],
    )
    return pl.pallas_call(
        _combine_kernel,
        out_shape=jax.ShapeDtypeStruct((t, D_MODEL), F32),
        grid_spec=grid_spec,
        compiler_params=pltpu.CompilerParams(
            dimension_semantics=("arbitrary",), vmem_limit_bytes=VMEM_LIMIT),
        name="combine",
    )(src_table, src_table, mt, x1, p["gf"], ys)


def _prepare_params(norm1_g, w_in, conv_w, conv_b, dt_bias_f, dt_bias_b, a_log_f, a_log_b, d_skip,
                    ssd_norm_g, gmlp_ln_g, gmlp_ln_b, w_spatial, b_spatial, w_out, norm2_g,
                    router_w, router_b, w_gate_up, b_gate_up, w_down, b_down, final_g):
    z_end = SSD_WIDTH
    xbc_end = z_end + CONV_CH
    dt_end = xbc_end + 2 * HEADS
    u_end = dt_end + GMLP_WIDTH
    w = w_in[0]
    pad_lanes = lambda a, fill=0.0: jnp.pad(a, ((0, 0), (0, LANES - a.shape[1])), constant_values=fill)
    head_lane = jnp.arange(LANES)[:, None]
    chan_head = (jnp.arange(SSD_WIDTH) // HEAD_DIM)[None, :]
    ar = jnp.arange(CHUNK)
    art = jnp.arange(TM_OUT)
    are = jnp.arange(N_EXPERTS)
    return {
        "g1": norm1_g[0][None, :],
        "w_z": w[:, :z_end].astype(BF16),
        "w_xbc": w[:, z_end:xbc_end].astype(BF16),
        "w_dt": pad_lanes(w[:, xbc_end:dt_end]).astype(BF16),
        "w_u": w[:, dt_end:u_end].astype(BF16),
        "w_v": w[:, u_end:].astype(BF16),
        "conv_w": jnp.pad(conv_w[0], ((0, 8 - CONV_WIDTH), (0, 0))),
        "conv_b": conv_b[0][None, :],
        "dt_bias": pad_lanes(jnp.concatenate([dt_bias_f[0], dt_bias_b[0]])[None, :]),
        "ln_g": gmlp_ln_g[0][None, :],
        "ln_b": gmlp_ln_b[0][None, :],
        "a_log": pad_lanes(jnp.concatenate([a_log_f[0], a_log_b[0]])[None, :]),
        "tri_lo": (ar[:, None] >= ar[None, :]).astype(BF16),
        "tri_up": (ar[:, None] <= ar[None, :]).astype(BF16),
        "rexp_f": jnp.tile((head_lane == chan_head).astype(BF16), (2, 1)),
        "rexp_b": jnp.tile((head_lane == chan_head + HEADS).astype(BF16), (2, 1)),
        "d_skip": jnp.repeat(d_skip[0], HEAD_DIM)[None, :],
        "gs": ssd_norm_g[0][None, :],
        "w_sp": w_spatial[0].astype(BF16),
        "b_sp": jnp.broadcast_to(b_spatial[0][:, :, None], (GMLP_HEADS, CHUNK, GMLP_HEAD_DIM)),
        "w_out": w_out[0].astype(BF16),
        "g2": norm2_g[0][None, :],
        "w_r": pad_lanes(router_w[0]).astype(BF16),
        "b_r": pad_lanes(router_b[0][None, :], NEG_BIG),
        "tok_before": (art[:, None] < art[None, :]).astype(BF16),
        "expert_before": (are[None, :] < are[:, None]).astype(BF16),
        "w_gu": w_gate_up[0].astype(BF16),
        "b_gu": b_gate_up[0][:, None, :],
        "w_d": w_down[0].astype(BF16),
        "b_d": b_down[0][:, None, :],
        "gf": final_g[None, :],
    }


def _encoder(x, p):
    batch, seq_len, _ = x.shape
    t = batch * seq_len
    x2d = x.reshape(t, D_MODEL)
    sz, xbc, dt, u, v = _inproj(x2d, seq_len, p)
    yf, yb = _ssd(xbc, dt, batch, seq_len, p)
    x1, h2, mt, tile_rows = _outproj(yf, yb, sz, u, v, x2d, p)
    tile_rows = tile_rows[:, 0, :]
    tn, tc, tl = (tile_rows[:, k * N_EXPERTS:(k + 1) * N_EXPERTS] for k in range(3))

    counts = tile_rows[-1, 3 * N_EXPERTS:]
    padded = (counts + BM - 1) // BM * BM
    pad_end = jnp.cumsum(padded)
    pad_start = (pad_end - padded).astype(I32)
    n_tiles = t // TM_OUT
    n_blocks = -(-(t * TOP_K + n_tiles * N_EXPERTS * (SEG - 1) + N_EXPERTS * (BM - 1)) // BM)
    n_used = (pad_end[-1] // BM).astype(I32)
    blk_row = jnp.minimum(jnp.arange(n_blocks + SPARE_BLOCKS, dtype=I32), n_used - 1) * BM
    onehot_e = (pad_end[None, :] <= blk_row[:, None]).astype(I32)
    block_e = jnp.minimum(jnp.sum(onehot_e, axis=1), N_EXPERTS - 1).astype(I32)
    n_used = n_used.reshape(1)

    group_row = jnp.arange(R_LOCAL // SEG, dtype=I32) * SEG
    owner = jnp.sum(((tl + tn)[:, None, :] <= group_row[None, :, None]).astype(I32), axis=-1)
    is_owner = (owner[:, :, None] == jnp.arange(N_EXPERTS, dtype=I32)[None, None, :]).astype(I32)
    base = jnp.sum(is_owner * (pad_start[None, :] + tc - tl)[:, None, :], axis=-1)
    used = owner < N_EXPERTS
    pad_table = lambda a: jnp.pad(a, ((0, 0), (0, ROW_TABLE - a.shape[1]))).reshape(-1)
    dst_table = pad_table(jnp.where(used, base + group_row[None, :], n_blocks * BM + group_row[None, :]))
    src_table = pad_table(jnp.where(used, base + group_row[None, :], 0))

    xs = _dispatch(pad_start, counts, n_used, dst_table, h2, mt, (n_blocks + SPARE_BLOCKS) * BM)
    ys = _experts(block_e, n_used, xs, p)
    y = _combine(src_table, mt, x1, ys, p)
    return y.reshape(batch, seq_len, D_MODEL)


def kernel(x_prompt, x_sample, norm1_g, w_in, conv_w, conv_b, dt_bias_f, dt_bias_b, a_log_f, a_log_b,
           d_skip, ssd_norm_g, gmlp_ln_g, gmlp_ln_b, w_spatial, b_spatial, w_out, norm2_g, router_w,
           router_b, w_gate_up, b_gate_up, w_down, b_down, final_g):
    p = _prepare_params(norm1_g, w_in, conv_w, conv_b, dt_bias_f, dt_bias_b, a_log_f, a_log_b, d_skip,
                        ssd_norm_g, gmlp_ln_g, gmlp_ln_b, w_spatial, b_spatial, w_out, norm2_g,
                        router_w, router_b, w_gate_up, b_gate_up, w_down, b_down, final_g)
    return (_encoder(x_prompt, p), _encoder(x_sample, p))
```

```python
import functools

import jax
import jax.numpy as jnp
from jax import lax
from jax.experimental import pallas as pl
from jax.experimental.pallas import tpu as pltpu

F32 = jnp.float32
BF16 = jnp.bfloat16
U32 = jnp.uint32
I32 = jnp.int32

D_MODEL = 1024
SSD_WIDTH = 1024
HEAD_DIM = 64
HEADS = 16
GROUPS = 2
HEADS_PER_GROUP = 8
STATE = 128
CHUNK = 128
CONV_WIDTH = 5
CONV_HALF = 2
CONV_CH = SSD_WIDTH + 2 * GROUPS * STATE
GMLP_WIDTH = 1024
GMLP_HEADS = 8
GMLP_HEAD_DIM = 128
N_EXPERTS = 32
TOP_K = 4
D_FF = 1024
SWIGLU_LIMIT = 7.0
SWIGLU_ALPHA = 1.702
RMS_EPS = 1e-5
LN_EPS = 1e-5

LANES = 128
HALO = 16
SSD_CHUNKS_PER_STEP = 4
TM_IN = 512
IN_COLS = 256
IN_ROWS = 128
TM_OUT = 512
BM_CHOICES = (1024, 512)
BM_MIN_BLOCKS = 6
SEG = 8
COPY_ROWS = (32, 16, 8)
SORT_BLK = 256
R_LOCAL = -(-(TM_OUT * TOP_K + N_EXPERTS * (SEG - 1)) // SORT_BLK) * SORT_BLK
ROW_TABLE = 1024
HALF = D_MODEL // 2
VMEM_LIMIT = 56 * 1024 * 1024
NEG_BIG = -1e30
LOG2_E = 1.4426950408889634


def _dot(a, b):
    return jnp.dot(a, b, preferred_element_type=F32)


def _gelu(x):
    return 0.5 * x * (1.0 + lax.erf(x * (2.0 ** -0.5)))


def _pack_bf16_pair(lo, hi):
    lo_bits = lax.bitcast_convert_type(lo.astype(BF16).astype(F32), U32)
    hi_bits = lax.bitcast_convert_type(hi.astype(BF16).astype(F32), U32)
    return (lo_bits >> 16) | hi_bits


def _unpack_bf16_pair(w):
    lo = lax.bitcast_convert_type(w << 16, F32)
    hi = lax.bitcast_convert_type(w & jnp.uint32(0xFFFF0000), F32)
    return lo, hi


def _const_spec(shape):
    nd = len(shape)
    return pl.BlockSpec(shape, lambda *_: (0,) * nd)


def _inproj_kernel(x_ref, xp_ref, xn_ref, g1_ref, wz_ref, wxbc_ref, wdt_ref, wu_ref, wv_ref,
                   cw_ref, cb_ref, dtb_ref, lng_ref, lnb_ref,
                   sz_ref, xbc_ref, dt_ref, u_ref, v_ref, buf0, buf1, buf2, gbuf, *, tiles_per_seq):
    i = pl.program_id(0)
    tm = x_ref.shape[0]
    g1 = g1_ref[...]
    row_blocks = [(r0, IN_ROWS) for r0 in range(0, tm, IN_ROWS)]

    def norm(xv):
        ms = jnp.mean(xv * xv, axis=-1, keepdims=True)
        return xv * lax.rsqrt(ms + RMS_EPS) * g1

    keep_prev = jnp.where(i % tiles_per_seq == 0, 0.0, 1.0)
    keep_next = jnp.where(i % tiles_per_seq == tiles_per_seq - 1, 0.0, 1.0)
    h = norm(x_ref[...]).astype(BF16)
    h_all = jnp.concatenate([(norm(xp_ref[...]) * keep_prev).astype(BF16), h,
                             (norm(xn_ref[...]) * keep_next).astype(BF16)], axis=0)

    def silu_to(out_ref):
        def epilogue(buf, c0, wd):
            for r0, nr in row_blocks:
                r = buf[r0:r0 + nr, 0:wd]
                out_ref[r0:r0 + nr, c0:c0 + wd] = (r * jax.nn.sigmoid(r)).astype(BF16)
        return epilogue

    def conv_silu(buf, c0, wd):
        for r0, nr in row_blocks:
            acc = jnp.broadcast_to(cb_ref[:, c0:c0 + wd], (nr, wd))
            for k in range(CONV_WIDTH):
                acc = acc + cw_ref[k:k + 1, c0:c0 + wd] * buf[pl.ds(r0 + HALO - CONV_HALF + k, nr), 0:wd]
            xbc_ref[r0:r0 + nr, c0:c0 + wd] = (acc * jax.nn.sigmoid(acc)).astype(BF16)

    def softplus_dt(buf, c0, wd):
        dt_ref[...] = jax.nn.softplus(buf[0:tm, 0:wd] + dtb_ref[...])

    def gelu_u(buf, c0, wd):
        for r0, nr in row_blocks:
            u_ref[r0:r0 + nr, c0:c0 + wd] = _gelu(buf[r0:r0 + nr, 0:wd]).astype(BF16)

    sums = [jnp.zeros((nr, LANES), F32) for _, nr in row_blocks]

    def gelu_v(buf, c0, wd):
        for b, (r0, nr) in enumerate(row_blocks):
            g = _gelu(buf[r0:r0 + nr, 0:wd])
            gbuf[r0:r0 + nr, c0:c0 + wd] = g
            for l0 in range(0, wd, LANES):
                sums[b] = sums[b] + g[:, l0:l0 + LANES]

    heavy = [(h_all, wxbc_ref, c0, IN_COLS, conv_silu) for c0 in range(0, CONV_CH, IN_COLS)]
    light = [(h, wz_ref, c0, IN_COLS, silu_to(sz_ref)) for c0 in range(0, SSD_WIDTH, IN_COLS)]
    light += [(h, wu_ref, c0, IN_COLS, gelu_u) for c0 in range(0, GMLP_WIDTH, IN_COLS)]
    light += [(h, wv_ref, c0, IN_COLS, gelu_v) for c0 in range(0, GMLP_WIDTH, IN_COLS)]
    light.append((h, wdt_ref, 0, LANES, softplus_dt))
    stages = []
    while heavy or light:
        if heavy:
            stages.append(heavy.pop(0))
        if light:
            stages.append(light.pop(0))
    bufs = (buf0, buf1, buf2)
    ahead = len(bufs) - 1

    def matmul(j):
        lhs, w_ref, c0, wd, _ = stages[j]
        bufs[j % len(bufs)][0:lhs.shape[0], 0:wd] = _dot(lhs, w_ref[:, c0:c0 + wd])

    for j in range(ahead):
        matmul(j)
    for j, (_, _, c0, wd, epilogue) in enumerate(stages):
        if j + ahead < len(stages):
            matmul(j + ahead)
        epilogue(bufs[j % len(bufs)], c0, wd)

    inv_n = 1.0 / GMLP_WIDTH
    for b, (r0, nr) in enumerate(row_blocks):
        mu = jnp.sum(sums[b], axis=-1, keepdims=True) * inv_n
        sq = jnp.zeros((nr, LANES), F32)
        for l0 in range(0, GMLP_WIDTH, LANES):
            d = gbuf[r0:r0 + nr, l0:l0 + LANES] - mu
            sq = sq + d * d
        scale = lax.rsqrt(jnp.sum(sq, axis=-1, keepdims=True) * inv_n + LN_EPS)
        for c0 in range(0, GMLP_WIDTH, IN_COLS):
            g = gbuf[r0:r0 + nr, c0:c0 + IN_COLS]
            v_ref[r0:r0 + nr, c0:c0 + IN_COLS] = (
                (g - mu) * scale * lng_ref[:, c0:c0 + IN_COLS] + lnb_ref[:, c0:c0 + IN_COLS]).astype(BF16)


def _inproj(x2d, seq_len, p):
    t = x2d.shape[0]
    tm = TM_IN
    tiles_per_seq = seq_len // tm
    nh = tm // HALO
    last_halo = t // HALO - 1
    kern = functools.partial(_inproj_kernel, tiles_per_seq=tiles_per_seq)
    row = lambda i: (i, 0)
    out_shape = (
        jax.ShapeDtypeStruct((t, SSD_WIDTH), BF16),
        jax.ShapeDtypeStruct((t, CONV_CH), BF16),
        jax.ShapeDtypeStruct((t, LANES), F32),
        jax.ShapeDtypeStruct((t, GMLP_WIDTH), BF16),
        jax.ShapeDtypeStruct((t, GMLP_WIDTH), BF16),
    )
    return pl.pallas_call(
        kern,
        out_shape=out_shape,
        grid=(t // tm,),
        in_specs=[
            pl.BlockSpec((tm, D_MODEL), row),
            pl.BlockSpec((HALO, D_MODEL), lambda i: (jnp.maximum(i * nh - 1, 0), 0)),
            pl.BlockSpec((HALO, D_MODEL), lambda i: (jnp.minimum((i + 1) * nh, last_halo), 0)),
            _const_spec((1, D_MODEL)),
            _const_spec((D_MODEL, SSD_WIDTH)),
            _const_spec((D_MODEL, CONV_CH)),
            _const_spec((D_MODEL, LANES)),
            _const_spec((D_MODEL, GMLP_WIDTH)),
            _const_spec((D_MODEL, GMLP_WIDTH)),
            _const_spec((8, CONV_CH)),
            _const_spec((1, CONV_CH)),
            _const_spec((1, LANES)),
            _const_spec((1, GMLP_WIDTH)),
            _const_spec((1, GMLP_WIDTH)),
        ],
        out_specs=(
            pl.BlockSpec((tm, SSD_WIDTH), row),
            pl.BlockSpec((tm, CONV_CH), row),
            pl.BlockSpec((tm, LANES), row),
            pl.BlockSpec((tm, GMLP_WIDTH), row),
            pl.BlockSpec((tm, GMLP_WIDTH), row),
        ),
        scratch_shapes=[pltpu.VMEM((tm + 2 * HALO, IN_COLS), F32)] * 3 + [pltpu.VMEM((tm, GMLP_WIDTH), F32)],
        compiler_params=pltpu.CompilerParams(
            dimension_semantics=("arbitrary",), vmem_limit_bytes=VMEM_LIMIT),
        name="inproj",
    )(x2d, x2d, x2d, p["g1"], p["w_z"], p["w_xbc"], p["w_dt"], p["w_u"], p["w_v"],
      p["conv_w"], p["conv_b"], p["dt_bias"], p["ln_g"], p["ln_b"])


def _split2(q):
    hi = q.astype(BF16)
    lo = (q - hi.astype(F32)).astype(BF16)
    return hi, lo


def _ssd_role(xbc_ref, dt_ref, arow, tri_ref, rexp_ref, dskip_ref, s_ref, y_ref, *, r0, backward):
    lane0 = HEADS if backward else 0
    rows = slice(r0, r0 + CHUNK)
    dt_all = dt_ref[rows, :]
    adt = dt_all * arow
    a1 = adt.astype(BF16)
    r1 = adt - a1.astype(F32)
    a2 = r1.astype(BF16)
    a3 = (r1 - a2.astype(F32)).astype(BF16)
    acs3 = _dot(tri_ref[...], jnp.concatenate([a1, a2, a3], axis=1))
    acs = acs3[:, 0:LANES] + acs3[:, LANES:2 * LANES] + acs3[:, 2 * LANES:]
    edge = 0 if backward else CHUNK - 1
    tot = acs[edge:edge + 1, :]
    decay = jnp.exp(tot - acs)
    eacs = jnp.exp(acs)
    acs2 = acs * LOG2_E
    acs2_t = acs2.T

    hi, lo = _split2(jnp.concatenate([dt_all, decay, eacs], axis=0))
    expanded = _dot(jnp.concatenate([hi, lo], axis=1), rexp_ref[...])
    dt_e = expanded[0:CHUNK]
    decay_e = expanded[CHUNK:2 * CHUNK]
    eacs_e = expanded[2 * CHUNK:]
    etot_e = eacs_e[edge:edge + 1, :]

    xf = xbc_ref[rows, 0:SSD_WIDTH].astype(F32)
    xdt = xf * dt_e
    xdd = (xdt * decay_e).astype(BF16)
    xdt_b = xdt.astype(BF16)

    li = lax.broadcasted_iota(I32, (CHUNK, CHUNK), 0)
    si = lax.broadcasted_iota(I32, (CHUNK, CHUNK), 1)
    keep = (si >= li) if backward else (si <= li)
    left = si < HEAD_DIM
    zero_b = jnp.zeros((CHUNK, LANES), BF16)

    for g in range(GROUPS):
        b_g = xbc_ref[rows, SSD_WIDTH + g * STATE:SSD_WIDTH + (g + 1) * STATE]
        c_g = xbc_ref[rows, SSD_WIDTH + (GROUPS + g) * STATE:SSD_WIDTH + (GROUPS + g + 1) * STATE]
        b_t = b_g.astype(F32).T.astype(BF16)
        cb = _dot(c_g, b_t)
        gc0 = g * HEADS_PER_GROUP * HEAD_DIM
        gw = HEADS_PER_GROUP * HEAD_DIM
        s_old = s_ref[:, gc0:gc0 + gw]
        y_off = _dot(c_g, s_old.astype(BF16))
        s_new = _dot(b_t, xdd[:, gc0:gc0 + gw])
        s_ref[:, gc0:gc0 + gw] = s_old * etot_e[:, gc0:gc0 + gw] + s_new
        for pp in range(HEADS_PER_GROUP // 2):
            c0 = gc0 + pp * LANES
            ms = []
            for q in range(2):
                hl = lane0 + g * HEADS_PER_GROUP + 2 * pp + q
                seg = acs2[:, hl:hl + 1] - acs2_t[hl:hl + 1, :]
                lmat = jnp.exp2(jnp.where(keep, seg, -jnp.inf))
                ms.append((cb * lmat).astype(BF16))
            lhs = jnp.concatenate(ms, axis=1)
            xp = xdt_b[:, c0:c0 + LANES]
            rhs = jnp.concatenate([jnp.where(left, xp, zero_b), jnp.where(left, zero_b, xp)], axis=0)
            y = _dot(lhs, rhs) + y_off[:, pp * LANES:(pp + 1) * LANES] * eacs_e[:, c0:c0 + LANES]
            if not backward:
                y = y + xf[:, c0:c0 + LANES] * dskip_ref[:, c0:c0 + LANES]
            y_ref[rows, c0:c0 + LANES] = y.astype(BF16)


def _ssd_kernel(xf_ref, dtf_ref, xb_ref, dtb_ref, alog_ref, trilo_ref, triup_ref, rf_ref, rb_ref,
                dskip_ref, yf_ref, yb_ref, sf_ref, sb_ref):
    @pl.when(pl.program_id(1) == 0)
    def _():
        sf_ref[...] = jnp.zeros_like(sf_ref)
        sb_ref[...] = jnp.zeros_like(sb_ref)

    arow = -jnp.exp(alog_ref[...])
    n_sub = xf_ref.shape[0] // CHUNK
    for j in range(n_sub):
        _ssd_role(xf_ref, dtf_ref, arow, trilo_ref, rf_ref, dskip_ref, sf_ref, yf_ref,
                  r0=j * CHUNK, backward=False)
        _ssd_role(xb_ref, dtb_ref, arow, triup_ref, rb_ref, dskip_ref, sb_ref, yb_ref,
                  r0=(n_sub - 1 - j) * CHUNK, backward=True)


def _ssd(xbc, dt, batch, seq_len, p):
    t = xbc.shape[0]
    rows = SSD_CHUNKS_PER_STEP * CHUNK
    nc = seq_len // rows
    fwd = lambda b, c: (b * nc + c, 0)
    bwd = lambda b, c: (b * nc + nc - 1 - c, 0)
    return pl.pallas_call(
        _ssd_kernel,
        out_shape=(jax.ShapeDtypeStruct((t, SSD_WIDTH), BF16), jax.ShapeDtypeStruct((t, SSD_WIDTH), BF16)),
        grid=(batch, nc),
        in_specs=[
            pl.BlockSpec((rows, CONV_CH), fwd),
            pl.BlockSpec((rows, LANES), fwd),
            pl.BlockSpec((rows, CONV_CH), bwd),
            pl.BlockSpec((rows, LANES), bwd),
            _const_spec((1, LANES)),
            _const_spec((CHUNK, CHUNK)),
            _const_spec((CHUNK, CHUNK)),
            _const_spec((2 * LANES, SSD_WIDTH)),
            _const_spec((2 * LANES, SSD_WIDTH)),
            _const_spec((1, SSD_WIDTH)),
        ],
        out_specs=(pl.BlockSpec((rows, SSD_WIDTH), fwd), pl.BlockSpec((rows, SSD_WIDTH), bwd)),
        scratch_shapes=[pltpu.VMEM((STATE, SSD_WIDTH), F32), pltpu.VMEM((STATE, SSD_WIDTH), F32)],
        compiler_params=pltpu.CompilerParams(
            dimension_semantics=("arbitrary", "arbitrary"), vmem_limit_bytes=VMEM_LIMIT),
        name="ssd",
    )(xbc, dt, xbc, dt, p["a_log"], p["tri_lo"], p["tri_up"], p["rexp_f"], p["rexp_b"], p["d_skip"])


def _outproj_kernel(yf_ref, yb_ref, sz_ref, u_ref, v_ref, x_ref, gs_ref, wsp_ref, bsp_ref, wo_ref,
                    g2_ref, wr_ref, rb_ref, tri_ref, trie_ref,
                    x1_ref, h2_ref, mt_ref, tm_ref,
                    gm_buf, carry):
    i = pl.program_id(0)
    tm = x_ref.shape[0]

    @pl.when(i == 0)
    def _():
        carry[...] = jnp.zeros_like(carry)

    gated = (yf_ref[...].astype(F32) + yb_ref[...].astype(F32)) * sz_ref[...].astype(F32)
    ms = jnp.mean(gated * gated, axis=-1, keepdims=True)
    ssd = (gated * lax.rsqrt(ms + RMS_EPS) * gs_ref[...]).astype(BF16)

    for c in range(tm // CHUNK):
        r0 = c * CHUNK
        for hh in range(GMLP_HEADS):
            c0 = hh * GMLP_HEAD_DIM
            sv = _dot(wsp_ref[hh], v_ref[r0:r0 + CHUNK, c0:c0 + GMLP_HEAD_DIM]) + bsp_ref[hh]
            gm_buf[r0:r0 + CHUNK, c0:c0 + GMLP_HEAD_DIM] = (
                u_ref[r0:r0 + CHUNK, c0:c0 + GMLP_HEAD_DIM].astype(F32) * sv).astype(BF16)

    x1 = x_ref[...] + _dot(ssd, wo_ref[0:SSD_WIDTH, :]) + _dot(gm_buf[...], wo_ref[SSD_WIDTH:, :])
    x1_ref[...] = x1

    ms2 = jnp.mean(x1 * x1, axis=-1, keepdims=True)
    h2 = (x1 * lax.rsqrt(ms2 + RMS_EPS) * g2_ref[...]).astype(BF16)
    h2_ref[...] = h2

    logits = _dot(h2, wr_ref[...]) + rb_ref[...]
    work = logits.T[0:N_EXPERTS, :]
    erow = lax.broadcasted_iota(I32, (N_EXPERTS, tm), 0).astype(F32)
    sel_all = jnp.zeros((N_EXPERTS, tm), F32)
    sels, vals = [], []
    for _ in range(TOP_K):
        m = jnp.max(work, axis=0, keepdims=True)
        idx = jnp.min(jnp.where(work == m, erow, float(N_EXPERTS)), axis=0, keepdims=True)
        sel = erow == idx
        sel_all = jnp.where(sel, 1.0, sel_all)
        work = jnp.where(sel, -jnp.inf, work)
        sels.append(sel)
        vals.append(m)
    exps = [jnp.exp(v - vals[0]) for v in vals]
    denom = exps[0] + exps[1] + exps[2] + exps[3]

    lrank = _dot(sel_all.astype(BF16), tri_ref[...])
    n_e = jnp.sum(sel_all, axis=1, keepdims=True)
    n_al = jnp.broadcast_to(jnp.floor((n_e + (SEG - 1)) * (1.0 / SEG)) * SEG, (N_EXPERTS, LANES))
    lstart = _dot(trie_ref[...], n_al.astype(BF16))
    old_cnt = carry[...]
    new_cnt = old_cnt + n_al
    carry[...] = new_cnt
    tile_row = jnp.concatenate([n_al, old_cnt, lstart, new_cnt], axis=0).T[0:1, :]
    tm_ref[0] = tile_row.astype(I32)

    local_row = lstart[:, 0:1] + lrank
    out_row = lax.broadcasted_iota(I32, (2 * TOP_K, tm), 0)
    meta = jnp.zeros((2 * TOP_K, tm), F32)
    for k in range(TOP_K):
        dk = jnp.sum(jnp.where(sels[k], local_row, 0.0), axis=0, keepdims=True)
        meta = jnp.where(out_row == k, dk, meta)
        meta = jnp.where(out_row == TOP_K + k, exps[k] / denom, meta)
    mt_ref[...] = meta


def _outproj(yf, yb, sz, u, v, x2d, p):
    t = x2d.shape[0]
    tm = TM_OUT
    row = lambda i: (i, 0)
    nt = t // tm
    out_shape = (
        jax.ShapeDtypeStruct((t, D_MODEL), F32),
        jax.ShapeDtypeStruct((t, D_MODEL), BF16),
        jax.ShapeDtypeStruct((nt * 2 * TOP_K, tm), F32),
        jax.ShapeDtypeStruct((nt, 1, LANES), I32),
    )
    bf_row = pl.BlockSpec((tm, D_MODEL), row)
    return pl.pallas_call(
        _outproj_kernel,
        out_shape=out_shape,
        grid=(t // tm,),
        in_specs=[
            bf_row, bf_row, bf_row, bf_row, bf_row, bf_row,
            _const_spec((1, SSD_WIDTH)),
            _const_spec((GMLP_HEADS, CHUNK, CHUNK)),
            _const_spec((GMLP_HEADS, CHUNK, GMLP_HEAD_DIM)),
            _const_spec((SSD_WIDTH + GMLP_WIDTH, D_MODEL)),
            _const_spec((1, D_MODEL)),
            _const_spec((D_MODEL, LANES)),
            _const_spec((1, LANES)),
            _const_spec((tm, tm)),
            _const_spec((N_EXPERTS, N_EXPERTS)),
        ],
        out_specs=(
            pl.BlockSpec((tm, D_MODEL), row),
            pl.BlockSpec((tm, D_MODEL), row),
            pl.BlockSpec((2 * TOP_K, tm), row),
            pl.BlockSpec((1, 1, LANES), lambda i: (i, 0, 0)),
        ),
        scratch_shapes=[pltpu.VMEM((tm, GMLP_WIDTH), BF16), pltpu.VMEM((N_EXPERTS, LANES), F32)],
        compiler_params=pltpu.CompilerParams(
            dimension_semantics=("arbitrary",), vmem_limit_bytes=VMEM_LIMIT),
        name="outproj",
    )(yf, yb, sz, u, v, x2d, p["gs"], p["w_sp"], p["b_sp"], p["w_out"], p["g2"], p["w_r"], p["b_r"],
      p["tok_before"], p["expert_before"])


def _copy_segment(make_copy, n, src_row, dst_row):
    big, mid, small = COPY_ROWS

    def body(c, carry_):
        make_copy(pl.multiple_of(src_row + c * big, SEG), pl.multiple_of(dst_row + c * big, SEG), big).start()
        return carry_
    nbig = n // big
    lax.fori_loop(0, nbig, body, 0)
    done = nbig * big

    @pl.when((n & mid) != 0)
    def _():
        make_copy(pl.multiple_of(src_row + done, SEG), pl.multiple_of(dst_row + done, SEG), mid).start()

    done2 = done + (n & mid)

    @pl.when((n & small) != 0)
    def _():
        make_copy(pl.multiple_of(src_row + done2, SEG), pl.multiple_of(dst_row + done2, SEG), small).start()


def _wait_rows(make_copy, nrows, limit=R_LOCAL):
    w = 1 << (limit.bit_length() - 1)
    while w >= SEG:
        @pl.when((nrows & w) != 0)
        def _(w=w):
            make_copy(0, 0, w).wait()
        w //= 2


def _onehot_rows(mt_ref, blk, values):
    tm = mt_ref.shape[1]
    riota = lax.broadcasted_iota(I32, (SORT_BLK, tm), 0).astype(F32).astype(BF16)
    out = jnp.zeros((SORT_BLK, tm), BF16)
    for k in reversed(range(TOP_K)):
        off = (mt_ref[k:k + 1, :] - float(blk * SORT_BLK)).astype(BF16)
        out = jnp.where(riota == off, values[k], out)
    return out


def _dispatch_kernel(ps_ref, cn_ref, nu_ref, dst_ref, h2_ref, mt_ref, xs_ref, sbuf, zbuf, sem):
    i = pl.program_id(0)
    nt = pl.num_programs(0)
    tm = h2_ref.shape[0]
    groups_per_blk = SORT_BLK // SEG

    def wait_tile():
        pltpu.make_async_copy(sbuf, xs_ref.at[pl.ds(0, R_LOCAL)], sem).wait()

    h2 = h2_ref[...]
    ones = [jnp.ones((1, tm), BF16)] * TOP_K
    for blk in range(R_LOCAL // SORT_BLK):
        s = _dot(_onehot_rows(mt_ref, blk, ones), h2)
        lo_bits = lax.bitcast_convert_type(s[:, 0:HALF], U32)
        hi_bits = lax.bitcast_convert_type(s[:, HALF:], U32)
        packed = (lo_bits >> 16) | hi_bits
        if blk == 0:
            @pl.when(i > 0)
            def _():
                wait_tile()
        sbuf[blk * SORT_BLK:(blk + 1) * SORT_BLK, :] = packed
        for g in range(blk * groups_per_blk, (blk + 1) * groups_per_blk):
            pltpu.make_async_copy(sbuf.at[pl.ds(g * SEG, SEG)],
                                  xs_ref.at[pl.ds(pl.multiple_of(dst_ref[g], SEG), SEG)], sem).start()

    @pl.when(i == nt - 1)
    def _():
        wait_tile()
        zbuf[...] = jnp.zeros_like(zbuf)

        def zero_copy(src_row, dst_row, rows):
            return pltpu.make_async_copy(zbuf.at[pl.ds(src_row, rows)], xs_ref.at[pl.ds(dst_row, rows)], sem)

        bm = zbuf.shape[0]

        def tail_rows(e):
            return (cn_ref[e] + (bm - 1)) // bm * bm - cn_ref[e]

        def start_tail(e, c):
            _copy_segment(zero_copy, tail_rows(e), 0, ps_ref[e] + cn_ref[e])
            return c
        lax.fori_loop(0, N_EXPERTS, start_tail, 0)

        def start_block(b, c):
            zero_copy(0, pl.multiple_of(b * bm, bm), bm).start()
            return c
        n_rows_total = xs_ref.shape[0]
        lax.fori_loop(nu_ref[0], n_rows_total // bm, start_block, 0)

        def wait_tail(e, c):
            _wait_rows(zero_copy, tail_rows(e), limit=bm - SEG)
            return c
        lax.fori_loop(0, N_EXPERTS, wait_tail, 0)

        def wait_block(b, c):
            zero_copy(0, 0, bm).wait()
            return c
        lax.fori_loop(nu_ref[0], n_rows_total // bm, wait_block, 0)


def _dispatch(pad_start, counts, n_used, dst_table, h2, mt, rows, bm):
    t = h2.shape[0]
    tm = TM_OUT
    grid_spec = pltpu.PrefetchScalarGridSpec(
        num_scalar_prefetch=3,
        grid=(t // tm,),
        in_specs=[
            pl.BlockSpec((ROW_TABLE,), lambda i, *_: (i,), memory_space=pltpu.SMEM),
            pl.BlockSpec((tm, D_MODEL), lambda i, *_: (i, 0)),
            pl.BlockSpec((2 * TOP_K, tm), lambda i, *_: (i, 0)),
        ],
        out_specs=pl.BlockSpec(memory_space=pl.ANY),
        scratch_shapes=[pltpu.VMEM((R_LOCAL, HALF), U32), pltpu.VMEM((bm, HALF), U32),
                        pltpu.SemaphoreType.DMA(())],
    )
    return pl.pallas_call(
        _dispatch_kernel,
        out_shape=jax.ShapeDtypeStruct((rows, HALF), U32),
        grid_spec=grid_spec,
        compiler_params=pltpu.CompilerParams(
            dimension_semantics=("arbitrary",), vmem_limit_bytes=VMEM_LIMIT, has_side_effects=True),
        name="dispatch",
    )(pad_start, counts, n_used, dst_table, h2, mt)


def _experts_kernel(be_ref, nu_ref, xs_ref, wgu_ref, bgu_ref, wd_ref, bd_ref, ys_ref):
    i = pl.program_id(0)

    @pl.when(i < nu_ref[0])
    def _():
        lo, hi = _unpack_bf16_pair(xs_ref[...])
        xb = jnp.concatenate([lo.astype(BF16), hi.astype(BF16)], axis=1)
        gu = _dot(xb, wgu_ref[0]) + bgu_ref[0]
        gate = jnp.minimum(gu[:, 0:D_FF], SWIGLU_LIMIT)
        up = jnp.clip(gu[:, D_FF:], -SWIGLU_LIMIT, SWIGLU_LIMIT)
        act = (up + 1.0) * (gate * jax.nn.sigmoid(gate * SWIGLU_ALPHA))
        y = _dot(act.astype(BF16), wd_ref[0]) + bd_ref[0]
        ys_ref[...] = _pack_bf16_pair(y[:, 0:HALF], y[:, HALF:])

    @pl.when(i >= nu_ref[0])
    def _():
        ys_ref[...] = jnp.zeros_like(ys_ref)


def _experts(block_e, n_used, xs, p, bm):
    rows = xs.shape[0]
    nb = rows // bm
    grid_spec = pltpu.PrefetchScalarGridSpec(
        num_scalar_prefetch=2,
        grid=(nb,),
        in_specs=[
            pl.BlockSpec((bm, HALF), lambda i, be, *_: (i, 0)),
            pl.BlockSpec((1, D_MODEL, 2 * D_FF), lambda i, be, *_: (be[i], 0, 0)),
            pl.BlockSpec((1, 1, 2 * D_FF), lambda i, be, *_: (be[i], 0, 0)),
            pl.BlockSpec((1, D_FF, D_MODEL), lambda i, be, *_: (be[i], 0, 0)),
            pl.BlockSpec((1, 1, D_MODEL), lambda i, be, *_: (be[i], 0, 0)),
        ],
        out_specs=pl.BlockSpec((bm, HALF), lambda i, be, *_: (i, 0)),
    )
    return pl.pallas_call(
        _experts_kernel,
        out_shape=jax.ShapeDtypeStruct((rows, HALF), U32),
        grid_spec=grid_spec,
        compiler_params=pltpu.CompilerParams(
            dimension_semantics=("arbitrary",), vmem_limit_bytes=VMEM_LIMIT),
        name="experts",
    )(block_e, n_used, xs, p["w_gu"], p["b_gu"], p["w_d"], p["b_d"])


def _combine_kernel(cur_ref, nxt_ref, mt_ref, x1_ref, gf_ref, ys_ref, o_ref,
                    ybuf, gt_buf, ylo_buf, yhi_buf, sems):
    i = pl.program_id(0)
    nt = pl.num_programs(0)

    def issue(table_ref, slot):
        for g in range(R_LOCAL // SEG):
            pltpu.make_async_copy(ys_ref.at[pl.ds(pl.multiple_of(table_ref[g], SEG), SEG)],
                                  ybuf.at[slot, pl.ds(g * SEG, SEG)], sems.at[slot]).start()

    def wait_slot(slot):
        pltpu.make_async_copy(ys_ref.at[pl.ds(0, R_LOCAL)], ybuf.at[slot], sems.at[slot]).wait()

    @pl.when(i == 0)
    def _():
        issue(cur_ref, 0)

    slot = i % 2
    issue(nxt_ref, 1 - slot)

    gates = [mt_ref[TOP_K + k:TOP_K + k + 1, :].astype(BF16) for k in range(TOP_K)]
    for blk in range(R_LOCAL // SORT_BLK):
        rows = slice(blk * SORT_BLK, (blk + 1) * SORT_BLK)
        gt_buf[rows, :] = _onehot_rows(mt_ref, blk, gates)
    wait_slot(slot)
    for blk in range(R_LOCAL // SORT_BLK):
        rows = slice(blk * SORT_BLK, (blk + 1) * SORT_BLK)
        lo, hi = _unpack_bf16_pair(ybuf[slot, rows, :])
        ylo_buf[rows, :] = lo.astype(BF16)
        yhi_buf[rows, :] = hi.astype(BF16)
    contract_rows = (((0,), (0,)), ((), ()))
    g_t = gt_buf[...]
    x1 = x1_ref[...]
    acc = jnp.concatenate(
        [x1[:, 0:HALF] + lax.dot_general(g_t, ylo_buf[...], contract_rows, preferred_element_type=F32),
         x1[:, HALF:] + lax.dot_general(g_t, yhi_buf[...], contract_rows, preferred_element_type=F32)], axis=1)
    ms = jnp.mean(acc * acc, axis=-1, keepdims=True)
    o_ref[...] = acc * lax.rsqrt(ms + RMS_EPS) * gf_ref[...]

    @pl.when(i == nt - 1)
    def _():
        wait_slot(1 - slot)


def _combine(src_table, mt, x1, ys, p):
    t = x1.shape[0]
    tm = TM_OUT
    nt = t // tm
    grid_spec = pltpu.PrefetchScalarGridSpec(
        num_scalar_prefetch=0,
        grid=(nt,),
        in_specs=[
            pl.BlockSpec((ROW_TABLE,), lambda i: (i,), memory_space=pltpu.SMEM),
            pl.BlockSpec((ROW_TABLE,), lambda i: (jnp.minimum(i + 1, nt - 1),), memory_space=pltpu.SMEM),
            pl.BlockSpec((2 * TOP_K, tm), lambda i: (i, 0)),
            pl.BlockSpec((tm, D_MODEL), lambda i: (i, 0)),
            pl.BlockSpec((1, D_MODEL), lambda i: (0, 0)),
            pl.BlockSpec(memory_space=pl.ANY),
        ],
        out_specs=pl.BlockSpec((tm, D_MODEL), lambda i: (i, 0)),
        scratch_shapes=[pltpu.VMEM((2, R_LOCAL, HALF), U32), pltpu.VMEM((R_LOCAL, tm), BF16),
                        pltpu.VMEM((R_LOCAL, HALF), BF16), pltpu.VMEM((R_LOCAL, HALF), BF16),
                        pltpu.SemaphoreType.DMA((2,))],
    )
    return pl.pallas_call(
        _combine_kernel,
        out_shape=jax.ShapeDtypeStruct((t, D_MODEL), F32),
        grid_spec=grid_spec,
        compiler_params=pltpu.CompilerParams(
            dimension_semantics=("arbitrary",), vmem_limit_bytes=VMEM_LIMIT),
        name="combine",
    )(src_table, src_table, mt, x1, p["gf"], ys)


def _prepare_params(norm1_g, w_in, conv_w, conv_b, dt_bias_f, dt_bias_b, a_log_f, a_log_b, d_skip,
                    ssd_norm_g, gmlp_ln_g, gmlp_ln_b, w_spatial, b_spatial, w_out, norm2_g,
                    router_w, router_b, w_gate_up, b_gate_up, w_down, b_down, final_g):
    z_end = SSD_WIDTH
    xbc_end = z_end + CONV_CH
    dt_end = xbc_end + 2 * HEADS
    u_end = dt_end + GMLP_WIDTH
    w = w_in[0]
    pad_lanes = lambda a, fill=0.0: jnp.pad(a, ((0, 0), (0, LANES - a.shape[1])), constant_values=fill)
    head_lane = jnp.arange(LANES)[:, None]
    chan_head = (jnp.arange(SSD_WIDTH) // HEAD_DIM)[None, :]
    ar = jnp.arange(CHUNK)
    art = jnp.arange(TM_OUT)
    are = jnp.arange(N_EXPERTS)
    return {
        "g1": norm1_g[0][None, :],
        "w_z": w[:, :z_end].astype(BF16),
        "w_xbc": w[:, z_end:xbc_end].astype(BF16),
        "w_dt": pad_lanes(w[:, xbc_end:dt_end]).astype(BF16),
        "w_u": w[:, dt_end:u_end].astype(BF16),
        "w_v": w[:, u_end:].astype(BF16),
        "conv_w": jnp.pad(conv_w[0], ((0, 8 - CONV_WIDTH), (0, 0))),
        "conv_b": conv_b[0][None, :],
        "dt_bias": pad_lanes(jnp.concatenate([dt_bias_f[0], dt_bias_b[0]])[None, :]),
        "ln_g": gmlp_ln_g[0][None, :],
        "ln_b": gmlp_ln_b[0][None, :],
        "a_log": pad_lanes(jnp.concatenate([a_log_f[0], a_log_b[0]])[None, :]),
        "tri_lo": (ar[:, None] >= ar[None, :]).astype(BF16),
        "tri_up": (ar[:, None] <= ar[None, :]).astype(BF16),
        "rexp_f": jnp.tile((head_lane == chan_head).astype(BF16), (2, 1)),
        "rexp_b": jnp.tile((head_lane == chan_head + HEADS).astype(BF16), (2, 1)),
        "d_skip": jnp.repeat(d_skip[0], HEAD_DIM)[None, :],
        "gs": ssd_norm_g[0][None, :],
        "w_sp": w_spatial[0].astype(BF16),
        "b_sp": jnp.broadcast_to(b_spatial[0][:, :, None], (GMLP_HEADS, CHUNK, GMLP_HEAD_DIM)),
        "w_out": w_out[0].astype(BF16),
        "g2": norm2_g[0][None, :],
        "w_r": pad_lanes(router_w[0]).astype(BF16),
        "b_r": pad_lanes(router_b[0][None, :], NEG_BIG),
        "tok_before": (art[:, None] < art[None, :]).astype(BF16),
        "expert_before": (are[None, :] < are[:, None]).astype(BF16),
        "w_gu": w_gate_up[0].astype(BF16),
        "b_gu": b_gate_up[0][:, None, :],
        "w_d": w_down[0].astype(BF16),
        "b_d": b_down[0][:, None, :],
        "gf": final_g[None, :],
    }


def _encoder(x, p):
    batch, seq_len, _ = x.shape
    t = batch * seq_len
    x2d = x.reshape(t, D_MODEL)
    sz, xbc, dt, u, v = _inproj(x2d, seq_len, p)
    yf, yb = _ssd(xbc, dt, batch, seq_len, p)
    x1, h2, mt, tile_rows = _outproj(yf, yb, sz, u, v, x2d, p)
    tile_rows = tile_rows[:, 0, :]
    tn, tc, tl = (tile_rows[:, k * N_EXPERTS:(k + 1) * N_EXPERTS] for k in range(3))

    counts = tile_rows[-1, 3 * N_EXPERTS:]
    rows_per_expert = t * TOP_K // N_EXPERTS
    bm = next((b for b in BM_CHOICES if rows_per_expert >= BM_MIN_BLOCKS * b), BM_CHOICES[-1])
    spare_blocks = -(-R_LOCAL // bm)
    padded = (counts + bm - 1) // bm * bm
    pad_end = jnp.cumsum(padded)
    pad_start = (pad_end - padded).astype(I32)
    n_tiles = t // TM_OUT
    n_blocks = -(-(t * TOP_K + n_tiles * N_EXPERTS * (SEG - 1) + N_EXPERTS * (bm - 1)) // bm)
    n_used = (pad_end[-1] // bm).astype(I32)
    blk_row = jnp.minimum(jnp.arange(n_blocks + spare_blocks, dtype=I32), n_used - 1) * bm
    onehot_e = (pad_end[None, :] <= blk_row[:, None]).astype(I32)
    block_e = jnp.minimum(jnp.sum(onehot_e, axis=1), N_EXPERTS - 1).astype(I32)
    n_used = n_used.reshape(1)

    group_row = jnp.arange(R_LOCAL // SEG, dtype=I32) * SEG
    owner = jnp.sum(((tl + tn)[:, None, :] <= group_row[None, :, None]).astype(I32), axis=-1)
    is_owner = (owner[:, :, None] == jnp.arange(N_EXPERTS, dtype=I32)[None, None, :]).astype(I32)
    base = jnp.sum(is_owner * (pad_start[None, :] + tc - tl)[:, None, :], axis=-1)
    used = owner < N_EXPERTS
    pad_table = lambda a: jnp.pad(a, ((0, 0), (0, ROW_TABLE - a.shape[1]))).reshape(-1)
    dst_table = pad_table(jnp.where(used, base + group_row[None, :], n_blocks * bm + group_row[None, :]))
    src_table = pad_table(jnp.where(used, base + group_row[None, :], 0))

    xs = _dispatch(pad_start, counts, n_used, dst_table, h2, mt, (n_blocks + spare_blocks) * bm, bm)
    ys = _experts(block_e, n_used, xs, p, bm)
    y = _combine(src_table, mt, x1, ys, p)
    return y.reshape(batch, seq_len, D_MODEL)


def kernel(x_prompt, x_sample, norm1_g, w_in, conv_w, conv_b, dt_bias_f, dt_bias_b, a_log_f, a_log_b,
           d_skip, ssd_norm_g, gmlp_ln_g, gmlp_ln_b, w_spatial, b_spatial, w_out, norm2_g, router_w,
           router_b, w_gate_up, b_gate_up, w_down, b_down, final_g):
    p = _prepare_params(norm1_g, w_in, conv_w, conv_b, dt_bias_f, dt_bias_b, a_log_f, a_log_b, d_skip,
                        ssd_norm_g, gmlp_ln_g, gmlp_ln_b, w_spatial, b_spatial, w_out, norm2_g,
                        router_w, router_b, w_gate_up, b_gate_up, w_down, b_down, final_g)
    return (_encoder(x_prompt, p), _encoder(x_sample, p))
```

```python
import functools

import jax
import jax.numpy as jnp
from jax import lax
from jax.experimental import pallas as pl
from jax.experimental.pallas import tpu as pltpu

F32 = jnp.float32
BF16 = jnp.bfloat16
U32 = jnp.uint32
I32 = jnp.int32

D_MODEL = 1024
SSD_WIDTH = 1024
HEAD_DIM = 64
HEADS = 16
GROUPS = 2
HEADS_PER_GROUP = 8
STATE = 128
CHUNK = 128
CONV_WIDTH = 5
CONV_HALF = 2
CONV_CH = SSD_WIDTH + 2 * GROUPS * STATE
GMLP_WIDTH = 1024
GMLP_HEADS = 8
GMLP_HEAD_DIM = 128
N_EXPERTS = 32
TOP_K = 4
D_FF = 1024
SWIGLU_LIMIT = 7.0
SWIGLU_ALPHA = 1.702
RMS_EPS = 1e-5
LN_EPS = 1e-5

LANES = 128
HALO = 16
SSD_CHUNKS_PER_STEP = 4
TM_IN = 512
IN_COLS = 256
IN_ROWS = 512
TM_OUT = 512
BM_CHOICES = (512,)
BM_MIN_BLOCKS = 6
SEG = 8
COPY_ROWS = (32, 16, 8)
SORT_BLK = 256
R_LOCAL = -(-(TM_OUT * TOP_K + N_EXPERTS * (SEG - 1)) // SORT_BLK) * SORT_BLK
ROW_TABLE = 1024
HALF = D_MODEL // 2
VMEM_LIMIT = 56 * 1024 * 1024
NEG_BIG = -1e30
LOG2_E = 1.4426950408889634


def _dot(a, b):
    return jnp.dot(a, b, preferred_element_type=F32)


def _gelu(x):
    return 0.5 * x * (1.0 + lax.erf(x * (2.0 ** -0.5)))


def _pack_bf16_pair(lo, hi):
    lo_bits = lax.bitcast_convert_type(lo.astype(BF16).astype(F32), U32)
    hi_bits = lax.bitcast_convert_type(hi.astype(BF16).astype(F32), U32)
    return (lo_bits >> 16) | hi_bits


def _unpack_bf16_pair(w):
    lo = lax.bitcast_convert_type(w << 16, F32)
    hi = lax.bitcast_convert_type(w & jnp.uint32(0xFFFF0000), F32)
    return lo, hi


def _const_spec(shape):
    nd = len(shape)
    return pl.BlockSpec(shape, lambda *_: (0,) * nd)


def _inproj_kernel(x_ref, xp_ref, xn_ref, g1_ref, wz_ref, wxbc_ref, wdt_ref, wu_ref, wv_ref,
                   cw_ref, cb_ref, dtb_ref, lng_ref, lnb_ref,
                   sz_ref, xbc_ref, dt_ref, u_ref, v_ref, buf0, buf1, buf2, gbuf, *, tiles_per_seq):
    i = pl.program_id(0)
    tm = x_ref.shape[0]
    g1 = g1_ref[...]
    row_blocks = [(r0, IN_ROWS) for r0 in range(0, tm, IN_ROWS)]

    def norm(xv):
        ms = jnp.mean(xv * xv, axis=-1, keepdims=True)
        return xv * lax.rsqrt(ms + RMS_EPS) * g1

    keep_prev = jnp.where(i % tiles_per_seq == 0, 0.0, 1.0)
    keep_next = jnp.where(i % tiles_per_seq == tiles_per_seq - 1, 0.0, 1.0)
    h = norm(x_ref[...]).astype(BF16)
    h_all = jnp.concatenate([(norm(xp_ref[...]) * keep_prev).astype(BF16), h,
                             (norm(xn_ref[...]) * keep_next).astype(BF16)], axis=0)

    def silu_to(out_ref):
        def epilogue(buf, c0, wd):
            for r0, nr in row_blocks:
                r = buf[r0:r0 + nr, 0:wd]
                out_ref[r0:r0 + nr, c0:c0 + wd] = (r * jax.nn.sigmoid(r)).astype(BF16)
        return epilogue

    def conv_silu(buf, c0, wd):
        for r0, nr in row_blocks:
            acc = jnp.broadcast_to(cb_ref[:, c0:c0 + wd], (nr, wd))
            for k in range(CONV_WIDTH):
                acc = acc + cw_ref[k:k + 1, c0:c0 + wd] * buf[pl.ds(r0 + HALO - CONV_HALF + k, nr), 0:wd]
            xbc_ref[r0:r0 + nr, c0:c0 + wd] = (acc * jax.nn.sigmoid(acc)).astype(BF16)

    def softplus_dt(buf, c0, wd):
        dt_ref[...] = jax.nn.softplus(buf[0:tm, 0:wd] + dtb_ref[...])

    def gelu_u(buf, c0, wd):
        for r0, nr in row_blocks:
            u_ref[r0:r0 + nr, c0:c0 + wd] = _gelu(buf[r0:r0 + nr, 0:wd]).astype(BF16)

    sums = [jnp.zeros((nr, LANES), F32) for _, nr in row_blocks]

    def gelu_v(buf, c0, wd):
        for b, (r0, nr) in enumerate(row_blocks):
            g = _gelu(buf[r0:r0 + nr, 0:wd])
            gbuf[r0:r0 + nr, c0:c0 + wd] = g
            for l0 in range(0, wd, LANES):
                sums[b] = sums[b] + g[:, l0:l0 + LANES]

    heavy = [(h_all, wxbc_ref, c0, IN_COLS, conv_silu) for c0 in range(0, CONV_CH, IN_COLS)]
    light = [(h, wz_ref, c0, IN_COLS, silu_to(sz_ref)) for c0 in range(0, SSD_WIDTH, IN_COLS)]
    light += [(h, wu_ref, c0, IN_COLS, gelu_u) for c0 in range(0, GMLP_WIDTH, IN_COLS)]
    light += [(h, wv_ref, c0, IN_COLS, gelu_v) for c0 in range(0, GMLP_WIDTH, IN_COLS)]
    light.append((h, wdt_ref, 0, LANES, softplus_dt))
    stages = []
    while heavy or light:
        if heavy:
            stages.append(heavy.pop(0))
        if light:
            stages.append(light.pop(0))
    bufs = (buf0, buf1, buf2)
    ahead = len(bufs) - 1

    def matmul(j):
        lhs, w_ref, c0, wd, _ = stages[j]
        bufs[j % len(bufs)][0:lhs.shape[0], 0:wd] = _dot(lhs, w_ref[:, c0:c0 + wd])

    for j in range(ahead):
        matmul(j)
    for j, (_, _, c0, wd, epilogue) in enumerate(stages):
        if j + ahead < len(stages):
            matmul(j + ahead)
        epilogue(bufs[j % len(bufs)], c0, wd)

    inv_n = 1.0 / GMLP_WIDTH
    for b, (r0, nr) in enumerate(row_blocks):
        mu = jnp.sum(sums[b], axis=-1, keepdims=True) * inv_n
        sq = jnp.zeros((nr, LANES), F32)
        for l0 in range(0, GMLP_WIDTH, LANES):
            d = gbuf[r0:r0 + nr, l0:l0 + LANES] - mu
            sq = sq + d * d
        scale = lax.rsqrt(jnp.sum(sq, axis=-1, keepdims=True) * inv_n + LN_EPS)
        for c0 in range(0, GMLP_WIDTH, IN_COLS):
            g = gbuf[r0:r0 + nr, c0:c0 + IN_COLS]
            v_ref[r0:r0 + nr, c0:c0 + IN_COLS] = (
                (g - mu) * scale * lng_ref[:, c0:c0 + IN_COLS] + lnb_ref[:, c0:c0 + IN_COLS]).astype(BF16)


def _inproj(x2d, seq_len, p):
    t = x2d.shape[0]
    tm = TM_IN
    tiles_per_seq = seq_len // tm
    nh = tm // HALO
    last_halo = t // HALO - 1
    kern = functools.partial(_inproj_kernel, tiles_per_seq=tiles_per_seq)
    row = lambda i: (i, 0)
    out_shape = (
        jax.ShapeDtypeStruct((t, SSD_WIDTH), BF16),
        jax.ShapeDtypeStruct((t, CONV_CH), BF16),
        jax.ShapeDtypeStruct((t, LANES), F32),
        jax.ShapeDtypeStruct((t, GMLP_WIDTH), BF16),
        jax.ShapeDtypeStruct((t, GMLP_WIDTH), BF16),
    )
    return pl.pallas_call(
        kern,
        out_shape=out_shape,
        grid=(t // tm,),
        in_specs=[
            pl.BlockSpec((tm, D_MODEL), row),
            pl.BlockSpec((HALO, D_MODEL), lambda i: (jnp.maximum(i * nh - 1, 0), 0)),
            pl.BlockSpec((HALO, D_MODEL), lambda i: (jnp.minimum((i + 1) * nh, last_halo), 0)),
            _const_spec((1, D_MODEL)),
            _const_spec((D_MODEL, SSD_WIDTH)),
            _const_spec((D_MODEL, CONV_CH)),
            _const_spec((D_MODEL, LANES)),
            _const_spec((D_MODEL, GMLP_WIDTH)),
            _const_spec((D_MODEL, GMLP_WIDTH)),
            _const_spec((8, CONV_CH)),
            _const_spec((1, CONV_CH)),
            _const_spec((1, LANES)),
            _const_spec((1, GMLP_WIDTH)),
            _const_spec((1, GMLP_WIDTH)),
        ],
        out_specs=(
            pl.BlockSpec((tm, SSD_WIDTH), row),
            pl.BlockSpec((tm, CONV_CH), row),
            pl.BlockSpec((tm, LANES), row),
            pl.BlockSpec((tm, GMLP_WIDTH), row),
            pl.BlockSpec((tm, GMLP_WIDTH), row),
        ),
        scratch_shapes=[pltpu.VMEM((tm + 2 * HALO, IN_COLS), F32)] * 3 + [pltpu.VMEM((tm, GMLP_WIDTH), F32)],
        compiler_params=pltpu.CompilerParams(
            dimension_semantics=("arbitrary",), vmem_limit_bytes=VMEM_LIMIT),
        name="inproj",
    )(x2d, x2d, x2d, p["g1"], p["w_z"], p["w_xbc"], p["w_dt"], p["w_u"], p["w_v"],
      p["conv_w"], p["conv_b"], p["dt_bias"], p["ln_g"], p["ln_b"])


def _split2(q):
    hi = q.astype(BF16)
    lo = (q - hi.astype(F32)).astype(BF16)
    return hi, lo


def _ssd_role(xbc_ref, dt_ref, arow, tri_ref, rexp_ref, dskip_ref, s_ref, y_ref, *, r0, backward):
    lane0 = HEADS if backward else 0
    rows = slice(r0, r0 + CHUNK)
    dt_all = dt_ref[rows, :]
    adt = dt_all * arow
    a1 = adt.astype(BF16)
    r1 = adt - a1.astype(F32)
    a2 = r1.astype(BF16)
    a3 = (r1 - a2.astype(F32)).astype(BF16)
    acs3 = _dot(tri_ref[...], jnp.concatenate([a1, a2, a3], axis=1))
    acs = acs3[:, 0:LANES] + acs3[:, LANES:2 * LANES] + acs3[:, 2 * LANES:]
    edge = 0 if backward else CHUNK - 1
    tot = acs[edge:edge + 1, :]
    decay = jnp.exp(tot - acs)
    eacs = jnp.exp(acs)
    acs2 = acs * LOG2_E
    acs2_t = acs2.T

    hi, lo = _split2(jnp.concatenate([dt_all, decay, eacs], axis=0))
    expanded = _dot(jnp.concatenate([hi, lo], axis=1), rexp_ref[...])
    dt_e = expanded[0:CHUNK]
    decay_e = expanded[CHUNK:2 * CHUNK]
    eacs_e = expanded[2 * CHUNK:]
    etot_e = eacs_e[edge:edge + 1, :]

    xf = xbc_ref[rows, 0:SSD_WIDTH].astype(F32)
    xdt = xf * dt_e
    xdd = (xdt * decay_e).astype(BF16)
    xdt_b = xdt.astype(BF16)

    li = lax.broadcasted_iota(I32, (CHUNK, CHUNK), 0)
    si = lax.broadcasted_iota(I32, (CHUNK, CHUNK), 1)
    keep = (si >= li) if backward else (si <= li)
    left = si < HEAD_DIM
    zero_b = jnp.zeros((CHUNK, LANES), BF16)

    for g in range(GROUPS):
        b_g = xbc_ref[rows, SSD_WIDTH + g * STATE:SSD_WIDTH + (g + 1) * STATE]
        c_g = xbc_ref[rows, SSD_WIDTH + (GROUPS + g) * STATE:SSD_WIDTH + (GROUPS + g + 1) * STATE]
        b_t = b_g.astype(F32).T.astype(BF16)
        cb = _dot(c_g, b_t)
        gc0 = g * HEADS_PER_GROUP * HEAD_DIM
        gw = HEADS_PER_GROUP * HEAD_DIM
        s_old = s_ref[:, gc0:gc0 + gw]
        y_off = _dot(c_g, s_old.astype(BF16))
        s_new = _dot(b_t, xdd[:, gc0:gc0 + gw])
        s_ref[:, gc0:gc0 + gw] = s_old * etot_e[:, gc0:gc0 + gw] + s_new
        for pp in range(HEADS_PER_GROUP // 2):
            c0 = gc0 + pp * LANES
            ms = []
            for q in range(2):
                hl = lane0 + g * HEADS_PER_GROUP + 2 * pp + q
                seg = acs2[:, hl:hl + 1] - acs2_t[hl:hl + 1, :]
                lmat = jnp.exp2(jnp.where(keep, seg, -jnp.inf))
                ms.append((cb * lmat).astype(BF16))
            lhs = jnp.concatenate(ms, axis=1)
            xp = xdt_b[:, c0:c0 + LANES]
            rhs = jnp.concatenate([jnp.where(left, xp, zero_b), jnp.where(left, zero_b, xp)], axis=0)
            y = _dot(lhs, rhs) + y_off[:, pp * LANES:(pp + 1) * LANES] * eacs_e[:, c0:c0 + LANES]
            if not backward:
                y = y + xf[:, c0:c0 + LANES] * dskip_ref[:, c0:c0 + LANES]
            y_ref[rows, c0:c0 + LANES] = y.astype(BF16)


def _ssd_kernel(xf_ref, dtf_ref, xb_ref, dtb_ref, alog_ref, trilo_ref, triup_ref, rf_ref, rb_ref,
                dskip_ref, yf_ref, yb_ref, sf_ref, sb_ref):
    @pl.when(pl.program_id(1) == 0)
    def _():
        sf_ref[...] = jnp.zeros_like(sf_ref)
        sb_ref[...] = jnp.zeros_like(sb_ref)

    arow = -jnp.exp(alog_ref[...])
    n_sub = xf_ref.shape[0] // CHUNK
    for j in range(n_sub):
        _ssd_role(xf_ref, dtf_ref, arow, trilo_ref, rf_ref, dskip_ref, sf_ref, yf_ref,
                  r0=j * CHUNK, backward=False)
        _ssd_role(xb_ref, dtb_ref, arow, triup_ref, rb_ref, dskip_ref, sb_ref, yb_ref,
                  r0=(n_sub - 1 - j) * CHUNK, backward=True)


def _ssd(xbc, dt, batch, seq_len, p):
    t = xbc.shape[0]
    rows = SSD_CHUNKS_PER_STEP * CHUNK
    nc = seq_len // rows
    fwd = lambda b, c: (b * nc + c, 0)
    bwd = lambda b, c: (b * nc + nc - 1 - c, 0)
    return pl.pallas_call(
        _ssd_kernel,
        out_shape=(jax.ShapeDtypeStruct((t, SSD_WIDTH), BF16), jax.ShapeDtypeStruct((t, SSD_WIDTH), BF16)),
        grid=(batch, nc),
        in_specs=[
            pl.BlockSpec((rows, CONV_CH), fwd),
            pl.BlockSpec((rows, LANES), fwd),
            pl.BlockSpec((rows, CONV_CH), bwd),
            pl.BlockSpec((rows, LANES), bwd),
            _const_spec((1, LANES)),
            _const_spec((CHUNK, CHUNK)),
            _const_spec((CHUNK, CHUNK)),
            _const_spec((2 * LANES, SSD_WIDTH)),
            _const_spec((2 * LANES, SSD_WIDTH)),
            _const_spec((1, SSD_WIDTH)),
        ],
        out_specs=(pl.BlockSpec((rows, SSD_WIDTH), fwd), pl.BlockSpec((rows, SSD_WIDTH), bwd)),
        scratch_shapes=[pltpu.VMEM((STATE, SSD_WIDTH), F32), pltpu.VMEM((STATE, SSD_WIDTH), F32)],
        compiler_params=pltpu.CompilerParams(
            dimension_semantics=("arbitrary", "arbitrary"), vmem_limit_bytes=VMEM_LIMIT),
        name="ssd",
    )(xbc, dt, xbc, dt, p["a_log"], p["tri_lo"], p["tri_up"], p["rexp_f"], p["rexp_b"], p["d_skip"])


def _outproj_kernel(yf_ref, yb_ref, sz_ref, u_ref, v_ref, x_ref, gs_ref, wsp_ref, bsp_ref, wo_ref,
                    g2_ref, wr_ref, rb_ref, tri_ref, trie_ref,
                    x1_ref, h2_ref, mt_ref, tm_ref,
                    gm_buf, carry, logit_t):
    i = pl.program_id(0)
    tm = x_ref.shape[0]

    @pl.when(i == 0)
    def _():
        carry[...] = jnp.zeros_like(carry)
        logit_t[...] = jnp.zeros_like(logit_t)

    _route_tile(logit_t[...], jnp.where(i > 0, 1.0, 0.0), tri_ref, trie_ref, carry, mt_ref, tm_ref)

    gated = (yf_ref[...].astype(F32) + yb_ref[...].astype(F32)) * sz_ref[...].astype(F32)
    ms = jnp.mean(gated * gated, axis=-1, keepdims=True)
    ssd = (gated * lax.rsqrt(ms + RMS_EPS) * gs_ref[...]).astype(BF16)

    for c in range(tm // CHUNK):
        r0 = c * CHUNK
        for hh in range(GMLP_HEADS):
            c0 = hh * GMLP_HEAD_DIM
            sv = _dot(wsp_ref[hh], v_ref[r0:r0 + CHUNK, c0:c0 + GMLP_HEAD_DIM]) + bsp_ref[hh]
            gm_buf[r0:r0 + CHUNK, c0:c0 + GMLP_HEAD_DIM] = (
                u_ref[r0:r0 + CHUNK, c0:c0 + GMLP_HEAD_DIM].astype(F32) * sv).astype(BF16)

    x1 = x_ref[...] + _dot(jnp.concatenate([ssd, gm_buf[...]], axis=1), wo_ref[...])
    x1_ref[...] = x1

    ms2 = jnp.mean(x1 * x1, axis=-1, keepdims=True)
    h2 = (x1 * lax.rsqrt(ms2 + RMS_EPS) * g2_ref[...]).astype(BF16)
    h2_ref[...] = h2

    logits = _dot(h2, wr_ref[...]) + rb_ref[...]
    logit_t[...] = logits.T[0:N_EXPERTS, :]


def _route_tile(work, live, tri_ref, trie_ref, carry, mt_ref, tm_ref):
    tm = work.shape[1]
    erow = lax.broadcasted_iota(I32, (N_EXPERTS, tm), 0).astype(F32)
    sel_all = jnp.zeros((N_EXPERTS, tm), F32)
    sels, vals = [], []
    for _ in range(TOP_K):
        m = jnp.max(work, axis=0, keepdims=True)
        idx = jnp.min(jnp.where(work == m, erow, float(N_EXPERTS)), axis=0, keepdims=True)
        sel = erow == idx
        sel_all = jnp.where(sel, 1.0, sel_all)
        work = jnp.where(sel, -jnp.inf, work)
        sels.append(sel)
        vals.append(m)
    exps = [jnp.exp(v - vals[0]) for v in vals]
    denom = exps[0] + exps[1] + exps[2] + exps[3]

    lrank = _dot(sel_all.astype(BF16), tri_ref[...])
    n_e = jnp.sum(sel_all, axis=1, keepdims=True)
    n_al = jnp.broadcast_to(jnp.floor((n_e + (SEG - 1)) * (1.0 / SEG)) * (SEG * live), (N_EXPERTS, LANES))
    lstart = _dot(trie_ref[...], n_al.astype(BF16))
    old_cnt = carry[...]
    new_cnt = old_cnt + n_al
    carry[...] = new_cnt
    tile_row = jnp.concatenate([n_al, old_cnt, lstart, new_cnt], axis=0).T[0:1, :]
    tm_ref[0] = tile_row.astype(I32)

    local_row = lstart[:, 0:1] + lrank
    out_row = lax.broadcasted_iota(I32, (2 * TOP_K, tm), 0)
    meta = jnp.zeros((2 * TOP_K, tm), F32)
    for k in range(TOP_K):
        dk = jnp.sum(jnp.where(sels[k], local_row, 0.0), axis=0, keepdims=True)
        meta = jnp.where(out_row == k, dk, meta)
        meta = jnp.where(out_row == TOP_K + k, exps[k] / denom, meta)
    mt_ref[...] = meta


def _outproj(yf, yb, sz, u, v, x2d, p):
    t = x2d.shape[0]
    tm = TM_OUT
    row = lambda i: (i, 0)
    nt = t // tm
    out_shape = (
        jax.ShapeDtypeStruct((t, D_MODEL), F32),
        jax.ShapeDtypeStruct((t, D_MODEL), BF16),
        jax.ShapeDtypeStruct((nt * 2 * TOP_K, tm), F32),
        jax.ShapeDtypeStruct((nt, 1, LANES), I32),
    )
    row = lambda i: (jnp.minimum(i, nt - 1), 0)
    routed = lambda i: jnp.maximum(i - 1, 0)
    bf_row = pl.BlockSpec((tm, D_MODEL), row)
    return pl.pallas_call(
        _outproj_kernel,
        out_shape=out_shape,
        grid=(nt + 1,),
        in_specs=[
            bf_row, bf_row, bf_row, bf_row, bf_row, bf_row,
            _const_spec((1, SSD_WIDTH)),
            _const_spec((GMLP_HEADS, CHUNK, CHUNK)),
            _const_spec((GMLP_HEADS, CHUNK, GMLP_HEAD_DIM)),
            _const_spec((SSD_WIDTH + GMLP_WIDTH, D_MODEL)),
            _const_spec((1, D_MODEL)),
            _const_spec((D_MODEL, LANES)),
            _const_spec((1, LANES)),
            _const_spec((tm, tm)),
            _const_spec((N_EXPERTS, N_EXPERTS)),
        ],
        out_specs=(
            pl.BlockSpec((tm, D_MODEL), row),
            pl.BlockSpec((tm, D_MODEL), row),
            pl.BlockSpec((2 * TOP_K, tm), lambda i: (routed(i), 0)),
            pl.BlockSpec((1, 1, LANES), lambda i: (routed(i), 0, 0)),
        ),
        scratch_shapes=[pltpu.VMEM((tm, GMLP_WIDTH), BF16), pltpu.VMEM((N_EXPERTS, LANES), F32),
                        pltpu.VMEM((N_EXPERTS, tm), F32)],
        compiler_params=pltpu.CompilerParams(
            dimension_semantics=("arbitrary",), vmem_limit_bytes=VMEM_LIMIT),
        name="outproj",
    )(yf, yb, sz, u, v, x2d, p["gs"], p["w_sp"], p["b_sp"], p["w_out"], p["g2"], p["w_r"], p["b_r"],
      p["tok_before"], p["expert_before"])


def _copy_segment(make_copy, n, src_row, dst_row):
    big, mid, small = COPY_ROWS

    def body(c, carry_):
        make_copy(pl.multiple_of(src_row + c * big, SEG), pl.multiple_of(dst_row + c * big, SEG), big).start()
        return carry_
    nbig = n // big
    lax.fori_loop(0, nbig, body, 0)
    done = nbig * big

    @pl.when((n & mid) != 0)
    def _():
        make_copy(pl.multiple_of(src_row + done, SEG), pl.multiple_of(dst_row + done, SEG), mid).start()

    done2 = done + (n & mid)

    @pl.when((n & small) != 0)
    def _():
        make_copy(pl.multiple_of(src_row + done2, SEG), pl.multiple_of(dst_row + done2, SEG), small).start()


def _wait_rows(make_copy, nrows, limit=R_LOCAL):
    w = 1 << (limit.bit_length() - 1)
    while w >= SEG:
        @pl.when((nrows & w) != 0)
        def _(w=w):
            make_copy(0, 0, w).wait()
        w //= 2


def _onehot_rows(mt_ref, blk, values):
    tm = mt_ref.shape[1]
    riota = lax.broadcasted_iota(I32, (SORT_BLK, tm), 0).astype(F32).astype(BF16)
    out = jnp.zeros((SORT_BLK, tm), BF16)
    for k in reversed(range(TOP_K)):
        off = (mt_ref[k:k + 1, :] - float(blk * SORT_BLK)).astype(BF16)
        out = jnp.where(riota == off, values[k], out)
    return out


def _dispatch_kernel(ps_ref, cn_ref, nu_ref, dst_ref, h2_ref, mt_ref, xs_ref, sbuf, zbuf, sem):
    i = pl.program_id(0)
    nt = pl.num_programs(0)
    tm = h2_ref.shape[0]
    groups_per_blk = SORT_BLK // SEG

    def wait_tile():
        pltpu.make_async_copy(sbuf, xs_ref.at[pl.ds(0, R_LOCAL)], sem).wait()

    h2 = h2_ref[...]
    ones = [jnp.ones((1, tm), BF16)] * TOP_K
    for blk in range(R_LOCAL // SORT_BLK):
        s = _dot(_onehot_rows(mt_ref, blk, ones), h2)
        lo_bits = lax.bitcast_convert_type(s[:, 0:HALF], U32)
        hi_bits = lax.bitcast_convert_type(s[:, HALF:], U32)
        packed = (lo_bits >> 16) | hi_bits
        if blk == 0:
            @pl.when(i > 0)
            def _():
                wait_tile()
        sbuf[blk * SORT_BLK:(blk + 1) * SORT_BLK, :] = packed
        for g in range(blk * groups_per_blk, (blk + 1) * groups_per_blk):
            pltpu.make_async_copy(sbuf.at[pl.ds(g * SEG, SEG)],
                                  xs_ref.at[pl.ds(pl.multiple_of(dst_ref[g], SEG), SEG)], sem).start()

    @pl.when(i == nt - 1)
    def _():
        wait_tile()
        zbuf[...] = jnp.zeros_like(zbuf)

        def zero_copy(src_row, dst_row, rows):
            return pltpu.make_async_copy(zbuf.at[pl.ds(src_row, rows)], xs_ref.at[pl.ds(dst_row, rows)], sem)

        bm = zbuf.shape[0]

        def tail_rows(e):
            return (cn_ref[e] + (bm - 1)) // bm * bm - cn_ref[e]

        def start_tail(e, c):
            _copy_segment(zero_copy, tail_rows(e), 0, ps_ref[e] + cn_ref[e])
            return c
        lax.fori_loop(0, N_EXPERTS, start_tail, 0)

        def start_block(b, c):
            zero_copy(0, pl.multiple_of(b * bm, bm), bm).start()
            return c
        n_rows_total = xs_ref.shape[0]
        lax.fori_loop(nu_ref[0], n_rows_total // bm, start_block, 0)

        def wait_tail(e, c):
            _wait_rows(zero_copy, tail_rows(e), limit=bm - SEG)
            return c
        lax.fori_loop(0, N_EXPERTS, wait_tail, 0)

        def wait_block(b, c):
            zero_copy(0, 0, bm).wait()
            return c
        lax.fori_loop(nu_ref[0], n_rows_total // bm, wait_block, 0)


def _dispatch(pad_start, counts, n_used, dst_table, h2, mt, rows, bm):
    t = h2.shape[0]
    tm = TM_OUT
    grid_spec = pltpu.PrefetchScalarGridSpec(
        num_scalar_prefetch=3,
        grid=(t // tm,),
        in_specs=[
            pl.BlockSpec((ROW_TABLE,), lambda i, *_: (i,), memory_space=pltpu.SMEM),
            pl.BlockSpec((tm, D_MODEL), lambda i, *_: (i, 0)),
            pl.BlockSpec((2 * TOP_K, tm), lambda i, *_: (i, 0)),
        ],
        out_specs=pl.BlockSpec(memory_space=pl.ANY),
        scratch_shapes=[pltpu.VMEM((R_LOCAL, HALF), U32), pltpu.VMEM((bm, HALF), U32),
                        pltpu.SemaphoreType.DMA(())],
    )
    return pl.pallas_call(
        _dispatch_kernel,
        out_shape=jax.ShapeDtypeStruct((rows, HALF), U32),
        grid_spec=grid_spec,
        compiler_params=pltpu.CompilerParams(
            dimension_semantics=("arbitrary",), vmem_limit_bytes=VMEM_LIMIT, has_side_effects=True),
        name="dispatch",
    )(pad_start, counts, n_used, dst_table, h2, mt)


def _experts_kernel(be_ref, nu_ref, xs_ref, wgu_ref, bgu_ref, wd_ref, bd_ref, ys_ref, wgu_bf, wd_bf):
    i = pl.program_id(0)
    used = i < nu_ref[0]

    @pl.when(used & ((i == 0) | (be_ref[i] != be_ref[jnp.maximum(i - 1, 0)])))
    def _():
        wgu_bf[...] = wgu_ref[0].astype(BF16)
        wd_bf[...] = wd_ref[0].astype(BF16)

    @pl.when(used)
    def _():
        lo, hi = _unpack_bf16_pair(xs_ref[...])
        xb = jnp.concatenate([lo.astype(BF16), hi.astype(BF16)], axis=1)
        gu = _dot(xb, wgu_bf[...]) + bgu_ref[0]
        gate = jnp.minimum(gu[:, 0:D_FF], SWIGLU_LIMIT)
        up = jnp.clip(gu[:, D_FF:], -SWIGLU_LIMIT, SWIGLU_LIMIT)
        act = (up + 1.0) * (gate * jax.nn.sigmoid(gate * SWIGLU_ALPHA))
        y = _dot(act.astype(BF16), wd_bf[...]) + bd_ref[0]
        ys_ref[...] = _pack_bf16_pair(y[:, 0:HALF], y[:, HALF:])

    @pl.when(i >= nu_ref[0])
    def _():
        ys_ref[...] = jnp.zeros_like(ys_ref)


def _experts(block_e, n_used, xs, p, bm):
    rows = xs.shape[0]
    nb = rows // bm
    grid_spec = pltpu.PrefetchScalarGridSpec(
        num_scalar_prefetch=2,
        grid=(nb,),
        in_specs=[
            pl.BlockSpec((bm, HALF), lambda i, be, *_: (i, 0)),
            pl.BlockSpec((1, D_MODEL, 2 * D_FF), lambda i, be, *_: (be[i], 0, 0)),
            pl.BlockSpec((1, 1, 2 * D_FF), lambda i, be, *_: (be[i], 0, 0)),
            pl.BlockSpec((1, D_FF, D_MODEL), lambda i, be, *_: (be[i], 0, 0)),
            pl.BlockSpec((1, 1, D_MODEL), lambda i, be, *_: (be[i], 0, 0)),
        ],
        out_specs=pl.BlockSpec((bm, HALF), lambda i, be, *_: (i, 0)),
        scratch_shapes=[pltpu.VMEM((D_MODEL, 2 * D_FF), BF16), pltpu.VMEM((D_FF, D_MODEL), BF16)],
    )
    return pl.pallas_call(
        _experts_kernel,
        out_shape=jax.ShapeDtypeStruct((rows, HALF), U32),
        grid_spec=grid_spec,
        compiler_params=pltpu.CompilerParams(
            dimension_semantics=("arbitrary",), vmem_limit_bytes=VMEM_LIMIT),
        name="experts",
    )(block_e, n_used, xs, p["w_gu"], p["b_gu"], p["w_d"], p["b_d"])


def _combine_kernel(cur_ref, nxt_ref, mt_ref, x1_ref, gf_ref, ys_ref, o_ref,
                    ybuf, gt_buf, ylo_buf, yhi_buf, sems):
    i = pl.program_id(0)
    nt = pl.num_programs(0)

    def issue(table_ref, slot):
        for g in range(R_LOCAL // SEG):
            pltpu.make_async_copy(ys_ref.at[pl.ds(pl.multiple_of(table_ref[g], SEG), SEG)],
                                  ybuf.at[slot, pl.ds(g * SEG, SEG)], sems.at[slot]).start()

    def wait_slot(slot):
        pltpu.make_async_copy(ys_ref.at[pl.ds(0, R_LOCAL)], ybuf.at[slot], sems.at[slot]).wait()

    @pl.when(i == 0)
    def _():
        issue(cur_ref, 0)

    slot = i % 2
    issue(nxt_ref, 1 - slot)

    gates = [mt_ref[TOP_K + k:TOP_K + k + 1, :].astype(BF16) for k in range(TOP_K)]
    for blk in range(R_LOCAL // SORT_BLK):
        rows = slice(blk * SORT_BLK, (blk + 1) * SORT_BLK)
        gt_buf[rows, :] = _onehot_rows(mt_ref, blk, gates)
    wait_slot(slot)
    for blk in range(R_LOCAL // SORT_BLK):
        rows = slice(blk * SORT_BLK, (blk + 1) * SORT_BLK)
        lo, hi = _unpack_bf16_pair(ybuf[slot, rows, :])
        ylo_buf[rows, :] = lo.astype(BF16)
        yhi_buf[rows, :] = hi.astype(BF16)
    contract_rows = (((0,), (0,)), ((), ()))
    g_t = gt_buf[...]
    x1 = x1_ref[...]
    acc = jnp.concatenate(
        [x1[:, 0:HALF] + lax.dot_general(g_t, ylo_buf[...], contract_rows, preferred_element_type=F32),
         x1[:, HALF:] + lax.dot_general(g_t, yhi_buf[...], contract_rows, preferred_element_type=F32)], axis=1)
    ms = jnp.mean(acc * acc, axis=-1, keepdims=True)
    o_ref[...] = acc * lax.rsqrt(ms + RMS_EPS) * gf_ref[...]

    @pl.when(i == nt - 1)
    def _():
        wait_slot(1 - slot)


def _combine(src_table, mt, x1, ys, p):
    t = x1.shape[0]
    tm = TM_OUT
    nt = t // tm
    grid_spec = pltpu.PrefetchScalarGridSpec(
        num_scalar_prefetch=0,
        grid=(nt,),
        in_specs=[
            pl.BlockSpec((ROW_TABLE,), lambda i: (i,), memory_space=pltpu.SMEM),
            pl.BlockSpec((ROW_TABLE,), lambda i: (jnp.minimum(i + 1, nt - 1),), memory_space=pltpu.SMEM),
            pl.BlockSpec((2 * TOP_K, tm), lambda i: (i, 0)),
            pl.BlockSpec((tm, D_MODEL), lambda i: (i, 0)),
            pl.BlockSpec((1, D_MODEL), lambda i: (0, 0)),
            pl.BlockSpec(memory_space=pl.ANY),
        ],
        out_specs=pl.BlockSpec((tm, D_MODEL), lambda i: (i, 0)),
        scratch_shapes=[pltpu.VMEM((2, R_LOCAL, HALF), U32), pltpu.VMEM((R_LOCAL, tm), BF16),
                        pltpu.VMEM((R_LOCAL, HALF), BF16), pltpu.VMEM((R_LOCAL, HALF), BF16),
                        pltpu.SemaphoreType.DMA((2,))],
    )
    return pl.pallas_call(
        _combine_kernel,
        out_shape=jax.ShapeDtypeStruct((t, D_MODEL), F32),
        grid_spec=grid_spec,
        compiler_params=pltpu.CompilerParams(
            dimension_semantics=("arbitrary",), vmem_limit_bytes=VMEM_LIMIT),
        name="combine",
    )(src_table, src_table, mt, x1, p["gf"], ys)


def _prepare_params(norm1_g, w_in, conv_w, conv_b, dt_bias_f, dt_bias_b, a_log_f, a_log_b, d_skip,
                    ssd_norm_g, gmlp_ln_g, gmlp_ln_b, w_spatial, b_spatial, w_out, norm2_g,
                    router_w, router_b, w_gate_up, b_gate_up, w_down, b_down, final_g):
    z_end = SSD_WIDTH
    xbc_end = z_end + CONV_CH
    dt_end = xbc_end + 2 * HEADS
    u_end = dt_end + GMLP_WIDTH
    w = w_in[0]
    pad_lanes = lambda a, fill=0.0: jnp.pad(a, ((0, 0), (0, LANES - a.shape[1])), constant_values=fill)
    head_lane = jnp.arange(LANES)[:, None]
    chan_head = (jnp.arange(SSD_WIDTH) // HEAD_DIM)[None, :]
    ar = jnp.arange(CHUNK)
    art = jnp.arange(TM_OUT)
    are = jnp.arange(N_EXPERTS)
    return {
        "g1": norm1_g[0][None, :],
        "w_z": w[:, :z_end].astype(BF16),
        "w_xbc": w[:, z_end:xbc_end].astype(BF16),
        "w_dt": pad_lanes(w[:, xbc_end:dt_end]).astype(BF16),
        "w_u": w[:, dt_end:u_end].astype(BF16),
        "w_v": w[:, u_end:].astype(BF16),
        "conv_w": jnp.pad(conv_w[0], ((0, 8 - CONV_WIDTH), (0, 0))),
        "conv_b": conv_b[0][None, :],
        "dt_bias": pad_lanes(jnp.concatenate([dt_bias_f[0], dt_bias_b[0]])[None, :]),
        "ln_g": gmlp_ln_g[0][None, :],
        "ln_b": gmlp_ln_b[0][None, :],
        "a_log": pad_lanes(jnp.concatenate([a_log_f[0], a_log_b[0]])[None, :]),
        "tri_lo": (ar[:, None] >= ar[None, :]).astype(BF16),
        "tri_up": (ar[:, None] <= ar[None, :]).astype(BF16),
        "rexp_f": jnp.tile((head_lane == chan_head).astype(BF16), (2, 1)),
        "rexp_b": jnp.tile((head_lane == chan_head + HEADS).astype(BF16), (2, 1)),
        "d_skip": jnp.repeat(d_skip[0], HEAD_DIM)[None, :],
        "gs": ssd_norm_g[0][None, :],
        "w_sp": w_spatial[0].astype(BF16),
        "b_sp": jnp.broadcast_to(b_spatial[0][:, :, None], (GMLP_HEADS, CHUNK, GMLP_HEAD_DIM)),
        "w_out": w_out[0].astype(BF16),
        "g2": norm2_g[0][None, :],
        "w_r": pad_lanes(router_w[0]).astype(BF16),
        "b_r": pad_lanes(router_b[0][None, :], NEG_BIG),
        "tok_before": (art[:, None] < art[None, :]).astype(BF16),
        "expert_before": (are[None, :] < are[:, None]).astype(BF16),
        "w_gu": w_gate_up[0],
        "b_gu": b_gate_up[0][:, None, :],
        "w_d": w_down[0],
        "b_d": b_down[0][:, None, :],
        "gf": final_g[None, :],
    }


def _encoder(x, p):
    batch, seq_len, _ = x.shape
    t = batch * seq_len
    x2d = x.reshape(t, D_MODEL)
    sz, xbc, dt, u, v = _inproj(x2d, seq_len, p)
    yf, yb = _ssd(xbc, dt, batch, seq_len, p)
    x1, h2, mt, tile_rows = _outproj(yf, yb, sz, u, v, x2d, p)
    tile_rows = tile_rows[:, 0, :]
    tn, tc, tl = (tile_rows[:, k * N_EXPERTS:(k + 1) * N_EXPERTS] for k in range(3))

    counts = tile_rows[-1, 3 * N_EXPERTS:]
    rows_per_expert = t * TOP_K // N_EXPERTS
    bm = next((b for b in BM_CHOICES if rows_per_expert >= BM_MIN_BLOCKS * b), BM_CHOICES[-1])
    spare_blocks = -(-R_LOCAL // bm)
    padded = (counts + bm - 1) // bm * bm
    pad_end = jnp.cumsum(padded)
    pad_start = (pad_end - padded).astype(I32)
    n_tiles = t // TM_OUT
    n_blocks = -(-(t * TOP_K + n_tiles * N_EXPERTS * (SEG - 1) + N_EXPERTS * (bm - 1)) // bm)
    n_used = (pad_end[-1] // bm).astype(I32)
    blk_row = jnp.minimum(jnp.arange(n_blocks + spare_blocks, dtype=I32), n_used - 1) * bm
    onehot_e = (pad_end[None, :] <= blk_row[:, None]).astype(I32)
    block_e = jnp.minimum(jnp.sum(onehot_e, axis=1), N_EXPERTS - 1).astype(I32)
    n_used = n_used.reshape(1)

    group_row = jnp.arange(R_LOCAL // SEG, dtype=I32) * SEG
    owner = jnp.sum(((tl + tn)[:, None, :] <= group_row[None, :, None]).astype(I32), axis=-1)
    is_owner = (owner[:, :, None] == jnp.arange(N_EXPERTS, dtype=I32)[None, None, :]).astype(I32)
    base = jnp.sum(is_owner * (pad_start[None, :] + tc - tl)[:, None, :], axis=-1)
    used = owner < N_EXPERTS
    pad_table = lambda a: jnp.pad(a, ((0, 0), (0, ROW_TABLE - a.shape[1]))).reshape(-1)
    dst_table = pad_table(jnp.where(used, base + group_row[None, :], n_blocks * bm + group_row[None, :]))
    src_table = pad_table(jnp.where(used, base + group_row[None, :], 0))

    xs = _dispatch(pad_start, counts, n_used, dst_table, h2, mt, (n_blocks + spare_blocks) * bm, bm)
    ys = _experts(block_e, n_used, xs, p, bm)
    y = _combine(src_table, mt, x1, ys, p)
    return y.reshape(batch, seq_len, D_MODEL)


def kernel(x_prompt, x_sample, norm1_g, w_in, conv_w, conv_b, dt_bias_f, dt_bias_b, a_log_f, a_log_b,
           d_skip, ssd_norm_g, gmlp_ln_g, gmlp_ln_b, w_spatial, b_spatial, w_out, norm2_g, router_w,
           router_b, w_gate_up, b_gate_up, w_down, b_down, final_g):
    p = _prepare_params(norm1_g, w_in, conv_w, conv_b, dt_bias_f, dt_bias_b, a_log_f, a_log_b, d_skip,
                        ssd_norm_g, gmlp_ln_g, gmlp_ln_b, w_spatial, b_spatial, w_out, norm2_g,
                        router_w, router_b, w_gate_up, b_gate_up, w_down, b_down, final_g)
    return (_encoder(x_prompt, p), _encoder(x_sample, p))
```

```python
import functools

import jax
import jax.numpy as jnp
from jax import lax
from jax.experimental import pallas as pl
from jax.experimental.pallas import tpu as pltpu

F32 = jnp.float32
BF16 = jnp.bfloat16
U32 = jnp.uint32
I32 = jnp.int32

D_MODEL = 1024
SSD_WIDTH = 1024
HEAD_DIM = 64
HEADS = 16
GROUPS = 2
HEADS_PER_GROUP = 8
STATE = 128
CHUNK = 128
CONV_WIDTH = 5
CONV_HALF = 2
CONV_CH = SSD_WIDTH + 2 * GROUPS * STATE
GMLP_WIDTH = 1024
GMLP_HEADS = 8
GMLP_HEAD_DIM = 128
N_EXPERTS = 32
TOP_K = 4
D_FF = 1024
SWIGLU_LIMIT = 7.0
SWIGLU_ALPHA = 1.702
RMS_EPS = 1e-5
LN_EPS = 1e-5

LANES = 128
HALO = 16
SSD_CHUNKS_PER_STEP = 4
TM_IN = 512
IN_COLS = 256
IN_ROWS = 512
TM_OUT = 512
BM_CHOICES = (1024, 512)
BM_MIN_BLOCKS = 6
SEG = 8
COPY_ROWS = (32, 16, 8)
SORT_BLK = 256
R_LOCAL = -(-(TM_OUT * TOP_K + N_EXPERTS * (SEG - 1)) // SORT_BLK) * SORT_BLK
ROW_TABLE = 1024
HALF = D_MODEL // 2
VMEM_LIMIT = 56 * 1024 * 1024
NEG_BIG = -1e30
LOG2_E = 1.4426950408889634


def _dot(a, b):
    return jnp.dot(a, b, preferred_element_type=F32)


def _gelu(x):
    return 0.5 * x * (1.0 + lax.erf(x * (2.0 ** -0.5)))


def _pack_bf16_pair(lo, hi):
    lo_bits = lax.bitcast_convert_type(lo.astype(BF16).astype(F32), U32)
    hi_bits = lax.bitcast_convert_type(hi.astype(BF16).astype(F32), U32)
    return (lo_bits >> 16) | hi_bits


def _unpack_bf16_pair(w):
    lo = lax.bitcast_convert_type(w << 16, F32)
    hi = lax.bitcast_convert_type(w & jnp.uint32(0xFFFF0000), F32)
    return lo, hi


def _const_spec(shape):
    nd = len(shape)
    return pl.BlockSpec(shape, lambda *_: (0,) * nd)


def _inproj_kernel(x_ref, xp_ref, xn_ref, g1_ref, wz_ref, wxbc_ref, wdt_ref, wu_ref, wv_ref,
                   cw_ref, cb_ref, dtb_ref, lng_ref, lnb_ref,
                   sz_ref, xbc_ref, dt_ref, u_ref, v_ref, buf0, buf1, buf2, gbuf, *, tiles_per_seq):
    i = pl.program_id(0)
    tm = x_ref.shape[0]
    g1 = g1_ref[...]
    row_blocks = [(r0, IN_ROWS) for r0 in range(0, tm, IN_ROWS)]

    def norm(xv):
        ms = jnp.mean(xv * xv, axis=-1, keepdims=True)
        return xv * lax.rsqrt(ms + RMS_EPS) * g1

    keep_prev = jnp.where(i % tiles_per_seq == 0, 0.0, 1.0)
    keep_next = jnp.where(i % tiles_per_seq == tiles_per_seq - 1, 0.0, 1.0)
    h = norm(x_ref[...]).astype(BF16)
    h_all = jnp.concatenate([(norm(xp_ref[...]) * keep_prev).astype(BF16), h,
                             (norm(xn_ref[...]) * keep_next).astype(BF16)], axis=0)

    def silu_to(out_ref):
        def epilogue(buf, c0, wd):
            for r0, nr in row_blocks:
                r = buf[r0:r0 + nr, 0:wd]
                out_ref[r0:r0 + nr, c0:c0 + wd] = (r * jax.nn.sigmoid(r)).astype(BF16)
        return epilogue

    def conv_silu(buf, c0, wd):
        for r0, nr in row_blocks:
            acc = jnp.broadcast_to(cb_ref[:, c0:c0 + wd], (nr, wd))
            for k in range(CONV_WIDTH):
                acc = acc + cw_ref[k:k + 1, c0:c0 + wd] * buf[pl.ds(r0 + HALO - CONV_HALF + k, nr), 0:wd]
            xbc_ref[r0:r0 + nr, c0:c0 + wd] = (acc * jax.nn.sigmoid(acc)).astype(BF16)

    def softplus_dt(buf, c0, wd):
        dt_ref[...] = jax.nn.softplus(buf[0:tm, 0:wd] + dtb_ref[...])

    def gelu_u(buf, c0, wd):
        for r0, nr in row_blocks:
            u_ref[r0:r0 + nr, c0:c0 + wd] = _gelu(buf[r0:r0 + nr, 0:wd]).astype(BF16)

    sums = [jnp.zeros((nr, LANES), F32) for _, nr in row_blocks]

    def gelu_v(buf, c0, wd):
        for b, (r0, nr) in enumerate(row_blocks):
            g = _gelu(buf[r0:r0 + nr, 0:wd])
            gbuf[r0:r0 + nr, c0:c0 + wd] = g
            for l0 in range(0, wd, LANES):
                sums[b] = sums[b] + g[:, l0:l0 + LANES]

    heavy = [(h_all, wxbc_ref, c0, IN_COLS, conv_silu) for c0 in range(0, CONV_CH, IN_COLS)]
    light = [(h, wz_ref, c0, IN_COLS, silu_to(sz_ref)) for c0 in range(0, SSD_WIDTH, IN_COLS)]
    light += [(h, wu_ref, c0, IN_COLS, gelu_u) for c0 in range(0, GMLP_WIDTH, IN_COLS)]
    light += [(h, wv_ref, c0, IN_COLS, gelu_v) for c0 in range(0, GMLP_WIDTH, IN_COLS)]
    light.append((h, wdt_ref, 0, LANES, softplus_dt))
    stages = []
    while heavy or light:
        if heavy:
            stages.append(heavy.pop(0))
        if light:
            stages.append(light.pop(0))
    bufs = (buf0, buf1, buf2)
    ahead = len(bufs) - 1

    def matmul(j):
        lhs, w_ref, c0, wd, _ = stages[j]
        bufs[j % len(bufs)][0:lhs.shape[0], 0:wd] = _dot(lhs, w_ref[:, c0:c0 + wd])

    for j in range(ahead):
        matmul(j)
    for j, (_, _, c0, wd, epilogue) in enumerate(stages):
        if j + ahead < len(stages):
            matmul(j + ahead)
        epilogue(bufs[j % len(bufs)], c0, wd)

    inv_n = 1.0 / GMLP_WIDTH
    for b, (r0, nr) in enumerate(row_blocks):
        mu = jnp.sum(sums[b], axis=-1, keepdims=True) * inv_n
        sq = jnp.zeros((nr, LANES), F32)
        for l0 in range(0, GMLP_WIDTH, LANES):
            d = gbuf[r0:r0 + nr, l0:l0 + LANES] - mu
            sq = sq + d * d
        scale = lax.rsqrt(jnp.sum(sq, axis=-1, keepdims=True) * inv_n + LN_EPS)
        for c0 in range(0, GMLP_WIDTH, IN_COLS):
            g = gbuf[r0:r0 + nr, c0:c0 + IN_COLS]
            v_ref[r0:r0 + nr, c0:c0 + IN_COLS] = (
                (g - mu) * scale * lng_ref[:, c0:c0 + IN_COLS] + lnb_ref[:, c0:c0 + IN_COLS]).astype(BF16)


def _inproj(x2d, seq_len, p):
    t = x2d.shape[0]
    tm = TM_IN
    tiles_per_seq = seq_len // tm
    nh = tm // HALO
    last_halo = t // HALO - 1
    kern = functools.partial(_inproj_kernel, tiles_per_seq=tiles_per_seq)
    row = lambda i: (i, 0)
    out_shape = (
        jax.ShapeDtypeStruct((t, SSD_WIDTH), BF16),
        jax.ShapeDtypeStruct((t, CONV_CH), BF16),
        jax.ShapeDtypeStruct((t, LANES), F32),
        jax.ShapeDtypeStruct((t, GMLP_WIDTH), BF16),
        jax.ShapeDtypeStruct((t, GMLP_WIDTH), BF16),
    )
    return pl.pallas_call(
        kern,
        out_shape=out_shape,
        grid=(t // tm,),
        in_specs=[
            pl.BlockSpec((tm, D_MODEL), row),
            pl.BlockSpec((HALO, D_MODEL), lambda i: (jnp.maximum(i * nh - 1, 0), 0)),
            pl.BlockSpec((HALO, D_MODEL), lambda i: (jnp.minimum((i + 1) * nh, last_halo), 0)),
            _const_spec((1, D_MODEL)),
            _const_spec((D_MODEL, SSD_WIDTH)),
            _const_spec((D_MODEL, CONV_CH)),
            _const_spec((D_MODEL, LANES)),
            _const_spec((D_MODEL, GMLP_WIDTH)),
            _const_spec((D_MODEL, GMLP_WIDTH)),
            _const_spec((8, CONV_CH)),
            _const_spec((1, CONV_CH)),
            _const_spec((1, LANES)),
            _const_spec((1, GMLP_WIDTH)),
            _const_spec((1, GMLP_WIDTH)),
        ],
        out_specs=(
            pl.BlockSpec((tm, SSD_WIDTH), row),
            pl.BlockSpec((tm, CONV_CH), row),
            pl.BlockSpec((tm, LANES), row),
            pl.BlockSpec((tm, GMLP_WIDTH), row),
            pl.BlockSpec((tm, GMLP_WIDTH), row),
        ),
        scratch_shapes=[pltpu.VMEM((tm + 2 * HALO, IN_COLS), F32)] * 3 + [pltpu.VMEM((tm, GMLP_WIDTH), F32)],
        compiler_params=pltpu.CompilerParams(
            dimension_semantics=("arbitrary",), vmem_limit_bytes=VMEM_LIMIT),
        name="inproj",
    )(x2d, x2d, x2d, p["g1"], p["w_z"], p["w_xbc"], p["w_dt"], p["w_u"], p["w_v"],
      p["conv_w"], p["conv_b"], p["dt_bias"], p["ln_g"], p["ln_b"])


def _split2(q):
    hi = q.astype(BF16)
    lo = (q - hi.astype(F32)).astype(BF16)
    return hi, lo


def _ssd_role(xbc_ref, dt_ref, arow, tri_ref, rexp_ref, dskip_ref, s_ref, y_ref, *, r0, backward):
    lane0 = HEADS if backward else 0
    rows = slice(r0, r0 + CHUNK)
    dt_all = dt_ref[rows, :]
    adt = dt_all * arow
    a1 = adt.astype(BF16)
    r1 = adt - a1.astype(F32)
    a2 = r1.astype(BF16)
    a3 = (r1 - a2.astype(F32)).astype(BF16)
    acs3 = _dot(tri_ref[...], jnp.concatenate([a1, a2, a3], axis=1))
    acs = acs3[:, 0:LANES] + acs3[:, LANES:2 * LANES] + acs3[:, 2 * LANES:]
    edge = 0 if backward else CHUNK - 1
    tot = acs[edge:edge + 1, :]
    decay = jnp.exp(tot - acs)
    eacs = jnp.exp(acs)
    acs2 = acs * LOG2_E
    acs2_t = acs2.T

    hi, lo = _split2(jnp.concatenate([dt_all, decay, eacs], axis=0))
    expanded = _dot(jnp.concatenate([hi, lo], axis=1), rexp_ref[...])
    dt_e = expanded[0:CHUNK]
    decay_e = expanded[CHUNK:2 * CHUNK]
    eacs_e = expanded[2 * CHUNK:]
    etot_e = eacs_e[edge:edge + 1, :]

    xf = xbc_ref[rows, 0:SSD_WIDTH].astype(F32)
    xdt = xf * dt_e
    xdd = (xdt * decay_e).astype(BF16)
    xdt_b = xdt.astype(BF16)

    li = lax.broadcasted_iota(I32, (CHUNK, CHUNK), 0)
    si = lax.broadcasted_iota(I32, (CHUNK, CHUNK), 1)
    keep = (si >= li) if backward else (si <= li)
    left = si < HEAD_DIM
    zero_b = jnp.zeros((CHUNK, LANES), BF16)

    for g in range(GROUPS):
        b_g = xbc_ref[rows, SSD_WIDTH + g * STATE:SSD_WIDTH + (g + 1) * STATE]
        c_g = xbc_ref[rows, SSD_WIDTH + (GROUPS + g) * STATE:SSD_WIDTH + (GROUPS + g + 1) * STATE]
        b_t = b_g.astype(F32).T.astype(BF16)
        cb = _dot(c_g, b_t)
        gc0 = g * HEADS_PER_GROUP * HEAD_DIM
        gw = HEADS_PER_GROUP * HEAD_DIM
        s_old = s_ref[:, gc0:gc0 + gw]
        y_off = _dot(c_g, s_old.astype(BF16))
        s_new = _dot(b_t, xdd[:, gc0:gc0 + gw])
        s_ref[:, gc0:gc0 + gw] = s_old * etot_e[:, gc0:gc0 + gw] + s_new
        for pp in range(HEADS_PER_GROUP // 2):
            c0 = gc0 + pp * LANES
            ms = []
            for q in range(2):
                hl = lane0 + g * HEADS_PER_GROUP + 2 * pp + q
                seg = acs2[:, hl:hl + 1] - acs2_t[hl:hl + 1, :]
                lmat = jnp.exp2(jnp.where(keep, seg, -jnp.inf))
                ms.append((cb * lmat).astype(BF16))
            lhs = jnp.concatenate(ms, axis=1)
            xp = xdt_b[:, c0:c0 + LANES]
            rhs = jnp.concatenate([jnp.where(left, xp, zero_b), jnp.where(left, zero_b, xp)], axis=0)
            y = _dot(lhs, rhs) + y_off[:, pp * LANES:(pp + 1) * LANES] * eacs_e[:, c0:c0 + LANES]
            if not backward:
                y = y + xf[:, c0:c0 + LANES] * dskip_ref[:, c0:c0 + LANES]
            y_ref[rows, c0:c0 + LANES] = y.astype(BF16)


def _ssd_kernel(xf_ref, dtf_ref, xb_ref, dtb_ref, alog_ref, trilo_ref, triup_ref, rf_ref, rb_ref,
                dskip_ref, yf_ref, yb_ref, sf_ref, sb_ref):
    @pl.when(pl.program_id(1) == 0)
    def _():
        sf_ref[...] = jnp.zeros_like(sf_ref)
        sb_ref[...] = jnp.zeros_like(sb_ref)

    arow = -jnp.exp(alog_ref[...])
    n_sub = xf_ref.shape[0] // CHUNK
    for j in range(n_sub):
        _ssd_role(xf_ref, dtf_ref, arow, trilo_ref, rf_ref, dskip_ref, sf_ref, yf_ref,
                  r0=j * CHUNK, backward=False)
        _ssd_role(xb_ref, dtb_ref, arow, triup_ref, rb_ref, dskip_ref, sb_ref, yb_ref,
                  r0=(n_sub - 1 - j) * CHUNK, backward=True)


def _ssd(xbc, dt, batch, seq_len, p):
    t = xbc.shape[0]
    rows = SSD_CHUNKS_PER_STEP * CHUNK
    nc = seq_len // rows
    fwd = lambda b, c: (b * nc + c, 0)
    bwd = lambda b, c: (b * nc + nc - 1 - c, 0)
    return pl.pallas_call(
        _ssd_kernel,
        out_shape=(jax.ShapeDtypeStruct((t, SSD_WIDTH), BF16), jax.ShapeDtypeStruct((t, SSD_WIDTH), BF16)),
        grid=(batch, nc),
        in_specs=[
            pl.BlockSpec((rows, CONV_CH), fwd),
            pl.BlockSpec((rows, LANES), fwd),
            pl.BlockSpec((rows, CONV_CH), bwd),
            pl.BlockSpec((rows, LANES), bwd),
            _const_spec((1, LANES)),
            _const_spec((CHUNK, CHUNK)),
            _const_spec((CHUNK, CHUNK)),
            _const_spec((2 * LANES, SSD_WIDTH)),
            _const_spec((2 * LANES, SSD_WIDTH)),
            _const_spec((1, SSD_WIDTH)),
        ],
        out_specs=(pl.BlockSpec((rows, SSD_WIDTH), fwd), pl.BlockSpec((rows, SSD_WIDTH), bwd)),
        scratch_shapes=[pltpu.VMEM((STATE, SSD_WIDTH), F32), pltpu.VMEM((STATE, SSD_WIDTH), F32)],
        compiler_params=pltpu.CompilerParams(
            dimension_semantics=("arbitrary", "arbitrary"), vmem_limit_bytes=VMEM_LIMIT),
        name="ssd",
    )(xbc, dt, xbc, dt, p["a_log"], p["tri_lo"], p["tri_up"], p["rexp_f"], p["rexp_b"], p["d_skip"])


def _outproj_kernel(yf_ref, yb_ref, sz_ref, u_ref, v_ref, x_ref, gs_ref, wsp_ref, bsp_ref, wo_ref,
                    g2_ref, wr_ref, rb_ref, tri_ref, trie_ref,
                    x1_ref, h2_ref, mt_ref, tm_ref,
                    gm_buf, carry, logit_t):
    i = pl.program_id(0)
    tm = x_ref.shape[0]

    @pl.when(i == 0)
    def _():
        carry[...] = jnp.zeros_like(carry)
        logit_t[...] = jnp.zeros_like(logit_t)

    _route_tile(logit_t[...], jnp.where(i > 0, 1.0, 0.0), tri_ref, trie_ref, carry, mt_ref, tm_ref)

    gated = (yf_ref[...].astype(F32) + yb_ref[...].astype(F32)) * sz_ref[...].astype(F32)
    ms = jnp.mean(gated * gated, axis=-1, keepdims=True)
    ssd = (gated * lax.rsqrt(ms + RMS_EPS) * gs_ref[...]).astype(BF16)

    for c in range(tm // CHUNK):
        r0 = c * CHUNK
        for hh in range(GMLP_HEADS):
            c0 = hh * GMLP_HEAD_DIM
            sv = _dot(wsp_ref[hh], v_ref[r0:r0 + CHUNK, c0:c0 + GMLP_HEAD_DIM]) + bsp_ref[hh]
            gm_buf[r0:r0 + CHUNK, c0:c0 + GMLP_HEAD_DIM] = (
                u_ref[r0:r0 + CHUNK, c0:c0 + GMLP_HEAD_DIM].astype(F32) * sv).astype(BF16)

    x1 = x_ref[...] + _dot(jnp.concatenate([ssd, gm_buf[...]], axis=1), wo_ref[...])
    x1_ref[...] = x1

    ms2 = jnp.mean(x1 * x1, axis=-1, keepdims=True)
    h2 = (x1 * lax.rsqrt(ms2 + RMS_EPS) * g2_ref[...]).astype(BF16)
    h2_ref[...] = h2

    logits = _dot(h2, wr_ref[...]) + rb_ref[...]
    logit_t[...] = logits.T[0:N_EXPERTS, :]


def _route_tile(work, live, tri_ref, trie_ref, carry, mt_ref, tm_ref):
    tm = work.shape[1]
    erow = lax.broadcasted_iota(I32, (N_EXPERTS, tm), 0).astype(F32)
    sel_all = jnp.zeros((N_EXPERTS, tm), F32)
    sels, vals = [], []
    for _ in range(TOP_K):
        m = jnp.max(work, axis=0, keepdims=True)
        idx = jnp.min(jnp.where(work == m, erow, float(N_EXPERTS)), axis=0, keepdims=True)
        sel = erow == idx
        sel_all = jnp.where(sel, 1.0, sel_all)
        work = jnp.where(sel, -jnp.inf, work)
        sels.append(sel)
        vals.append(m)
    exps = [jnp.exp(v - vals[0]) for v in vals]
    denom = exps[0] + exps[1] + exps[2] + exps[3]

    lrank = _dot(sel_all.astype(BF16), tri_ref[...])
    n_e = jnp.sum(sel_all, axis=1, keepdims=True)
    n_al = jnp.broadcast_to(jnp.floor((n_e + (SEG - 1)) * (1.0 / SEG)) * (SEG * live), (N_EXPERTS, LANES))
    lstart = _dot(trie_ref[...], n_al.astype(BF16))
    old_cnt = carry[...]
    new_cnt = old_cnt + n_al
    carry[...] = new_cnt
    tile_row = jnp.concatenate([n_al, old_cnt, lstart, new_cnt], axis=0).T[0:1, :]
    tm_ref[0] = tile_row.astype(I32)

    local_row = lstart[:, 0:1] + lrank
    out_row = lax.broadcasted_iota(I32, (2 * TOP_K, tm), 0)
    meta = jnp.zeros((2 * TOP_K, tm), F32)
    for k in range(TOP_K):
        dk = jnp.sum(jnp.where(sels[k], local_row, 0.0), axis=0, keepdims=True)
        meta = jnp.where(out_row == k, dk, meta)
        meta = jnp.where(out_row == TOP_K + k, exps[k] / denom, meta)
    mt_ref[...] = meta


def _outproj(yf, yb, sz, u, v, x2d, p):
    t = x2d.shape[0]
    tm = TM_OUT
    row = lambda i: (i, 0)
    nt = t // tm
    out_shape = (
        jax.ShapeDtypeStruct((t, D_MODEL), F32),
        jax.ShapeDtypeStruct((t, D_MODEL), BF16),
        jax.ShapeDtypeStruct((nt * 2 * TOP_K, tm), F32),
        jax.ShapeDtypeStruct((nt, 1, LANES), I32),
    )
    row = lambda i: (jnp.minimum(i, nt - 1), 0)
    routed = lambda i: jnp.maximum(i - 1, 0)
    bf_row = pl.BlockSpec((tm, D_MODEL), row)
    return pl.pallas_call(
        _outproj_kernel,
        out_shape=out_shape,
        grid=(nt + 1,),
        in_specs=[
            bf_row, bf_row, bf_row, bf_row, bf_row, bf_row,
            _const_spec((1, SSD_WIDTH)),
            _const_spec((GMLP_HEADS, CHUNK, CHUNK)),
            _const_spec((GMLP_HEADS, CHUNK, GMLP_HEAD_DIM)),
            _const_spec((SSD_WIDTH + GMLP_WIDTH, D_MODEL)),
            _const_spec((1, D_MODEL)),
            _const_spec((D_MODEL, LANES)),
            _const_spec((1, LANES)),
            _const_spec((tm, tm)),
            _const_spec((N_EXPERTS, N_EXPERTS)),
        ],
        out_specs=(
            pl.BlockSpec((tm, D_MODEL), row),
            pl.BlockSpec((tm, D_MODEL), row),
            pl.BlockSpec((2 * TOP_K, tm), lambda i: (routed(i), 0)),
            pl.BlockSpec((1, 1, LANES), lambda i: (routed(i), 0, 0)),
        ),
        scratch_shapes=[pltpu.VMEM((tm, GMLP_WIDTH), BF16), pltpu.VMEM((N_EXPERTS, LANES), F32),
                        pltpu.VMEM((N_EXPERTS, tm), F32)],
        compiler_params=pltpu.CompilerParams(
            dimension_semantics=("arbitrary",), vmem_limit_bytes=VMEM_LIMIT),
        name="outproj",
    )(yf, yb, sz, u, v, x2d, p["gs"], p["w_sp"], p["b_sp"], p["w_out"], p["g2"], p["w_r"], p["b_r"],
      p["tok_before"], p["expert_before"])


def _copy_segment(make_copy, n, src_row, dst_row):
    big, mid, small = COPY_ROWS

    def body(c, carry_):
        make_copy(pl.multiple_of(src_row + c * big, SEG), pl.multiple_of(dst_row + c * big, SEG), big).start()
        return carry_
    nbig = n // big
    lax.fori_loop(0, nbig, body, 0)
    done = nbig * big

    @pl.when((n & mid) != 0)
    def _():
        make_copy(pl.multiple_of(src_row + done, SEG), pl.multiple_of(dst_row + done, SEG), mid).start()

    done2 = done + (n & mid)

    @pl.when((n & small) != 0)
    def _():
        make_copy(pl.multiple_of(src_row + done2, SEG), pl.multiple_of(dst_row + done2, SEG), small).start()


def _wait_rows(make_copy, nrows, limit=R_LOCAL):
    w = 1 << (limit.bit_length() - 1)
    while w >= SEG:
        @pl.when((nrows & w) != 0)
        def _(w=w):
            make_copy(0, 0, w).wait()
        w //= 2


def _onehot_rows(mt_ref, blk, values):
    tm = mt_ref.shape[1]
    riota = lax.broadcasted_iota(I32, (SORT_BLK, tm), 0).astype(F32).astype(BF16)
    out = jnp.zeros((SORT_BLK, tm), BF16)
    for k in reversed(range(TOP_K)):
        off = (mt_ref[k:k + 1, :] - float(blk * SORT_BLK)).astype(BF16)
        out = jnp.where(riota == off, values[k], out)
    return out


def _dispatch_kernel(ps_ref, cn_ref, nu_ref, dst_ref, h2_ref, mt_ref, xs_ref, sbuf, zbuf, sem):
    i = pl.program_id(0)
    nt = pl.num_programs(0)
    tm = h2_ref.shape[0]
    groups_per_blk = SORT_BLK // SEG

    def wait_tile():
        pltpu.make_async_copy(sbuf, xs_ref.at[pl.ds(0, R_LOCAL)], sem).wait()

    h2 = h2_ref[...]
    ones = [jnp.ones((1, tm), BF16)] * TOP_K
    for blk in range(R_LOCAL // SORT_BLK):
        s = _dot(_onehot_rows(mt_ref, blk, ones), h2)
        lo_bits = lax.bitcast_convert_type(s[:, 0:HALF], U32)
        hi_bits = lax.bitcast_convert_type(s[:, HALF:], U32)
        packed = (lo_bits >> 16) | hi_bits
        if blk == 0:
            @pl.when(i > 0)
            def _():
                wait_tile()
        sbuf[blk * SORT_BLK:(blk + 1) * SORT_BLK, :] = packed
        for g in range(blk * groups_per_blk, (blk + 1) * groups_per_blk):
            pltpu.make_async_copy(sbuf.at[pl.ds(g * SEG, SEG)],
                                  xs_ref.at[pl.ds(pl.multiple_of(dst_ref[g], SEG), SEG)], sem).start()

    @pl.when(i == nt - 1)
    def _():
        wait_tile()
        zbuf[...] = jnp.zeros_like(zbuf)

        def zero_copy(src_row, dst_row, rows):
            return pltpu.make_async_copy(zbuf.at[pl.ds(src_row, rows)], xs_ref.at[pl.ds(dst_row, rows)], sem)

        bm = zbuf.shape[0]

        def tail_rows(e):
            return (cn_ref[e] + (bm - 1)) // bm * bm - cn_ref[e]

        def start_tail(e, c):
            _copy_segment(zero_copy, tail_rows(e), 0, ps_ref[e] + cn_ref[e])
            return c
        lax.fori_loop(0, N_EXPERTS, start_tail, 0)

        def start_block(b, c):
            zero_copy(0, pl.multiple_of(b * bm, bm), bm).start()
            return c
        n_rows_total = xs_ref.shape[0]
        lax.fori_loop(nu_ref[0], n_rows_total // bm, start_block, 0)

        def wait_tail(e, c):
            _wait_rows(zero_copy, tail_rows(e), limit=bm - SEG)
            return c
        lax.fori_loop(0, N_EXPERTS, wait_tail, 0)

        def wait_block(b, c):
            zero_copy(0, 0, bm).wait()
            return c
        lax.fori_loop(nu_ref[0], n_rows_total // bm, wait_block, 0)


def _dispatch(pad_start, counts, n_used, dst_table, h2, mt, rows, bm):
    t = h2.shape[0]
    tm = TM_OUT
    grid_spec = pltpu.PrefetchScalarGridSpec(
        num_scalar_prefetch=3,
        grid=(t // tm,),
        in_specs=[
            pl.BlockSpec((ROW_TABLE,), lambda i, *_: (i,), memory_space=pltpu.SMEM),
            pl.BlockSpec((tm, D_MODEL), lambda i, *_: (i, 0)),
            pl.BlockSpec((2 * TOP_K, tm), lambda i, *_: (i, 0)),
        ],
        out_specs=pl.BlockSpec(memory_space=pl.ANY),
        scratch_shapes=[pltpu.VMEM((R_LOCAL, HALF), U32), pltpu.VMEM((bm, HALF), U32),
                        pltpu.SemaphoreType.DMA(())],
    )
    return pl.pallas_call(
        _dispatch_kernel,
        out_shape=jax.ShapeDtypeStruct((rows, HALF), U32),
        grid_spec=grid_spec,
        compiler_params=pltpu.CompilerParams(
            dimension_semantics=("arbitrary",), vmem_limit_bytes=VMEM_LIMIT, has_side_effects=True),
        name="dispatch",
    )(pad_start, counts, n_used, dst_table, h2, mt)


def _experts_kernel(be_ref, nu_ref, xs_ref, wgu_ref, bgu_ref, wd_ref, bd_ref, ys_ref, wgu_bf, wd_bf):
    i = pl.program_id(0)
    used = i < nu_ref[0]

    @pl.when(used & ((i == 0) | (be_ref[i] != be_ref[jnp.maximum(i - 1, 0)])))
    def _():
        wgu_bf[...] = wgu_ref[0].astype(BF16)
        wd_bf[...] = wd_ref[0].astype(BF16)

    @pl.when(used)
    def _():
        lo, hi = _unpack_bf16_pair(xs_ref[...])
        xb = jnp.concatenate([lo.astype(BF16), hi.astype(BF16)], axis=1)
        gu = _dot(xb, wgu_bf[...]) + bgu_ref[0]
        gate = jnp.minimum(gu[:, 0:D_FF], SWIGLU_LIMIT)
        up = jnp.clip(gu[:, D_FF:], -SWIGLU_LIMIT, SWIGLU_LIMIT)
        act = (up + 1.0) * (gate * jax.nn.sigmoid(gate * SWIGLU_ALPHA))
        y = _dot(act.astype(BF16), wd_bf[...]) + bd_ref[0]
        ys_ref[...] = _pack_bf16_pair(y[:, 0:HALF], y[:, HALF:])

    @pl.when(i >= nu_ref[0])
    def _():
        ys_ref[...] = jnp.zeros_like(ys_ref)


def _experts(block_e, n_used, xs, p, bm):
    rows = xs.shape[0]
    nb = rows // bm
    grid_spec = pltpu.PrefetchScalarGridSpec(
        num_scalar_prefetch=2,
        grid=(nb,),
        in_specs=[
            pl.BlockSpec((bm, HALF), lambda i, be, *_: (i, 0)),
            pl.BlockSpec((1, D_MODEL, 2 * D_FF), lambda i, be, *_: (be[i], 0, 0)),
            pl.BlockSpec((1, 1, 2 * D_FF), lambda i, be, *_: (be[i], 0, 0)),
            pl.BlockSpec((1, D_FF, D_MODEL), lambda i, be, *_: (be[i], 0, 0)),
            pl.BlockSpec((1, 1, D_MODEL), lambda i, be, *_: (be[i], 0, 0)),
        ],
        out_specs=pl.BlockSpec((bm, HALF), lambda i, be, *_: (i, 0)),
        scratch_shapes=[pltpu.VMEM((D_MODEL, 2 * D_FF), BF16), pltpu.VMEM((D_FF, D_MODEL), BF16)],
    )
    return pl.pallas_call(
        _experts_kernel,
        out_shape=jax.ShapeDtypeStruct((rows, HALF), U32),
        grid_spec=grid_spec,
        compiler_params=pltpu.CompilerParams(
            dimension_semantics=("arbitrary",), vmem_limit_bytes=VMEM_LIMIT),
        name="experts",
    )(block_e, n_used, xs, p["w_gu"], p["b_gu"], p["w_d"], p["b_d"])


def _combine_kernel(cur_ref, nxt_ref, mt_ref, x1_ref, gf_ref, ys_ref, o_ref,
                    ybuf, gt_buf, ylo_buf, yhi_buf, sems):
    i = pl.program_id(0)
    nt = pl.num_programs(0)

    def issue(table_ref, slot):
        for g in range(R_LOCAL // SEG):
            pltpu.make_async_copy(ys_ref.at[pl.ds(pl.multiple_of(table_ref[g], SEG), SEG)],
                                  ybuf.at[slot, pl.ds(g * SEG, SEG)], sems.at[slot]).start()

    def wait_slot(slot):
        pltpu.make_async_copy(ys_ref.at[pl.ds(0, R_LOCAL)], ybuf.at[slot], sems.at[slot]).wait()

    @pl.when(i == 0)
    def _():
        issue(cur_ref, 0)

    slot = i % 2
    issue(nxt_ref, 1 - slot)

    gates = [mt_ref[TOP_K + k:TOP_K + k + 1, :].astype(BF16) for k in range(TOP_K)]
    for blk in range(R_LOCAL // SORT_BLK):
        rows = slice(blk * SORT_BLK, (blk + 1) * SORT_BLK)
        gt_buf[rows, :] = _onehot_rows(mt_ref, blk, gates)
    wait_slot(slot)
    for blk in range(R_LOCAL // SORT_BLK):
        rows = slice(blk * SORT_BLK, (blk + 1) * SORT_BLK)
        lo, hi = _unpack_bf16_pair(ybuf[slot, rows, :])
        ylo_buf[rows, :] = lo.astype(BF16)
        yhi_buf[rows, :] = hi.astype(BF16)
    contract_rows = (((0,), (0,)), ((), ()))
    g_t = gt_buf[...]
    x1 = x1_ref[...]
    acc = jnp.concatenate(
        [x1[:, 0:HALF] + lax.dot_general(g_t, ylo_buf[...], contract_rows, preferred_element_type=F32),
         x1[:, HALF:] + lax.dot_general(g_t, yhi_buf[...], contract_rows, preferred_element_type=F32)], axis=1)
    ms = jnp.mean(acc * acc, axis=-1, keepdims=True)
    o_ref[...] = acc * lax.rsqrt(ms + RMS_EPS) * gf_ref[...]

    @pl.when(i == nt - 1)
    def _():
        wait_slot(1 - slot)


def _combine(src_table, mt, x1, ys, p):
    t = x1.shape[0]
    tm = TM_OUT
    nt = t // tm
    grid_spec = pltpu.PrefetchScalarGridSpec(
        num_scalar_prefetch=0,
        grid=(nt,),
        in_specs=[
            pl.BlockSpec((ROW_TABLE,), lambda i: (i,), memory_space=pltpu.SMEM),
            pl.BlockSpec((ROW_TABLE,), lambda i: (jnp.minimum(i + 1, nt - 1),), memory_space=pltpu.SMEM),
            pl.BlockSpec((2 * TOP_K, tm), lambda i: (i, 0)),
            pl.BlockSpec((tm, D_MODEL), lambda i: (i, 0)),
            pl.BlockSpec((1, D_MODEL), lambda i: (0, 0)),
            pl.BlockSpec(memory_space=pl.ANY),
        ],
        out_specs=pl.BlockSpec((tm, D_MODEL), lambda i: (i, 0)),
        scratch_shapes=[pltpu.VMEM((2, R_LOCAL, HALF), U32), pltpu.VMEM((R_LOCAL, tm), BF16),
                        pltpu.VMEM((R_LOCAL, HALF), BF16), pltpu.VMEM((R_LOCAL, HALF), BF16),
                        pltpu.SemaphoreType.DMA((2,))],
    )
    return pl.pallas_call(
        _combine_kernel,
        out_shape=jax.ShapeDtypeStruct((t, D_MODEL), F32),
        grid_spec=grid_spec,
        compiler_params=pltpu.CompilerParams(
            dimension_semantics=("arbitrary",), vmem_limit_bytes=VMEM_LIMIT),
        name="combine",
    )(src_table, src_table, mt, x1, p["gf"], ys)


def _prepare_params(norm1_g, w_in, conv_w, conv_b, dt_bias_f, dt_bias_b, a_log_f, a_log_b, d_skip,
                    ssd_norm_g, gmlp_ln_g, gmlp_ln_b, w_spatial, b_spatial, w_out, norm2_g,
                    router_w, router_b, w_gate_up, b_gate_up, w_down, b_down, final_g):
    z_end = SSD_WIDTH
    xbc_end = z_end + CONV_CH
    dt_end = xbc_end + 2 * HEADS
    u_end = dt_end + GMLP_WIDTH
    w = w_in[0]
    pad_lanes = lambda a, fill=0.0: jnp.pad(a, ((0, 0), (0, LANES - a.shape[1])), constant_values=fill)
    head_lane = jnp.arange(LANES)[:, None]
    chan_head = (jnp.arange(SSD_WIDTH) // HEAD_DIM)[None, :]
    ar = jnp.arange(CHUNK)
    art = jnp.arange(TM_OUT)
    are = jnp.arange(N_EXPERTS)
    return {
        "g1": norm1_g[0][None, :],
        "w_z": w[:, :z_end].astype(BF16),
        "w_xbc": w[:, z_end:xbc_end].astype(BF16),
        "w_dt": pad_lanes(w[:, xbc_end:dt_end]).astype(BF16),
        "w_u": w[:, dt_end:u_end].astype(BF16),
        "w_v": w[:, u_end:].astype(BF16),
        "conv_w": jnp.pad(conv_w[0], ((0, 8 - CONV_WIDTH), (0, 0))),
        "conv_b": conv_b[0][None, :],
        "dt_bias": pad_lanes(jnp.concatenate([dt_bias_f[0], dt_bias_b[0]])[None, :]),
        "ln_g": gmlp_ln_g[0][None, :],
        "ln_b": gmlp_ln_b[0][None, :],
        "a_log": pad_lanes(jnp.concatenate([a_log_f[0], a_log_b[0]])[None, :]),
        "tri_lo": (ar[:, None] >= ar[None, :]).astype(BF16),
        "tri_up": (ar[:, None] <= ar[None, :]).astype(BF16),
        "rexp_f": jnp.tile((head_lane == chan_head).astype(BF16), (2, 1)),
        "rexp_b": jnp.tile((head_lane == chan_head + HEADS).astype(BF16), (2, 1)),
        "d_skip": jnp.repeat(d_skip[0], HEAD_DIM)[None, :],
        "gs": ssd_norm_g[0][None, :],
        "w_sp": w_spatial[0].astype(BF16),
        "b_sp": jnp.broadcast_to(b_spatial[0][:, :, None], (GMLP_HEADS, CHUNK, GMLP_HEAD_DIM)),
        "w_out": w_out[0].astype(BF16),
        "g2": norm2_g[0][None, :],
        "w_r": pad_lanes(router_w[0]).astype(BF16),
        "b_r": pad_lanes(router_b[0][None, :], NEG_BIG),
        "tok_before": (art[:, None] < art[None, :]).astype(BF16),
        "expert_before": (are[None, :] < are[:, None]).astype(BF16),
        "w_gu": w_gate_up[0],
        "b_gu": b_gate_up[0][:, None, :],
        "w_d": w_down[0],
        "b_d": b_down[0][:, None, :],
        "gf": final_g[None, :],
    }


def _encoder(x, p):
    batch, seq_len, _ = x.shape
    t = batch * seq_len
    x2d = x.reshape(t, D_MODEL)
    sz, xbc, dt, u, v = _inproj(x2d, seq_len, p)
    yf, yb = _ssd(xbc, dt, batch, seq_len, p)
    x1, h2, mt, tile_rows = _outproj(yf, yb, sz, u, v, x2d, p)
    tile_rows = tile_rows[:, 0, :]
    tn, tc, tl = (tile_rows[:, k * N_EXPERTS:(k + 1) * N_EXPERTS] for k in range(3))

    counts = tile_rows[-1, 3 * N_EXPERTS:]
    rows_per_expert = t * TOP_K // N_EXPERTS
    bm = next((b for b in BM_CHOICES if rows_per_expert >= BM_MIN_BLOCKS * b), BM_CHOICES[-1])
    spare_blocks = -(-R_LOCAL // bm)
    padded = (counts + bm - 1) // bm * bm
    pad_end = jnp.cumsum(padded)
    pad_start = (pad_end - padded).astype(I32)
    n_tiles = t // TM_OUT
    n_blocks = -(-(t * TOP_K + n_tiles * N_EXPERTS * (SEG - 1) + N_EXPERTS * (bm - 1)) // bm)
    n_used = (pad_end[-1] // bm).astype(I32)
    blk_row = jnp.minimum(jnp.arange(n_blocks + spare_blocks, dtype=I32), n_used - 1) * bm
    onehot_e = (pad_end[None, :] <= blk_row[:, None]).astype(I32)
    block_e = jnp.minimum(jnp.sum(onehot_e, axis=1), N_EXPERTS - 1).astype(I32)
    n_used = n_used.reshape(1)

    group_row = jnp.arange(R_LOCAL // SEG, dtype=I32) * SEG
    owner = jnp.sum(((tl + tn)[:, None, :] <= group_row[None, :, None]).astype(I32), axis=-1)
    is_owner = (owner[:, :, None] == jnp.arange(N_EXPERTS, dtype=I32)[None, None, :]).astype(I32)
    base = jnp.sum(is_owner * (pad_start[None, :] + tc - tl)[:, None, :], axis=-1)
    used = owner < N_EXPERTS
    pad_table = lambda a: jnp.pad(a, ((0, 0), (0, ROW_TABLE - a.shape[1]))).reshape(-1)
    dst_table = pad_table(jnp.where(used, base + group_row[None, :], n_blocks * bm + group_row[None, :]))
    src_table = pad_table(jnp.where(used, base + group_row[None, :], 0))

    xs = _dispatch(pad_start, counts, n_used, dst_table, h2, mt, (n_blocks + spare_blocks) * bm, bm)
    ys = _experts(block_e, n_used, xs, p, bm)
    y = _combine(src_table, mt, x1, ys, p)
    return y.reshape(batch, seq_len, D_MODEL)


def kernel(x_prompt, x_sample, norm1_g, w_in, conv_w, conv_b, dt_bias_f, dt_bias_b, a_log_f, a_log_b,
           d_skip, ssd_norm_g, gmlp_ln_g, gmlp_ln_b, w_spatial, b_spatial, w_out, norm2_g, router_w,
           router_b, w_gate_up, b_gate_up, w_down, b_down, final_g):
    p = _prepare_params(norm1_g, w_in, conv_w, conv_b, dt_bias_f, dt_bias_b, a_log_f, a_log_b, d_skip,
                        ssd_norm_g, gmlp_ln_g, gmlp_ln_b, w_spatial, b_spatial, w_out, norm2_g,
                        router_w, router_b, w_gate_up, b_gate_up, w_down, b_down, final_g)
    return (_encoder(x_prompt, p), _encoder(x_sample, p))
```

```python
import functools

import jax
import jax.numpy as jnp
from jax import lax
from jax.experimental import pallas as pl
from jax.experimental.pallas import tpu as pltpu

F32 = jnp.float32
BF16 = jnp.bfloat16
U32 = jnp.uint32
I32 = jnp.int32

D_MODEL = 1024
SSD_WIDTH = 1024
HEAD_DIM = 64
HEADS = 16
GROUPS = 2
HEADS_PER_GROUP = 8
STATE = 128
CHUNK = 128
CONV_WIDTH = 5
CONV_HALF = 2
CONV_CH = SSD_WIDTH + 2 * GROUPS * STATE
GMLP_WIDTH = 1024
GMLP_HEADS = 8
GMLP_HEAD_DIM = 128
N_EXPERTS = 32
TOP_K = 4
D_FF = 1024
SWIGLU_LIMIT = 7.0
SWIGLU_ALPHA = 1.702
RMS_EPS = 1e-5
LN_EPS = 1e-5

LANES = 128
HALO = 16
SSD_CHUNKS_PER_STEP = 8
TM_IN = 512
IN_COLS = 256
IN_ROWS = 512
TM_OUT = 512
BM_CHOICES = (1024, 512)
BM_MIN_BLOCKS = 6
SEG = 8
COPY_ROWS = (32, 16, 8)
SORT_BLK = 256
R_LOCAL = -(-(TM_OUT * TOP_K + N_EXPERTS * (SEG - 1)) // SORT_BLK) * SORT_BLK
ROW_TABLE = 1024
HALF = D_MODEL // 2
VMEM_LIMIT = 56 * 1024 * 1024
NEG_BIG = -1e30
LOG2_E = 1.4426950408889634


def _dot(a, b):
    return jnp.dot(a, b, preferred_element_type=F32)


def _gelu(x):
    return 0.5 * x * (1.0 + lax.erf(x * (2.0 ** -0.5)))


def _pack_bf16_pair(lo, hi):
    lo_bits = lax.bitcast_convert_type(lo.astype(BF16).astype(F32), U32)
    hi_bits = lax.bitcast_convert_type(hi.astype(BF16).astype(F32), U32)
    return (lo_bits >> 16) | hi_bits


def _unpack_bf16_pair(w):
    lo = lax.bitcast_convert_type(w << 16, F32)
    hi = lax.bitcast_convert_type(w & jnp.uint32(0xFFFF0000), F32)
    return lo, hi


def _const_spec(shape):
    nd = len(shape)
    return pl.BlockSpec(shape, lambda *_: (0,) * nd)


def _inproj_kernel(x_ref, xp_ref, xn_ref, g1_ref, wz_ref, wxbc_ref, wdt_ref, wu_ref, wv_ref,
                   cw_ref, cb_ref, dtb_ref, lng_ref, lnb_ref,
                   sz_ref, xbc_ref, dt_ref, u_ref, v_ref, buf0, buf1, buf2, gbuf, *, tiles_per_seq):
    i = pl.program_id(0)
    tm = x_ref.shape[0]
    g1 = g1_ref[...]
    row_blocks = [(r0, IN_ROWS) for r0 in range(0, tm, IN_ROWS)]

    def norm(xv):
        ms = jnp.mean(xv * xv, axis=-1, keepdims=True)
        return xv * lax.rsqrt(ms + RMS_EPS) * g1

    keep_prev = jnp.where(i % tiles_per_seq == 0, 0.0, 1.0)
    keep_next = jnp.where(i % tiles_per_seq == tiles_per_seq - 1, 0.0, 1.0)
    h = norm(x_ref[...]).astype(BF16)
    h_all = jnp.concatenate([(norm(xp_ref[...]) * keep_prev).astype(BF16), h,
                             (norm(xn_ref[...]) * keep_next).astype(BF16)], axis=0)

    def silu_to(out_ref):
        def epilogue(buf, c0, wd):
            for r0, nr in row_blocks:
                r = buf[r0:r0 + nr, 0:wd]
                out_ref[r0:r0 + nr, c0:c0 + wd] = (r * jax.nn.sigmoid(r)).astype(BF16)
        return epilogue

    def conv_silu(buf, c0, wd):
        for r0, nr in row_blocks:
            acc = jnp.broadcast_to(cb_ref[:, c0:c0 + wd], (nr, wd))
            for k in range(CONV_WIDTH):
                acc = acc + cw_ref[k:k + 1, c0:c0 + wd] * buf[pl.ds(r0 + HALO - CONV_HALF + k, nr), 0:wd]
            xbc_ref[r0:r0 + nr, c0:c0 + wd] = (acc * jax.nn.sigmoid(acc)).astype(BF16)

    def softplus_dt(buf, c0, wd):
        dt_ref[...] = jax.nn.softplus(buf[0:tm, 0:wd] + dtb_ref[...])

    def gelu_u(buf, c0, wd):
        for r0, nr in row_blocks:
            u_ref[r0:r0 + nr, c0:c0 + wd] = _gelu(buf[r0:r0 + nr, 0:wd]).astype(BF16)

    sums = [jnp.zeros((nr, LANES), F32) for _, nr in row_blocks]

    def gelu_v(buf, c0, wd):
        for b, (r0, nr) in enumerate(row_blocks):
            g = _gelu(buf[r0:r0 + nr, 0:wd])
            gbuf[r0:r0 + nr, c0:c0 + wd] = g
            for l0 in range(0, wd, LANES):
                sums[b] = sums[b] + g[:, l0:l0 + LANES]

    heavy = [(h_all, wxbc_ref, c0, IN_COLS, conv_silu) for c0 in range(0, CONV_CH, IN_COLS)]
    light = [(h, wz_ref, c0, IN_COLS, silu_to(sz_ref)) for c0 in range(0, SSD_WIDTH, IN_COLS)]
    light += [(h, wu_ref, c0, IN_COLS, gelu_u) for c0 in range(0, GMLP_WIDTH, IN_COLS)]
    light += [(h, wv_ref, c0, IN_COLS, gelu_v) for c0 in range(0, GMLP_WIDTH, IN_COLS)]
    light.append((h, wdt_ref, 0, LANES, softplus_dt))
    stages = []
    while heavy or light:
        if heavy:
            stages.append(heavy.pop(0))
        if light:
            stages.append(light.pop(0))
    bufs = (buf0, buf1, buf2)
    ahead = len(bufs) - 1

    def matmul(j):
        lhs, w_ref, c0, wd, _ = stages[j]
        bufs[j % len(bufs)][0:lhs.shape[0], 0:wd] = _dot(lhs, w_ref[:, c0:c0 + wd])

    for j in range(ahead):
        matmul(j)
    for j, (_, _, c0, wd, epilogue) in enumerate(stages):
        if j + ahead < len(stages):
            matmul(j + ahead)
        epilogue(bufs[j % len(bufs)], c0, wd)

    inv_n = 1.0 / GMLP_WIDTH
    for b, (r0, nr) in enumerate(row_blocks):
        mu = jnp.sum(sums[b], axis=-1, keepdims=True) * inv_n
        sq = jnp.zeros((nr, LANES), F32)
        for l0 in range(0, GMLP_WIDTH, LANES):
            d = gbuf[r0:r0 + nr, l0:l0 + LANES] - mu
            sq = sq + d * d
        scale = lax.rsqrt(jnp.sum(sq, axis=-1, keepdims=True) * inv_n + LN_EPS)
        for c0 in range(0, GMLP_WIDTH, IN_COLS):
            g = gbuf[r0:r0 + nr, c0:c0 + IN_COLS]
            v_ref[r0:r0 + nr, c0:c0 + IN_COLS] = (
                (g - mu) * scale * lng_ref[:, c0:c0 + IN_COLS] + lnb_ref[:, c0:c0 + IN_COLS]).astype(BF16)


def _inproj(x2d, seq_len, p):
    t = x2d.shape[0]
    tm = TM_IN
    tiles_per_seq = seq_len // tm
    nh = tm // HALO
    last_halo = t // HALO - 1
    kern = functools.partial(_inproj_kernel, tiles_per_seq=tiles_per_seq)
    row = lambda i: (i, 0)
    out_shape = (
        jax.ShapeDtypeStruct((t, SSD_WIDTH), BF16),
        jax.ShapeDtypeStruct((t, CONV_CH), BF16),
        jax.ShapeDtypeStruct((t, LANES), F32),
        jax.ShapeDtypeStruct((t, GMLP_WIDTH), BF16),
        jax.ShapeDtypeStruct((t, GMLP_WIDTH), BF16),
    )
    return pl.pallas_call(
        kern,
        out_shape=out_shape,
        grid=(t // tm,),
        in_specs=[
            pl.BlockSpec((tm, D_MODEL), row),
            pl.BlockSpec((HALO, D_MODEL), lambda i: (jnp.maximum(i * nh - 1, 0), 0)),
            pl.BlockSpec((HALO, D_MODEL), lambda i: (jnp.minimum((i + 1) * nh, last_halo), 0)),
            _const_spec((1, D_MODEL)),
            _const_spec((D_MODEL, SSD_WIDTH)),
            _const_spec((D_MODEL, CONV_CH)),
            _const_spec((D_MODEL, LANES)),
            _const_spec((D_MODEL, GMLP_WIDTH)),
            _const_spec((D_MODEL, GMLP_WIDTH)),
            _const_spec((8, CONV_CH)),
            _const_spec((1, CONV_CH)),
            _const_spec((1, LANES)),
            _const_spec((1, GMLP_WIDTH)),
            _const_spec((1, GMLP_WIDTH)),
        ],
        out_specs=(
            pl.BlockSpec((tm, SSD_WIDTH), row),
            pl.BlockSpec((tm, CONV_CH), row),
            pl.BlockSpec((tm, LANES), row),
            pl.BlockSpec((tm, GMLP_WIDTH), row),
            pl.BlockSpec((tm, GMLP_WIDTH), row),
        ),
        scratch_shapes=[pltpu.VMEM((tm + 2 * HALO, IN_COLS), F32)] * 3 + [pltpu.VMEM((tm, GMLP_WIDTH), F32)],
        compiler_params=pltpu.CompilerParams(
            dimension_semantics=("arbitrary",), vmem_limit_bytes=VMEM_LIMIT),
        name="inproj",
    )(x2d, x2d, x2d, p["g1"], p["w_z"], p["w_xbc"], p["w_dt"], p["w_u"], p["w_v"],
      p["conv_w"], p["conv_b"], p["dt_bias"], p["ln_g"], p["ln_b"])


def _split2(q):
    hi = q.astype(BF16)
    lo = (q - hi.astype(F32)).astype(BF16)
    return hi, lo


def _ssd_role(xbc_ref, dt_ref, arow, tri_ref, rexp_ref, dskip_ref, s_ref, y_ref, *, r0, backward):
    lane0 = HEADS if backward else 0
    rows = slice(r0, r0 + CHUNK)
    dt_all = dt_ref[rows, :]
    adt = dt_all * arow
    a1 = adt.astype(BF16)
    r1 = adt - a1.astype(F32)
    a2 = r1.astype(BF16)
    a3 = (r1 - a2.astype(F32)).astype(BF16)
    acs3 = _dot(tri_ref[...], jnp.concatenate([a1, a2, a3], axis=1))
    acs = acs3[:, 0:LANES] + acs3[:, LANES:2 * LANES] + acs3[:, 2 * LANES:]
    edge = 0 if backward else CHUNK - 1
    tot = acs[edge:edge + 1, :]
    decay = jnp.exp(tot - acs)
    eacs = jnp.exp(acs)
    acs2 = acs * LOG2_E
    acs2_t = acs2.T

    hi, lo = _split2(jnp.concatenate([dt_all, decay, eacs], axis=0))
    expanded = _dot(jnp.concatenate([hi, lo], axis=1), rexp_ref[...])
    dt_e = expanded[0:CHUNK]
    decay_e = expanded[CHUNK:2 * CHUNK]
    eacs_e = expanded[2 * CHUNK:]
    etot_e = eacs_e[edge:edge + 1, :]

    xf = xbc_ref[rows, 0:SSD_WIDTH].astype(F32)
    xdt = xf * dt_e
    xdd = (xdt * decay_e).astype(BF16)
    xdt_b = xdt.astype(BF16)

    li = lax.broadcasted_iota(I32, (CHUNK, CHUNK), 0)
    si = lax.broadcasted_iota(I32, (CHUNK, CHUNK), 1)
    keep = (si >= li) if backward else (si <= li)
    left = si < HEAD_DIM
    zero_b = jnp.zeros((CHUNK, LANES), BF16)

    for g in range(GROUPS):
        b_g = xbc_ref[rows, SSD_WIDTH + g * STATE:SSD_WIDTH + (g + 1) * STATE]
        c_g = xbc_ref[rows, SSD_WIDTH + (GROUPS + g) * STATE:SSD_WIDTH + (GROUPS + g + 1) * STATE]
        b_t = b_g.astype(F32).T.astype(BF16)
        cb = _dot(c_g, b_t)
        gc0 = g * HEADS_PER_GROUP * HEAD_DIM
        gw = HEADS_PER_GROUP * HEAD_DIM
        s_old = s_ref[:, gc0:gc0 + gw]
        y_off = _dot(c_g, s_old.astype(BF16))
        s_new = _dot(b_t, xdd[:, gc0:gc0 + gw])
        s_ref[:, gc0:gc0 + gw] = s_old * etot_e[:, gc0:gc0 + gw] + s_new
        for pp in range(HEADS_PER_GROUP // 2):
            c0 = gc0 + pp * LANES
            ms = []
            for q in range(2):
                hl = lane0 + g * HEADS_PER_GROUP + 2 * pp + q
                seg = acs2[:, hl:hl + 1] - acs2_t[hl:hl + 1, :]
                lmat = jnp.exp2(jnp.where(keep, seg, -jnp.inf))
                ms.append((cb * lmat).astype(BF16))
            lhs = jnp.concatenate(ms, axis=1)
            xp = xdt_b[:, c0:c0 + LANES]
            rhs = jnp.concatenate([jnp.where(left, xp, zero_b), jnp.where(left, zero_b, xp)], axis=0)
            y = _dot(lhs, rhs) + y_off[:, pp * LANES:(pp + 1) * LANES] * eacs_e[:, c0:c0 + LANES]
            if not backward:
                y = y + xf[:, c0:c0 + LANES] * dskip_ref[:, c0:c0 + LANES]
            y_ref[rows, c0:c0 + LANES] = y.astype(BF16)


def _ssd_kernel(xf_ref, dtf_ref, xb_ref, dtb_ref, alog_ref, trilo_ref, triup_ref, rf_ref, rb_ref,
                dskip_ref, yf_ref, yb_ref, sf_ref, sb_ref):
    @pl.when(pl.program_id(1) == 0)
    def _():
        sf_ref[...] = jnp.zeros_like(sf_ref)
        sb_ref[...] = jnp.zeros_like(sb_ref)

    arow = -jnp.exp(alog_ref[...])
    n_sub = xf_ref.shape[0] // CHUNK
    for j in range(n_sub):
        _ssd_role(xf_ref, dtf_ref, arow, trilo_ref, rf_ref, dskip_ref, sf_ref, yf_ref,
                  r0=j * CHUNK, backward=False)
        _ssd_role(xb_ref, dtb_ref, arow, triup_ref, rb_ref, dskip_ref, sb_ref, yb_ref,
                  r0=(n_sub - 1 - j) * CHUNK, backward=True)


def _ssd(xbc, dt, batch, seq_len, p):
    t = xbc.shape[0]
    rows = SSD_CHUNKS_PER_STEP * CHUNK
    nc = seq_len // rows
    fwd = lambda b, c: (b * nc + c, 0)
    bwd = lambda b, c: (b * nc + nc - 1 - c, 0)
    return pl.pallas_call(
        _ssd_kernel,
        out_shape=(jax.ShapeDtypeStruct((t, SSD_WIDTH), BF16), jax.ShapeDtypeStruct((t, SSD_WIDTH), BF16)),
        grid=(batch, nc),
        in_specs=[
            pl.BlockSpec((rows, CONV_CH), fwd),
            pl.BlockSpec((rows, LANES), fwd),
            pl.BlockSpec((rows, CONV_CH), bwd),
            pl.BlockSpec((rows, LANES), bwd),
            _const_spec((1, LANES)),
            _const_spec((CHUNK, CHUNK)),
            _const_spec((CHUNK, CHUNK)),
            _const_spec((2 * LANES, SSD_WIDTH)),
            _const_spec((2 * LANES, SSD_WIDTH)),
            _const_spec((1, SSD_WIDTH)),
        ],
        out_specs=(pl.BlockSpec((rows, SSD_WIDTH), fwd), pl.BlockSpec((rows, SSD_WIDTH), bwd)),
        scratch_shapes=[pltpu.VMEM((STATE, SSD_WIDTH), F32), pltpu.VMEM((STATE, SSD_WIDTH), F32)],
        compiler_params=pltpu.CompilerParams(
            dimension_semantics=("arbitrary", "arbitrary"), vmem_limit_bytes=VMEM_LIMIT),
        name="ssd",
    )(xbc, dt, xbc, dt, p["a_log"], p["tri_lo"], p["tri_up"], p["rexp_f"], p["rexp_b"], p["d_skip"])


def _outproj_kernel(yf_ref, yb_ref, sz_ref, u_ref, v_ref, x_ref, gs_ref, wsp_ref, bsp_ref, wo_ref,
                    g2_ref, wr_ref, rb_ref, tri_ref, trie_ref,
                    x1_ref, h2_ref, mt_ref, tm_ref,
                    gm_buf, carry, logit_t):
    i = pl.program_id(0)
    tm = x_ref.shape[0]

    @pl.when(i == 0)
    def _():
        carry[...] = jnp.zeros_like(carry)
        logit_t[...] = jnp.zeros_like(logit_t)

    _route_tile(logit_t[...], jnp.where(i > 0, 1.0, 0.0), tri_ref, trie_ref, carry, mt_ref, tm_ref)

    gated = (yf_ref[...].astype(F32) + yb_ref[...].astype(F32)) * sz_ref[...].astype(F32)
    ms = jnp.mean(gated * gated, axis=-1, keepdims=True)
    ssd = (gated * lax.rsqrt(ms + RMS_EPS) * gs_ref[...]).astype(BF16)

    for c in range(tm // CHUNK):
        r0 = c * CHUNK
        for hh in range(GMLP_HEADS):
            c0 = hh * GMLP_HEAD_DIM
            sv = _dot(wsp_ref[hh], v_ref[r0:r0 + CHUNK, c0:c0 + GMLP_HEAD_DIM]) + bsp_ref[hh]
            gm_buf[r0:r0 + CHUNK, c0:c0 + GMLP_HEAD_DIM] = (
                u_ref[r0:r0 + CHUNK, c0:c0 + GMLP_HEAD_DIM].astype(F32) * sv).astype(BF16)

    x1 = x_ref[...] + _dot(jnp.concatenate([ssd, gm_buf[...]], axis=1), wo_ref[...])
    x1_ref[...] = x1

    ms2 = jnp.mean(x1 * x1, axis=-1, keepdims=True)
    h2 = (x1 * lax.rsqrt(ms2 + RMS_EPS) * g2_ref[...]).astype(BF16)
    h2_ref[...] = h2

    logits = _dot(h2, wr_ref[...]) + rb_ref[...]
    logit_t[...] = logits.T[0:N_EXPERTS, :]


def _route_tile(work, live, tri_ref, trie_ref, carry, mt_ref, tm_ref):
    tm = work.shape[1]
    erow = lax.broadcasted_iota(I32, (N_EXPERTS, tm), 0).astype(F32)
    sel_all = jnp.zeros((N_EXPERTS, tm), F32)
    sels, vals = [], []
    for _ in range(TOP_K):
        m = jnp.max(work, axis=0, keepdims=True)
        idx = jnp.min(jnp.where(work == m, erow, float(N_EXPERTS)), axis=0, keepdims=True)
        sel = erow == idx
        sel_all = jnp.where(sel, 1.0, sel_all)
        work = jnp.where(sel, -jnp.inf, work)
        sels.append(sel)
        vals.append(m)
    exps = [jnp.exp(v - vals[0]) for v in vals]
    denom = exps[0] + exps[1] + exps[2] + exps[3]

    lrank = _dot(sel_all.astype(BF16), tri_ref[...])
    n_e = jnp.sum(sel_all, axis=1, keepdims=True)
    n_al = jnp.broadcast_to(jnp.floor((n_e + (SEG - 1)) * (1.0 / SEG)) * (SEG * live), (N_EXPERTS, LANES))
    lstart = _dot(trie_ref[...], n_al.astype(BF16))
    old_cnt = carry[...]
    new_cnt = old_cnt + n_al
    carry[...] = new_cnt
    tile_row = jnp.concatenate([n_al, old_cnt, lstart, new_cnt], axis=0).T[0:1, :]
    tm_ref[0] = tile_row.astype(I32)

    local_row = lstart[:, 0:1] + lrank
    out_row = lax.broadcasted_iota(I32, (2 * TOP_K, tm), 0)
    meta = jnp.zeros((2 * TOP_K, tm), F32)
    for k in range(TOP_K):
        dk = jnp.sum(jnp.where(sels[k], local_row, 0.0), axis=0, keepdims=True)
        meta = jnp.where(out_row == k, dk, meta)
        meta = jnp.where(out_row == TOP_K + k, exps[k] / denom, meta)
    mt_ref[...] = meta


def _outproj(yf, yb, sz, u, v, x2d, p):
    t = x2d.shape[0]
    tm = TM_OUT
    row = lambda i: (i, 0)
    nt = t // tm
    out_shape = (
        jax.ShapeDtypeStruct((t, D_MODEL), F32),
        jax.ShapeDtypeStruct((t, D_MODEL), BF16),
        jax.ShapeDtypeStruct((nt * 2 * TOP_K, tm), F32),
        jax.ShapeDtypeStruct((nt, 1, LANES), I32),
    )
    row = lambda i: (jnp.minimum(i, nt - 1), 0)
    routed = lambda i: jnp.maximum(i - 1, 0)
    bf_row = pl.BlockSpec((tm, D_MODEL), row)
    return pl.pallas_call(
        _outproj_kernel,
        out_shape=out_shape,
        grid=(nt + 1,),
        in_specs=[
            bf_row, bf_row, bf_row, bf_row, bf_row, bf_row,
            _const_spec((1, SSD_WIDTH)),
            _const_spec((GMLP_HEADS, CHUNK, CHUNK)),
            _const_spec((GMLP_HEADS, CHUNK, GMLP_HEAD_DIM)),
            _const_spec((SSD_WIDTH + GMLP_WIDTH, D_MODEL)),
            _const_spec((1, D_MODEL)),
            _const_spec((D_MODEL, LANES)),
            _const_spec((1, LANES)),
            _const_spec((tm, tm)),
            _const_spec((N_EXPERTS, N_EXPERTS)),
        ],
        out_specs=(
            pl.BlockSpec((tm, D_MODEL), row),
            pl.BlockSpec((tm, D_MODEL), row),
            pl.BlockSpec((2 * TOP_K, tm), lambda i: (routed(i), 0)),
            pl.BlockSpec((1, 1, LANES), lambda i: (routed(i), 0, 0)),
        ),
        scratch_shapes=[pltpu.VMEM((tm, GMLP_WIDTH), BF16), pltpu.VMEM((N_EXPERTS, LANES), F32),
                        pltpu.VMEM((N_EXPERTS, tm), F32)],
        compiler_params=pltpu.CompilerParams(
            dimension_semantics=("arbitrary",), vmem_limit_bytes=VMEM_LIMIT),
        name="outproj",
    )(yf, yb, sz, u, v, x2d, p["gs"], p["w_sp"], p["b_sp"], p["w_out"], p["g2"], p["w_r"], p["b_r"],
      p["tok_before"], p["expert_before"])


def _copy_segment(make_copy, n, src_row, dst_row):
    big, mid, small = COPY_ROWS

    def body(c, carry_):
        make_copy(pl.multiple_of(src_row + c * big, SEG), pl.multiple_of(dst_row + c * big, SEG), big).start()
        return carry_
    nbig = n // big
    lax.fori_loop(0, nbig, body, 0)
    done = nbig * big

    @pl.when((n & mid) != 0)
    def _():
        make_copy(pl.multiple_of(src_row + done, SEG), pl.multiple_of(dst_row + done, SEG), mid).start()

    done2 = done + (n & mid)

    @pl.when((n & small) != 0)
    def _():
        make_copy(pl.multiple_of(src_row + done2, SEG), pl.multiple_of(dst_row + done2, SEG), small).start()


def _wait_rows(make_copy, nrows, limit=R_LOCAL):
    w = 1 << (limit.bit_length() - 1)
    while w >= SEG:
        @pl.when((nrows & w) != 0)
        def _(w=w):
            make_copy(0, 0, w).wait()
        w //= 2


def _onehot_rows(mt_ref, blk, values):
    tm = mt_ref.shape[1]
    riota = lax.broadcasted_iota(I32, (SORT_BLK, tm), 0).astype(F32).astype(BF16)
    out = jnp.zeros((SORT_BLK, tm), BF16)
    for k in reversed(range(TOP_K)):
        off = (mt_ref[k:k + 1, :] - float(blk * SORT_BLK)).astype(BF16)
        out = jnp.where(riota == off, values[k], out)
    return out


def _dispatch_kernel(ps_ref, cn_ref, nu_ref, dst_ref, h2_ref, mt_ref, xs_ref, sbuf, zbuf, sem):
    i = pl.program_id(0)
    nt = pl.num_programs(0)
    tm = h2_ref.shape[0]
    groups_per_blk = SORT_BLK // SEG

    def wait_tile():
        pltpu.make_async_copy(sbuf, xs_ref.at[pl.ds(0, R_LOCAL)], sem).wait()

    h2 = h2_ref[...]
    ones = [jnp.ones((1, tm), BF16)] * TOP_K
    for blk in range(R_LOCAL // SORT_BLK):
        s = _dot(_onehot_rows(mt_ref, blk, ones), h2)
        lo_bits = lax.bitcast_convert_type(s[:, 0:HALF], U32)
        hi_bits = lax.bitcast_convert_type(s[:, HALF:], U32)
        packed = (lo_bits >> 16) | hi_bits
        if blk == 0:
            @pl.when(i > 0)
            def _():
                wait_tile()
        sbuf[blk * SORT_BLK:(blk + 1) * SORT_BLK, :] = packed
        for g in range(blk * groups_per_blk, (blk + 1) * groups_per_blk):
            pltpu.make_async_copy(sbuf.at[pl.ds(g * SEG, SEG)],
                                  xs_ref.at[pl.ds(pl.multiple_of(dst_ref[g], SEG), SEG)], sem).start()

    @pl.when(i == nt - 1)
    def _():
        wait_tile()
        zbuf[...] = jnp.zeros_like(zbuf)

        def zero_copy(src_row, dst_row, rows):
            return pltpu.make_async_copy(zbuf.at[pl.ds(src_row, rows)], xs_ref.at[pl.ds(dst_row, rows)], sem)

        bm = zbuf.shape[0]

        def tail_rows(e):
            return (cn_ref[e] + (bm - 1)) // bm * bm - cn_ref[e]

        def start_tail(e, c):
            _copy_segment(zero_copy, tail_rows(e), 0, ps_ref[e] + cn_ref[e])
            return c
        lax.fori_loop(0, N_EXPERTS, start_tail, 0)

        def start_block(b, c):
            zero_copy(0, pl.multiple_of(b * bm, bm), bm).start()
            return c
        n_rows_total = xs_ref.shape[0]
        lax.fori_loop(nu_ref[0], n_rows_total // bm, start_block, 0)

        def wait_tail(e, c):
            _wait_rows(zero_copy, tail_rows(e), limit=bm - SEG)
            return c
        lax.fori_loop(0, N_EXPERTS, wait_tail, 0)

        def wait_block(b, c):
            zero_copy(0, 0, bm).wait()
            return c
        lax.fori_loop(nu_ref[0], n_rows_total // bm, wait_block, 0)


def _dispatch(pad_start, counts, n_used, dst_table, h2, mt, rows, bm):
    t = h2.shape[0]
    tm = TM_OUT
    grid_spec = pltpu.PrefetchScalarGridSpec(
        num_scalar_prefetch=3,
        grid=(t // tm,),
        in_specs=[
            pl.BlockSpec((ROW_TABLE,), lambda i, *_: (i,), memory_space=pltpu.SMEM),
            pl.BlockSpec((tm, D_MODEL), lambda i, *_: (i, 0)),
            pl.BlockSpec((2 * TOP_K, tm), lambda i, *_: (i, 0)),
        ],
        out_specs=pl.BlockSpec(memory_space=pl.ANY),
        scratch_shapes=[pltpu.VMEM((R_LOCAL, HALF), U32), pltpu.VMEM((bm, HALF), U32),
                        pltpu.SemaphoreType.DMA(())],
    )
    return pl.pallas_call(
        _dispatch_kernel,
        out_shape=jax.ShapeDtypeStruct((rows, HALF), U32),
        grid_spec=grid_spec,
        compiler_params=pltpu.CompilerParams(
            dimension_semantics=("arbitrary",), vmem_limit_bytes=VMEM_LIMIT, has_side_effects=True),
        name="dispatch",
    )(pad_start, counts, n_used, dst_table, h2, mt)


def _experts_kernel(be_ref, nu_ref, xs_ref, wgu_ref, bgu_ref, wd_ref, bd_ref, ys_ref, wgu_bf, wd_bf):
    i = pl.program_id(0)
    used = i < nu_ref[0]

    @pl.when(used & ((i == 0) | (be_ref[i] != be_ref[jnp.maximum(i - 1, 0)])))
    def _():
        wgu_bf[...] = wgu_ref[0].astype(BF16)
        wd_bf[...] = wd_ref[0].astype(BF16)

    @pl.when(used)
    def _():
        lo, hi = _unpack_bf16_pair(xs_ref[...])
        xb = jnp.concatenate([lo.astype(BF16), hi.astype(BF16)], axis=1)
        gu = _dot(xb, wgu_bf[...]) + bgu_ref[0]
        gate = jnp.minimum(gu[:, 0:D_FF], SWIGLU_LIMIT)
        up = jnp.clip(gu[:, D_FF:], -SWIGLU_LIMIT, SWIGLU_LIMIT)
        act = (up + 1.0) * (gate * jax.nn.sigmoid(gate * SWIGLU_ALPHA))
        y = _dot(act.astype(BF16), wd_bf[...]) + bd_ref[0]
        ys_ref[...] = _pack_bf16_pair(y[:, 0:HALF], y[:, HALF:])

    @pl.when(i >= nu_ref[0])
    def _():
        ys_ref[...] = jnp.zeros_like(ys_ref)


def _experts(block_e, n_used, xs, p, bm):
    rows = xs.shape[0]
    nb = rows // bm
    grid_spec = pltpu.PrefetchScalarGridSpec(
        num_scalar_prefetch=2,
        grid=(nb,),
        in_specs=[
            pl.BlockSpec((bm, HALF), lambda i, be, *_: (i, 0)),
            pl.BlockSpec((1, D_MODEL, 2 * D_FF), lambda i, be, *_: (be[i], 0, 0)),
            pl.BlockSpec((1, 1, 2 * D_FF), lambda i, be, *_: (be[i], 0, 0)),
            pl.BlockSpec((1, D_FF, D_MODEL), lambda i, be, *_: (be[i], 0, 0)),
            pl.BlockSpec((1, 1, D_MODEL), lambda i, be, *_: (be[i], 0, 0)),
        ],
        out_specs=pl.BlockSpec((bm, HALF), lambda i, be, *_: (i, 0)),
        scratch_shapes=[pltpu.VMEM((D_MODEL, 2 * D_FF), BF16), pltpu.VMEM((D_FF, D_MODEL), BF16)],
    )
    return pl.pallas_call(
        _experts_kernel,
        out_shape=jax.ShapeDtypeStruct((rows, HALF), U32),
        grid_spec=grid_spec,
        compiler_params=pltpu.CompilerParams(
            dimension_semantics=("arbitrary",), vmem_limit_bytes=VMEM_LIMIT),
        name="experts",
    )(block_e, n_used, xs, p["w_gu"], p["b_gu"], p["w_d"], p["b_d"])


def _combine_kernel(cur_ref, nxt_ref, mt_ref, x1_ref, gf_ref, ys_ref, o_ref,
                    ybuf, gt_buf, ylo_buf, yhi_buf, sems):
    i = pl.program_id(0)
    nt = pl.num_programs(0)

    def issue(table_ref, slot):
        for g in range(R_LOCAL // SEG):
            pltpu.make_async_copy(ys_ref.at[pl.ds(pl.multiple_of(table_ref[g], SEG), SEG)],
                                  ybuf.at[slot, pl.ds(g * SEG, SEG)], sems.at[slot]).start()

    def wait_slot(slot):
        pltpu.make_async_copy(ys_ref.at[pl.ds(0, R_LOCAL)], ybuf.at[slot], sems.at[slot]).wait()

    @pl.when(i == 0)
    def _():
        issue(cur_ref, 0)

    slot = i % 2
    issue(nxt_ref, 1 - slot)

    gates = [mt_ref[TOP_K + k:TOP_K + k + 1, :].astype(BF16) for k in range(TOP_K)]
    for blk in range(R_LOCAL // SORT_BLK):
        rows = slice(blk * SORT_BLK, (blk + 1) * SORT_BLK)
        gt_buf[rows, :] = _onehot_rows(mt_ref, blk, gates)
    wait_slot(slot)
    for blk in range(R_LOCAL // SORT_BLK):
        rows = slice(blk * SORT_BLK, (blk + 1) * SORT_BLK)
        lo, hi = _unpack_bf16_pair(ybuf[slot, rows, :])
        ylo_buf[rows, :] = lo.astype(BF16)
        yhi_buf[rows, :] = hi.astype(BF16)
    contract_rows = (((0,), (0,)), ((), ()))
    g_t = gt_buf[...]
    x1 = x1_ref[...]
    acc = jnp.concatenate(
        [x1[:, 0:HALF] + lax.dot_general(g_t, ylo_buf[...], contract_rows, preferred_element_type=F32),
         x1[:, HALF:] + lax.dot_general(g_t, yhi_buf[...], contract_rows, preferred_element_type=F32)], axis=1)
    ms = jnp.mean(acc * acc, axis=-1, keepdims=True)
    o_ref[...] = acc * lax.rsqrt(ms + RMS_EPS) * gf_ref[...]

    @pl.when(i == nt - 1)
    def _():
        wait_slot(1 - slot)


def _combine(src_table, mt, x1, ys, p):
    t = x1.shape[0]
    tm = TM_OUT
    nt = t // tm
    grid_spec = pltpu.PrefetchScalarGridSpec(
        num_scalar_prefetch=0,
        grid=(nt,),
        in_specs=[
            pl.BlockSpec((ROW_TABLE,), lambda i: (i,), memory_space=pltpu.SMEM),
            pl.BlockSpec((ROW_TABLE,), lambda i: (jnp.minimum(i + 1, nt - 1),), memory_space=pltpu.SMEM),
            pl.BlockSpec((2 * TOP_K, tm), lambda i: (i, 0)),
            pl.BlockSpec((tm, D_MODEL), lambda i: (i, 0)),
            pl.BlockSpec((1, D_MODEL), lambda i: (0, 0)),
            pl.BlockSpec(memory_space=pl.ANY),
        ],
        out_specs=pl.BlockSpec((tm, D_MODEL), lambda i: (i, 0)),
        scratch_shapes=[pltpu.VMEM((2, R_LOCAL, HALF), U32), pltpu.VMEM((R_LOCAL, tm), BF16),
                        pltpu.VMEM((R_LOCAL, HALF), BF16), pltpu.VMEM((R_LOCAL, HALF), BF16),
                        pltpu.SemaphoreType.DMA((2,))],
    )
    return pl.pallas_call(
        _combine_kernel,
        out_shape=jax.ShapeDtypeStruct((t, D_MODEL), F32),
        grid_spec=grid_spec,
        compiler_params=pltpu.CompilerParams(
            dimension_semantics=("arbitrary",), vmem_limit_bytes=VMEM_LIMIT),
        name="combine",
    )(src_table, src_table, mt, x1, p["gf"], ys)


def _prepare_params(norm1_g, w_in, conv_w, conv_b, dt_bias_f, dt_bias_b, a_log_f, a_log_b, d_skip,
                    ssd_norm_g, gmlp_ln_g, gmlp_ln_b, w_spatial, b_spatial, w_out, norm2_g,
                    router_w, router_b, w_gate_up, b_gate_up, w_down, b_down, final_g):
    z_end = SSD_WIDTH
    xbc_end = z_end + CONV_CH
    dt_end = xbc_end + 2 * HEADS
    u_end = dt_end + GMLP_WIDTH
    w = w_in[0]
    pad_lanes = lambda a, fill=0.0: jnp.pad(a, ((0, 0), (0, LANES - a.shape[1])), constant_values=fill)
    head_lane = jnp.arange(LANES)[:, None]
    chan_head = (jnp.arange(SSD_WIDTH) // HEAD_DIM)[None, :]
    ar = jnp.arange(CHUNK)
    art = jnp.arange(TM_OUT)
    are = jnp.arange(N_EXPERTS)
    return {
        "g1": norm1_g[0][None, :],
        "w_z": w[:, :z_end].astype(BF16),
        "w_xbc": w[:, z_end:xbc_end].astype(BF16),
        "w_dt": pad_lanes(w[:, xbc_end:dt_end]).astype(BF16),
        "w_u": w[:, dt_end:u_end].astype(BF16),
        "w_v": w[:, u_end:].astype(BF16),
        "conv_w": jnp.pad(conv_w[0], ((0, 8 - CONV_WIDTH), (0, 0))),
        "conv_b": conv_b[0][None, :],
        "dt_bias": pad_lanes(jnp.concatenate([dt_bias_f[0], dt_bias_b[0]])[None, :]),
        "ln_g": gmlp_ln_g[0][None, :],
        "ln_b": gmlp_ln_b[0][None, :],
        "a_log": pad_lanes(jnp.concatenate([a_log_f[0], a_log_b[0]])[None, :]),
        "tri_lo": (ar[:, None] >= ar[None, :]).astype(BF16),
        "tri_up": (ar[:, None] <= ar[None, :]).astype(BF16),
        "rexp_f": jnp.tile((head_lane == chan_head).astype(BF16), (2, 1)),
        "rexp_b": jnp.tile((head_lane == chan_head + HEADS).astype(BF16), (2, 1)),
        "d_skip": jnp.repeat(d_skip[0], HEAD_DIM)[None, :],
        "gs": ssd_norm_g[0][None, :],
        "w_sp": w_spatial[0].astype(BF16),
        "b_sp": jnp.broadcast_to(b_spatial[0][:, :, None], (GMLP_HEADS, CHUNK, GMLP_HEAD_DIM)),
        "w_out": w_out[0].astype(BF16),
        "g2": norm2_g[0][None, :],
        "w_r": pad_lanes(router_w[0]).astype(BF16),
        "b_r": pad_lanes(router_b[0][None, :], NEG_BIG),
        "tok_before": (art[:, None] < art[None, :]).astype(BF16),
        "expert_before": (are[None, :] < are[:, None]).astype(BF16),
        "w_gu": w_gate_up[0],
        "b_gu": b_gate_up[0][:, None, :],
        "w_d": w_down[0],
        "b_d": b_down[0][:, None, :],
        "gf": final_g[None, :],
    }


def _encoder(x, p):
    batch, seq_len, _ = x.shape
    t = batch * seq_len
    x2d = x.reshape(t, D_MODEL)
    sz, xbc, dt, u, v = _inproj(x2d, seq_len, p)
    yf, yb = _ssd(xbc, dt, batch, seq_len, p)
    x1, h2, mt, tile_rows = _outproj(yf, yb, sz, u, v, x2d, p)
    tile_rows = tile_rows[:, 0, :]
    tn, tc, tl = (tile_rows[:, k * N_EXPERTS:(k + 1) * N_EXPERTS] for k in range(3))

    counts = tile_rows[-1, 3 * N_EXPERTS:]
    rows_per_expert = t * TOP_K // N_EXPERTS
    bm = next((b for b in BM_CHOICES if rows_per_expert >= BM_MIN_BLOCKS * b), BM_CHOICES[-1])
    spare_blocks = -(-R_LOCAL // bm)
    padded = (counts + bm - 1) // bm * bm
    pad_end = jnp.cumsum(padded)
    pad_start = (pad_end - padded).astype(I32)
    n_tiles = t // TM_OUT
    n_blocks = -(-(t * TOP_K + n_tiles * N_EXPERTS * (SEG - 1) + N_EXPERTS * (bm - 1)) // bm)
    n_used = (pad_end[-1] // bm).astype(I32)
    blk_row = jnp.minimum(jnp.arange(n_blocks + spare_blocks, dtype=I32), n_used - 1) * bm
    onehot_e = (pad_end[None, :] <= blk_row[:, None]).astype(I32)
    block_e = jnp.minimum(jnp.sum(onehot_e, axis=1), N_EXPERTS - 1).astype(I32)
    n_used = n_used.reshape(1)

    group_row = jnp.arange(R_LOCAL // SEG, dtype=I32) * SEG
    owner = jnp.sum(((tl + tn)[:, None, :] <= group_row[None, :, None]).astype(I32), axis=-1)
    is_owner = (owner[:, :, None] == jnp.arange(N_EXPERTS, dtype=I32)[None, None, :]).astype(I32)
    base = jnp.sum(is_owner * (pad_start[None, :] + tc - tl)[:, None, :], axis=-1)
    used = owner < N_EXPERTS
    pad_table = lambda a: jnp.pad(a, ((0, 0), (0, ROW_TABLE - a.shape[1]))).reshape(-1)
    dst_table = pad_table(jnp.where(used, base + group_row[None, :], n_blocks * bm + group_row[None, :]))
    src_table = pad_table(jnp.where(used, base + group_row[None, :], 0))

    xs = _dispatch(pad_start, counts, n_used, dst_table, h2, mt, (n_blocks + spare_blocks) * bm, bm)
    ys = _experts(block_e, n_used, xs, p, bm)
    y = _combine(src_table, mt, x1, ys, p)
    return y.reshape(batch, seq_len, D_MODEL)


def kernel(x_prompt, x_sample, norm1_g, w_in, conv_w, conv_b, dt_bias_f, dt_bias_b, a_log_f, a_log_b,
           d_skip, ssd_norm_g, gmlp_ln_g, gmlp_ln_b, w_spatial, b_spatial, w_out, norm2_g, router_w,
           router_b, w_gate_up, b_gate_up, w_down, b_down, final_g):
    p = _prepare_params(norm1_g, w_in, conv_w, conv_b, dt_bias_f, dt_bias_b, a_log_f, a_log_b, d_skip,
                        ssd_norm_g, gmlp_ln_g, gmlp_ln_b, w_spatial, b_spatial, w_out, norm2_g,
                        router_w, router_b, w_gate_up, b_gate_up, w_down, b_down, final_g)
    return (_encoder(x_prompt, p), _encoder(x_sample, p))
```

```python
import functools

import jax
import jax.numpy as jnp
from jax import lax
from jax.experimental import pallas as pl
from jax.experimental.pallas import tpu as pltpu

F32 = jnp.float32
BF16 = jnp.bfloat16
U32 = jnp.uint32
I32 = jnp.int32

D_MODEL = 1024
SSD_WIDTH = 1024
HEAD_DIM = 64
HEADS = 16
GROUPS = 2
HEADS_PER_GROUP = 8
STATE = 128
CHUNK = 128
CONV_WIDTH = 5
CONV_HALF = 2
CONV_CH = SSD_WIDTH + 2 * GROUPS * STATE
GMLP_WIDTH = 1024
GMLP_HEADS = 8
GMLP_HEAD_DIM = 128
N_EXPERTS = 32
TOP_K = 4
D_FF = 1024
SWIGLU_LIMIT = 7.0
SWIGLU_ALPHA = 1.702
RMS_EPS = 1e-5
LN_EPS = 1e-5

LANES = 128
HALO = 16
SSD_CHUNKS_PER_STEP = 8
TM_IN = 512
IN_COLS = 256
IN_ROWS = 512
TM_OUT = 512
BM_CHOICES = (1024, 512)
BM_MIN_BLOCKS = 6
SEG = 8
COPY_ROWS = (32, 16, 8)
SORT_BLK = 256
R_LOCAL = -(-(TM_OUT * TOP_K + N_EXPERTS * (SEG - 1)) // SORT_BLK) * SORT_BLK
ROW_TABLE = 1024
HALF = D_MODEL // 2
VMEM_LIMIT = 56 * 1024 * 1024
NEG_BIG = -1e30
LOG2_E = 1.4426950408889634


def _dot(a, b):
    return jnp.dot(a, b, preferred_element_type=F32)


def _gelu(x):
    return 0.5 * x * (1.0 + lax.erf(x * (2.0 ** -0.5)))


def _pack_bf16_pair(lo, hi):
    lo_bits = lax.bitcast_convert_type(lo.astype(BF16).astype(F32), U32)
    hi_bits = lax.bitcast_convert_type(hi.astype(BF16).astype(F32), U32)
    return (lo_bits >> 16) | hi_bits


def _unpack_bf16_pair(w):
    lo = lax.bitcast_convert_type(w << 16, F32)
    hi = lax.bitcast_convert_type(w & jnp.uint32(0xFFFF0000), F32)
    return lo, hi


def _const_spec(shape):
    nd = len(shape)
    return pl.BlockSpec(shape, lambda *_: (0,) * nd)


def _inproj_kernel(x_ref, xp_ref, xn_ref, g1_ref, wz_ref, wxbc_ref, wdt_ref, wu_ref, wv_ref,
                   cw_ref, cb_ref, dtb_ref, lng_ref, lnb_ref,
                   sz_ref, xbc_ref, dt_ref, u_ref, v_ref, buf0, buf1, buf2, gbuf, *, tiles_per_seq):
    i = pl.program_id(0)
    tm = x_ref.shape[0]
    g1 = g1_ref[...]
    row_blocks = [(r0, IN_ROWS) for r0 in range(0, tm, IN_ROWS)]

    def norm(xv):
        ms = jnp.mean(xv * xv, axis=-1, keepdims=True)
        return xv * lax.rsqrt(ms + RMS_EPS) * g1

    keep_prev = jnp.where(i % tiles_per_seq == 0, 0.0, 1.0)
    keep_next = jnp.where(i % tiles_per_seq == tiles_per_seq - 1, 0.0, 1.0)
    h = norm(x_ref[...]).astype(BF16)
    h_all = jnp.concatenate([(norm(xp_ref[...]) * keep_prev).astype(BF16), h,
                             (norm(xn_ref[...]) * keep_next).astype(BF16)], axis=0)

    def silu_to(out_ref):
        def epilogue(buf, c0, wd):
            for r0, nr in row_blocks:
                r = buf[r0:r0 + nr, 0:wd]
                out_ref[r0:r0 + nr, c0:c0 + wd] = (r * jax.nn.sigmoid(r)).astype(BF16)
        return epilogue

    def conv_silu(buf, c0, wd):
        for r0, nr in row_blocks:
            acc = jnp.broadcast_to(cb_ref[:, c0:c0 + wd], (nr, wd))
            for k in range(CONV_WIDTH):
                acc = acc + cw_ref[k:k + 1, c0:c0 + wd] * buf[pl.ds(r0 + HALO - CONV_HALF + k, nr), 0:wd]
            xbc_ref[r0:r0 + nr, c0:c0 + wd] = (acc * jax.nn.sigmoid(acc)).astype(BF16)

    def softplus_dt(buf, c0, wd):
        dt_ref[...] = jax.nn.softplus(buf[0:tm, 0:wd] + dtb_ref[...])

    def gelu_u(buf, c0, wd):
        for r0, nr in row_blocks:
            u_ref[r0:r0 + nr, c0:c0 + wd] = _gelu(buf[r0:r0 + nr, 0:wd]).astype(BF16)

    sums = [jnp.zeros((nr, LANES), F32) for _, nr in row_blocks]

    def gelu_v(buf, c0, wd):
        for b, (r0, nr) in enumerate(row_blocks):
            g = _gelu(buf[r0:r0 + nr, 0:wd])
            gbuf[r0:r0 + nr, c0:c0 + wd] = g
            for l0 in range(0, wd, LANES):
                sums[b] = sums[b] + g[:, l0:l0 + LANES]

    heavy = [(h_all, wxbc_ref, c0, IN_COLS, conv_silu) for c0 in range(0, CONV_CH, IN_COLS)]
    light = [(h, wz_ref, c0, IN_COLS, silu_to(sz_ref)) for c0 in range(0, SSD_WIDTH, IN_COLS)]
    light += [(h, wu_ref, c0, IN_COLS, gelu_u) for c0 in range(0, GMLP_WIDTH, IN_COLS)]
    light += [(h, wv_ref, c0, IN_COLS, gelu_v) for c0 in range(0, GMLP_WIDTH, IN_COLS)]
    light.append((h, wdt_ref, 0, LANES, softplus_dt))
    stages = []
    while heavy or light:
        if heavy:
            stages.append(heavy.pop(0))
        if light:
            stages.append(light.pop(0))
    bufs = (buf0, buf1, buf2)
    ahead = len(bufs) - 1

    def matmul(j):
        lhs, w_ref, c0, wd, _ = stages[j]
        bufs[j % len(bufs)][0:lhs.shape[0], 0:wd] = _dot(lhs, w_ref[:, c0:c0 + wd])

    for j in range(ahead):
        matmul(j)
    for j, (_, _, c0, wd, epilogue) in enumerate(stages):
        if j + ahead < len(stages):
            matmul(j + ahead)
        epilogue(bufs[j % len(bufs)], c0, wd)

    inv_n = 1.0 / GMLP_WIDTH
    for b, (r0, nr) in enumerate(row_blocks):
        mu = jnp.sum(sums[b], axis=-1, keepdims=True) * inv_n
        sq = jnp.zeros((nr, LANES), F32)
        for l0 in range(0, GMLP_WIDTH, LANES):
            d = gbuf[r0:r0 + nr, l0:l0 + LANES] - mu
            sq = sq + d * d
        scale = lax.rsqrt(jnp.sum(sq, axis=-1, keepdims=True) * inv_n + LN_EPS)
        for c0 in range(0, GMLP_WIDTH, IN_COLS):
            g = gbuf[r0:r0 + nr, c0:c0 + IN_COLS]
            v_ref[r0:r0 + nr, c0:c0 + IN_COLS] = (
                (g - mu) * scale * lng_ref[:, c0:c0 + IN_COLS] + lnb_ref[:, c0:c0 + IN_COLS]).astype(BF16)


def _inproj(x2d, seq_len, p):
    t = x2d.shape[0]
    tm = TM_IN
    tiles_per_seq = seq_len // tm
    nh = tm // HALO
    last_halo = t // HALO - 1
    kern = functools.partial(_inproj_kernel, tiles_per_seq=tiles_per_seq)
    row = lambda i: (i, 0)
    out_shape = (
        jax.ShapeDtypeStruct((t, SSD_WIDTH), BF16),
        jax.ShapeDtypeStruct((t, CONV_CH), BF16),
        jax.ShapeDtypeStruct((t, LANES), F32),
        jax.ShapeDtypeStruct((t, GMLP_WIDTH), BF16),
        jax.ShapeDtypeStruct((t, GMLP_WIDTH), BF16),
    )
    return pl.pallas_call(
        kern,
        out_shape=out_shape,
        grid=(t // tm,),
        in_specs=[
            pl.BlockSpec((tm, D_MODEL), row),
            pl.BlockSpec((HALO, D_MODEL), lambda i: (jnp.maximum(i * nh - 1, 0), 0)),
            pl.BlockSpec((HALO, D_MODEL), lambda i: (jnp.minimum((i + 1) * nh, last_halo), 0)),
            _const_spec((1, D_MODEL)),
            _const_spec((D_MODEL, SSD_WIDTH)),
            _const_spec((D_MODEL, CONV_CH)),
            _const_spec((D_MODEL, LANES)),
            _const_spec((D_MODEL, GMLP_WIDTH)),
            _const_spec((D_MODEL, GMLP_WIDTH)),
            _const_spec((8, CONV_CH)),
            _const_spec((1, CONV_CH)),
            _const_spec((1, LANES)),
            _const_spec((1, GMLP_WIDTH)),
            _const_spec((1, GMLP_WIDTH)),
        ],
        out_specs=(
            pl.BlockSpec((tm, SSD_WIDTH), row),
            pl.BlockSpec((tm, CONV_CH), row),
            pl.BlockSpec((tm, LANES), row),
            pl.BlockSpec((tm, GMLP_WIDTH), row),
            pl.BlockSpec((tm, GMLP_WIDTH), row),
        ),
        scratch_shapes=[pltpu.VMEM((tm + 2 * HALO, IN_COLS), F32)] * 3 + [pltpu.VMEM((tm, GMLP_WIDTH), F32)],
        compiler_params=pltpu.CompilerParams(
            dimension_semantics=("arbitrary",), vmem_limit_bytes=VMEM_LIMIT),
        name="inproj",
    )(x2d, x2d, x2d, p["g1"], p["w_z"], p["w_xbc"], p["w_dt"], p["w_u"], p["w_v"],
      p["conv_w"], p["conv_b"], p["dt_bias"], p["ln_g"], p["ln_b"])


def _split2(q):
    hi = q.astype(BF16)
    lo = (q - hi.astype(F32)).astype(BF16)
    return hi, lo


def _ssd_role(xbc_ref, dt_ref, arow, tri_ref, rexp_ref, dskip_ref, s_ref, y_ref, *, r0, backward):
    lane0 = HEADS if backward else 0
    rows = slice(r0, r0 + CHUNK)
    dt_all = dt_ref[rows, :]
    adt = dt_all * arow
    a1 = adt.astype(BF16)
    r1 = adt - a1.astype(F32)
    a2 = r1.astype(BF16)
    a3 = (r1 - a2.astype(F32)).astype(BF16)
    acs3 = _dot(tri_ref[...], jnp.concatenate([a1, a2, a3], axis=1))
    acs = acs3[:, 0:LANES] + acs3[:, LANES:2 * LANES] + acs3[:, 2 * LANES:]
    edge = 0 if backward else CHUNK - 1
    tot = acs[edge:edge + 1, :]
    decay = jnp.exp(tot - acs)
    eacs = jnp.exp(acs)
    acs2 = acs * LOG2_E
    acs2_t = acs2.T

    hi, lo = _split2(jnp.concatenate([dt_all, decay, eacs], axis=0))
    expanded = _dot(jnp.concatenate([hi, lo], axis=1), rexp_ref[...])
    dt_e = expanded[0:CHUNK]
    decay_e = expanded[CHUNK:2 * CHUNK]
    eacs_e = expanded[2 * CHUNK:]
    etot_e = eacs_e[edge:edge + 1, :]

    xf = xbc_ref[rows, 0:SSD_WIDTH].astype(F32)
    xdt = xf * dt_e
    xdd = (xdt * decay_e).astype(BF16)
    xdt_b = xdt.astype(BF16)

    li = lax.broadcasted_iota(I32, (CHUNK, CHUNK), 0)
    si = lax.broadcasted_iota(I32, (CHUNK, CHUNK), 1)
    keep = (si >= li) if backward else (si <= li)
    left = si < HEAD_DIM
    zero_b = jnp.zeros((CHUNK, LANES), BF16)

    for g in range(GROUPS):
        b_g = xbc_ref[rows, SSD_WIDTH + g * STATE:SSD_WIDTH + (g + 1) * STATE]
        c_g = xbc_ref[rows, SSD_WIDTH + (GROUPS + g) * STATE:SSD_WIDTH + (GROUPS + g + 1) * STATE]
        b_t = b_g.astype(F32).T.astype(BF16)
        cb = _dot(c_g, b_t)
        gc0 = g * HEADS_PER_GROUP * HEAD_DIM
        gw = HEADS_PER_GROUP * HEAD_DIM
        s_old = s_ref[:, gc0:gc0 + gw]
        y_off = _dot(c_g, s_old.astype(BF16))
        s_new = _dot(b_t, xdd[:, gc0:gc0 + gw])
        s_ref[:, gc0:gc0 + gw] = s_old * etot_e[:, gc0:gc0 + gw] + s_new
        for pp in range(HEADS_PER_GROUP // 2):
            c0 = gc0 + pp * LANES
            ms = []
            for q in range(2):
                hl = lane0 + g * HEADS_PER_GROUP + 2 * pp + q
                seg = acs2[:, hl:hl + 1] - acs2_t[hl:hl + 1, :]
                lmat = jnp.exp2(jnp.where(keep, seg, -jnp.inf))
                ms.append((cb * lmat).astype(BF16))
            lhs = jnp.concatenate(ms, axis=1)
            xp = xdt_b[:, c0:c0 + LANES]
            rhs = jnp.concatenate([jnp.where(left, xp, zero_b), jnp.where(left, zero_b, xp)], axis=0)
            y = _dot(lhs, rhs) + y_off[:, pp * LANES:(pp + 1) * LANES] * eacs_e[:, c0:c0 + LANES]
            if not backward:
                y = y + xf[:, c0:c0 + LANES] * dskip_ref[:, c0:c0 + LANES]
            y_ref[rows, c0:c0 + LANES] = y.astype(BF16)


def _ssd_kernel(xf_ref, dtf_ref, xb_ref, dtb_ref, alog_ref, trilo_ref, triup_ref, rf_ref, rb_ref,
                dskip_ref, yf_ref, yb_ref, sf_ref, sb_ref):
    @pl.when(pl.program_id(1) == 0)
    def _():
        sf_ref[...] = jnp.zeros_like(sf_ref)
        sb_ref[...] = jnp.zeros_like(sb_ref)

    arow = -jnp.exp(alog_ref[...])
    n_sub = xf_ref.shape[0] // CHUNK
    for j in range(n_sub):
        _ssd_role(xf_ref, dtf_ref, arow, trilo_ref, rf_ref, dskip_ref, sf_ref, yf_ref,
                  r0=j * CHUNK, backward=False)
        _ssd_role(xb_ref, dtb_ref, arow, triup_ref, rb_ref, dskip_ref, sb_ref, yb_ref,
                  r0=(n_sub - 1 - j) * CHUNK, backward=True)


def _ssd(xbc, dt, batch, seq_len, p):
    t = xbc.shape[0]
    rows = SSD_CHUNKS_PER_STEP * CHUNK
    nc = seq_len // rows
    fwd = lambda b, c: (b * nc + c, 0)
    bwd = lambda b, c: (b * nc + nc - 1 - c, 0)
    return pl.pallas_call(
        _ssd_kernel,
        out_shape=(jax.ShapeDtypeStruct((t, SSD_WIDTH), BF16), jax.ShapeDtypeStruct((t, SSD_WIDTH), BF16)),
        grid=(batch, nc),
        in_specs=[
            pl.BlockSpec((rows, CONV_CH), fwd),
            pl.BlockSpec((rows, LANES), fwd),
            pl.BlockSpec((rows, CONV_CH), bwd),
            pl.BlockSpec((rows, LANES), bwd),
            _const_spec((1, LANES)),
            _const_spec((CHUNK, CHUNK)),
            _const_spec((CHUNK, CHUNK)),
            _const_spec((2 * LANES, SSD_WIDTH)),
            _const_spec((2 * LANES, SSD_WIDTH)),
            _const_spec((1, SSD_WIDTH)),
        ],
        out_specs=(pl.BlockSpec((rows, SSD_WIDTH), fwd), pl.BlockSpec((rows, SSD_WIDTH), bwd)),
        scratch_shapes=[pltpu.VMEM((STATE, SSD_WIDTH), F32), pltpu.VMEM((STATE, SSD_WIDTH), F32)],
        compiler_params=pltpu.CompilerParams(
            dimension_semantics=("arbitrary", "arbitrary"), vmem_limit_bytes=VMEM_LIMIT),
        name="ssd",
    )(xbc, dt, xbc, dt, p["a_log"], p["tri_lo"], p["tri_up"], p["rexp_f"], p["rexp_b"], p["d_skip"])


def _outproj_kernel(yf_ref, yb_ref, sz_ref, u_ref, v_ref, x_ref, gs_ref, wsp_ref, bsp_ref, wo_ref,
                    g2_ref, wr_ref, rb_ref, tri_ref, trie_ref,
                    x1_ref, h2_ref, mt_ref, tm_ref,
                    gm_buf, carry, logit_t):
    i = pl.program_id(0)
    tm = x_ref.shape[0]

    @pl.when(i == 0)
    def _():
        carry[...] = jnp.zeros_like(carry)
        logit_t[...] = jnp.zeros_like(logit_t)

    _route_tile(logit_t[...], jnp.where(i > 0, 1.0, 0.0), tri_ref, trie_ref, carry, mt_ref, tm_ref)

    gated = ((yf_ref[...] + yb_ref[...]) * sz_ref[...]).astype(F32)
    ms = jnp.mean(gated * gated, axis=-1, keepdims=True)
    ssd = (gated * lax.rsqrt(ms + RMS_EPS) * gs_ref[...]).astype(BF16)

    for c in range(tm // CHUNK):
        r0 = c * CHUNK
        for hh in range(GMLP_HEADS):
            c0 = hh * GMLP_HEAD_DIM
            sv = _dot(wsp_ref[hh], v_ref[r0:r0 + CHUNK, c0:c0 + GMLP_HEAD_DIM]) + bsp_ref[hh]
            gm_buf[r0:r0 + CHUNK, c0:c0 + GMLP_HEAD_DIM] = (
                u_ref[r0:r0 + CHUNK, c0:c0 + GMLP_HEAD_DIM].astype(F32) * sv).astype(BF16)

    x1 = x_ref[...] + _dot(jnp.concatenate([ssd, gm_buf[...]], axis=1), wo_ref[...])
    x1_ref[...] = x1

    ms2 = jnp.mean(x1 * x1, axis=-1, keepdims=True)
    h2 = (x1 * lax.rsqrt(ms2 + RMS_EPS) * g2_ref[...]).astype(BF16)
    h2_ref[...] = h2

    logits = _dot(h2, wr_ref[...]) + rb_ref[...]
    logit_t[...] = logits.T[0:N_EXPERTS, :]


def _route_tile(work, live, tri_ref, trie_ref, carry, mt_ref, tm_ref):
    tm = work.shape[1]
    erow = lax.broadcasted_iota(I32, (N_EXPERTS, tm), 0).astype(F32)
    sel_all = jnp.zeros((N_EXPERTS, tm), F32)
    sels, vals = [], []
    for _ in range(TOP_K):
        m = jnp.max(work, axis=0, keepdims=True)
        idx = jnp.min(jnp.where(work == m, erow, float(N_EXPERTS)), axis=0, keepdims=True)
        sel = erow == idx
        sel_all = jnp.where(sel, 1.0, sel_all)
        work = jnp.where(sel, -jnp.inf, work)
        sels.append(sel)
        vals.append(m)
    exps = [jnp.exp(v - vals[0]) for v in vals]
    denom = exps[0] + exps[1] + exps[2] + exps[3]

    lrank = _dot(sel_all.astype(BF16), tri_ref[...])
    n_e = jnp.sum(sel_all, axis=1, keepdims=True)
    n_al = jnp.broadcast_to(jnp.floor((n_e + (SEG - 1)) * (1.0 / SEG)) * (SEG * live), (N_EXPERTS, LANES))
    lstart = _dot(trie_ref[...], n_al.astype(BF16))
    old_cnt = carry[...]
    new_cnt = old_cnt + n_al
    carry[...] = new_cnt
    tile_row = jnp.concatenate([n_al, old_cnt, lstart, new_cnt], axis=0).T[0:1, :]
    tm_ref[0] = tile_row.astype(I32)

    local_row = lstart[:, 0:1] + lrank
    out_row = lax.broadcasted_iota(I32, (2 * TOP_K, tm), 0)
    meta = jnp.zeros((2 * TOP_K, tm), F32)
    for k in range(TOP_K):
        dk = jnp.sum(jnp.where(sels[k], local_row, 0.0), axis=0, keepdims=True)
        meta = jnp.where(out_row == k, dk, meta)
        meta = jnp.where(out_row == TOP_K + k, exps[k] / denom, meta)
    mt_ref[...] = meta


def _outproj(yf, yb, sz, u, v, x2d, p):
    t = x2d.shape[0]
    tm = TM_OUT
    row = lambda i: (i, 0)
    nt = t // tm
    out_shape = (
        jax.ShapeDtypeStruct((t, D_MODEL), F32),
        jax.ShapeDtypeStruct((t, D_MODEL), BF16),
        jax.ShapeDtypeStruct((nt * 2 * TOP_K, tm), F32),
        jax.ShapeDtypeStruct((nt, 1, LANES), I32),
    )
    row = lambda i: (jnp.minimum(i, nt - 1), 0)
    routed = lambda i: jnp.maximum(i - 1, 0)
    bf_row = pl.BlockSpec((tm, D_MODEL), row)
    return pl.pallas_call(
        _outproj_kernel,
        out_shape=out_shape,
        grid=(nt + 1,),
        in_specs=[
            bf_row, bf_row, bf_row, bf_row, bf_row, bf_row,
            _const_spec((1, SSD_WIDTH)),
            _const_spec((GMLP_HEADS, CHUNK, CHUNK)),
            _const_spec((GMLP_HEADS, CHUNK, GMLP_HEAD_DIM)),
            _const_spec((SSD_WIDTH + GMLP_WIDTH, D_MODEL)),
            _const_spec((1, D_MODEL)),
            _const_spec((D_MODEL, LANES)),
            _const_spec((1, LANES)),
            _const_spec((tm, tm)),
            _const_spec((N_EXPERTS, N_EXPERTS)),
        ],
        out_specs=(
            pl.BlockSpec((tm, D_MODEL), row),
            pl.BlockSpec((tm, D_MODEL), row),
            pl.BlockSpec((2 * TOP_K, tm), lambda i: (routed(i), 0)),
            pl.BlockSpec((1, 1, LANES), lambda i: (routed(i), 0, 0)),
        ),
        scratch_shapes=[pltpu.VMEM((tm, GMLP_WIDTH), BF16), pltpu.VMEM((N_EXPERTS, LANES), F32),
                        pltpu.VMEM((N_EXPERTS, tm), F32)],
        compiler_params=pltpu.CompilerParams(
            dimension_semantics=("arbitrary",), vmem_limit_bytes=VMEM_LIMIT),
        name="outproj",
    )(yf, yb, sz, u, v, x2d, p["gs"], p["w_sp"], p["b_sp"], p["w_out"], p["g2"], p["w_r"], p["b_r"],
      p["tok_before"], p["expert_before"])


def _copy_segment(make_copy, n, src_row, dst_row):
    big, mid, small = COPY_ROWS

    def body(c, carry_):
        make_copy(pl.multiple_of(src_row + c * big, SEG), pl.multiple_of(dst_row + c * big, SEG), big).start()
        return carry_
    nbig = n // big
    lax.fori_loop(0, nbig, body, 0)
    done = nbig * big

    @pl.when((n & mid) != 0)
    def _():
        make_copy(pl.multiple_of(src_row + done, SEG), pl.multiple_of(dst_row + done, SEG), mid).start()

    done2 = done + (n & mid)

    @pl.when((n & small) != 0)
    def _():
        make_copy(pl.multiple_of(src_row + done2, SEG), pl.multiple_of(dst_row + done2, SEG), small).start()


def _wait_rows(make_copy, nrows, limit=R_LOCAL):
    w = 1 << (limit.bit_length() - 1)
    while w >= SEG:
        @pl.when((nrows & w) != 0)
        def _(w=w):
            make_copy(0, 0, w).wait()
        w //= 2


def _onehot_rows(mt_ref, blk, values):
    tm = mt_ref.shape[1]
    riota = lax.broadcasted_iota(I32, (SORT_BLK, tm), 0).astype(F32).astype(BF16)
    out = jnp.zeros((SORT_BLK, tm), BF16)
    for k in reversed(range(TOP_K)):
        off = (mt_ref[k:k + 1, :] - float(blk * SORT_BLK)).astype(BF16)
        out = jnp.where(riota == off, values[k], out)
    return out


def _dispatch_kernel(ps_ref, cn_ref, nu_ref, dst_ref, h2_ref, mt_ref, xs_ref, sbuf, zbuf, sem):
    i = pl.program_id(0)
    nt = pl.num_programs(0)
    tm = h2_ref.shape[0]
    groups_per_blk = SORT_BLK // SEG

    def wait_tile():
        pltpu.make_async_copy(sbuf, xs_ref.at[pl.ds(0, R_LOCAL)], sem).wait()

    h2 = h2_ref[...]
    ones = [jnp.ones((1, tm), BF16)] * TOP_K
    for blk in range(R_LOCAL // SORT_BLK):
        s = _dot(_onehot_rows(mt_ref, blk, ones), h2)
        lo_bits = lax.bitcast_convert_type(s[:, 0:HALF], U32)
        hi_bits = lax.bitcast_convert_type(s[:, HALF:], U32)
        packed = (lo_bits >> 16) | hi_bits
        if blk == 0:
            @pl.when(i > 0)
            def _():
                wait_tile()
        sbuf[blk * SORT_BLK:(blk + 1) * SORT_BLK, :] = packed
        for g in range(blk * groups_per_blk, (blk + 1) * groups_per_blk):
            pltpu.make_async_copy(sbuf.at[pl.ds(g * SEG, SEG)],
                                  xs_ref.at[pl.ds(pl.multiple_of(dst_ref[g], SEG), SEG)], sem).start()

    @pl.when(i == nt - 1)
    def _():
        wait_tile()
        zbuf[...] = jnp.zeros_like(zbuf)

        def zero_copy(src_row, dst_row, rows):
            return pltpu.make_async_copy(zbuf.at[pl.ds(src_row, rows)], xs_ref.at[pl.ds(dst_row, rows)], sem)

        bm = zbuf.shape[0]

        def tail_rows(e):
            return (cn_ref[e] + (bm - 1)) // bm * bm - cn_ref[e]

        def start_tail(e, c):
            _copy_segment(zero_copy, tail_rows(e), 0, ps_ref[e] + cn_ref[e])
            return c
        lax.fori_loop(0, N_EXPERTS, start_tail, 0)

        def start_block(b, c):
            zero_copy(0, pl.multiple_of(b * bm, bm), bm).start()
            return c
        n_rows_total = xs_ref.shape[0]
        lax.fori_loop(nu_ref[0], n_rows_total // bm, start_block, 0)

        def wait_tail(e, c):
            _wait_rows(zero_copy, tail_rows(e), limit=bm - SEG)
            return c
        lax.fori_loop(0, N_EXPERTS, wait_tail, 0)

        def wait_block(b, c):
            zero_copy(0, 0, bm).wait()
            return c
        lax.fori_loop(nu_ref[0], n_rows_total // bm, wait_block, 0)


def _dispatch(pad_start, counts, n_used, dst_table, h2, mt, rows, bm):
    t = h2.shape[0]
    tm = TM_OUT
    grid_spec = pltpu.PrefetchScalarGridSpec(
        num_scalar_prefetch=3,
        grid=(t // tm,),
        in_specs=[
            pl.BlockSpec((ROW_TABLE,), lambda i, *_: (i,), memory_space=pltpu.SMEM),
            pl.BlockSpec((tm, D_MODEL), lambda i, *_: (i, 0)),
            pl.BlockSpec((2 * TOP_K, tm), lambda i, *_: (i, 0)),
        ],
        out_specs=pl.BlockSpec(memory_space=pl.ANY),
        scratch_shapes=[pltpu.VMEM((R_LOCAL, HALF), U32), pltpu.VMEM((bm, HALF), U32),
                        pltpu.SemaphoreType.DMA(())],
    )
    return pl.pallas_call(
        _dispatch_kernel,
        out_shape=jax.ShapeDtypeStruct((rows, HALF), U32),
        grid_spec=grid_spec,
        compiler_params=pltpu.CompilerParams(
            dimension_semantics=("arbitrary",), vmem_limit_bytes=VMEM_LIMIT, has_side_effects=True),
        name="dispatch",
    )(pad_start, counts, n_used, dst_table, h2, mt)


def _experts_kernel(be_ref, nu_ref, xs_ref, wgu_ref, bgu_ref, wd_ref, bd_ref, ys_ref, wgu_bf, wd_bf):
    i = pl.program_id(0)
    used = i < nu_ref[0]

    @pl.when(used & ((i == 0) | (be_ref[i] != be_ref[jnp.maximum(i - 1, 0)])))
    def _():
        wgu_bf[...] = wgu_ref[0].astype(BF16)
        wd_bf[...] = wd_ref[0].astype(BF16)

    @pl.when(used)
    def _():
        lo, hi = _unpack_bf16_pair(xs_ref[...])
        xb = jnp.concatenate([lo.astype(BF16), hi.astype(BF16)], axis=1)
        gu = _dot(xb, wgu_bf[...]) + bgu_ref[0]
        gate = jnp.minimum(gu[:, 0:D_FF], SWIGLU_LIMIT)
        up = jnp.clip(gu[:, D_FF:], -SWIGLU_LIMIT, SWIGLU_LIMIT)
        act = (up + 1.0) * (gate * jax.nn.sigmoid(gate * SWIGLU_ALPHA))
        y = _dot(act.astype(BF16), wd_bf[...]) + bd_ref[0]
        ys_ref[...] = _pack_bf16_pair(y[:, 0:HALF], y[:, HALF:])

    @pl.when(i >= nu_ref[0])
    def _():
        ys_ref[...] = jnp.zeros_like(ys_ref)


def _experts(block_e, n_used, xs, p, bm):
    rows = xs.shape[0]
    nb = rows // bm
    grid_spec = pltpu.PrefetchScalarGridSpec(
        num_scalar_prefetch=2,
        grid=(nb,),
        in_specs=[
            pl.BlockSpec((bm, HALF), lambda i, be, *_: (i, 0)),
            pl.BlockSpec((1, D_MODEL, 2 * D_FF), lambda i, be, *_: (be[i], 0, 0)),
            pl.BlockSpec((1, 1, 2 * D_FF), lambda i, be, *_: (be[i], 0, 0)),
            pl.BlockSpec((1, D_FF, D_MODEL), lambda i, be, *_: (be[i], 0, 0)),
            pl.BlockSpec((1, 1, D_MODEL), lambda i, be, *_: (be[i], 0, 0)),
        ],
        out_specs=pl.BlockSpec((bm, HALF), lambda i, be, *_: (i, 0)),
        scratch_shapes=[pltpu.VMEM((D_MODEL, 2 * D_FF), BF16), pltpu.VMEM((D_FF, D_MODEL), BF16)],
    )
    return pl.pallas_call(
        _experts_kernel,
        out_shape=jax.ShapeDtypeStruct((rows, HALF), U32),
        grid_spec=grid_spec,
        compiler_params=pltpu.CompilerParams(
            dimension_semantics=("arbitrary",), vmem_limit_bytes=VMEM_LIMIT),
        name="experts",
    )(block_e, n_used, xs, p["w_gu"], p["b_gu"], p["w_d"], p["b_d"])


def _combine_kernel(cur_ref, nxt_ref, mt_ref, x1_ref, gf_ref, ys_ref, o_ref,
                    ybuf, gt_buf, ylo_buf, yhi_buf, sems):
    i = pl.program_id(0)
    nt = pl.num_programs(0)

    def issue(table_ref, slot):
        for g in range(R_LOCAL // SEG):
            pltpu.make_async_copy(ys_ref.at[pl.ds(pl.multiple_of(table_ref[g], SEG), SEG)],
                                  ybuf.at[slot, pl.ds(g * SEG, SEG)], sems.at[slot]).start()

    def wait_slot(slot):
        pltpu.make_async_copy(ys_ref.at[pl.ds(0, R_LOCAL)], ybuf.at[slot], sems.at[slot]).wait()

    @pl.when(i == 0)
    def _():
        issue(cur_ref, 0)

    slot = i % 2
    issue(nxt_ref, 1 - slot)

    gates = [mt_ref[TOP_K + k:TOP_K + k + 1, :].astype(BF16) for k in range(TOP_K)]
    for blk in range(R_LOCAL // SORT_BLK):
        rows = slice(blk * SORT_BLK, (blk + 1) * SORT_BLK)
        gt_buf[rows, :] = _onehot_rows(mt_ref, blk, gates)
    wait_slot(slot)
    for blk in range(R_LOCAL // SORT_BLK):
        rows = slice(blk * SORT_BLK, (blk + 1) * SORT_BLK)
        lo, hi = _unpack_bf16_pair(ybuf[slot, rows, :])
        ylo_buf[rows, :] = lo.astype(BF16)
        yhi_buf[rows, :] = hi.astype(BF16)
    contract_rows = (((0,), (0,)), ((), ()))
    g_t = gt_buf[...]
    x1 = x1_ref[...]
    acc = jnp.concatenate(
        [x1[:, 0:HALF] + lax.dot_general(g_t, ylo_buf[...], contract_rows, preferred_element_type=F32),
         x1[:, HALF:] + lax.dot_general(g_t, yhi_buf[...], contract_rows, preferred_element_type=F32)], axis=1)
    ms = jnp.mean(acc * acc, axis=-1, keepdims=True)
    o_ref[...] = acc * lax.rsqrt(ms + RMS_EPS) * gf_ref[...]

    @pl.when(i == nt - 1)
    def _():
        wait_slot(1 - slot)


def _combine(src_table, mt, x1, ys, p):
    t = x1.shape[0]
    tm = TM_OUT
    nt = t // tm
    grid_spec = pltpu.PrefetchScalarGridSpec(
        num_scalar_prefetch=0,
        grid=(nt,),
        in_specs=[
            pl.BlockSpec((ROW_TABLE,), lambda i: (i,), memory_space=pltpu.SMEM),
            pl.BlockSpec((ROW_TABLE,), lambda i: (jnp.minimum(i + 1, nt - 1),), memory_space=pltpu.SMEM),
            pl.BlockSpec((2 * TOP_K, tm), lambda i: (i, 0)),
            pl.BlockSpec((tm, D_MODEL), lambda i: (i, 0)),
            pl.BlockSpec((1, D_MODEL), lambda i: (0, 0)),
            pl.BlockSpec(memory_space=pl.ANY),
        ],
        out_specs=pl.BlockSpec((tm, D_MODEL), lambda i: (i, 0)),
        scratch_shapes=[pltpu.VMEM((2, R_LOCAL, HALF), U32), pltpu.VMEM((R_LOCAL, tm), BF16),
                        pltpu.VMEM((R_LOCAL, HALF), BF16), pltpu.VMEM((R_LOCAL, HALF), BF16),
                        pltpu.SemaphoreType.DMA((2,))],
    )
    return pl.pallas_call(
        _combine_kernel,
        out_shape=jax.ShapeDtypeStruct((t, D_MODEL), F32),
        grid_spec=grid_spec,
        compiler_params=pltpu.CompilerParams(
            dimension_semantics=("arbitrary",), vmem_limit_bytes=VMEM_LIMIT),
        name="combine",
    )(src_table, src_table, mt, x1, p["gf"], ys)


def _prepare_params(norm1_g, w_in, conv_w, conv_b, dt_bias_f, dt_bias_b, a_log_f, a_log_b, d_skip,
                    ssd_norm_g, gmlp_ln_g, gmlp_ln_b, w_spatial, b_spatial, w_out, norm2_g,
                    router_w, router_b, w_gate_up, b_gate_up, w_down, b_down, final_g):
    z_end = SSD_WIDTH
    xbc_end = z_end + CONV_CH
    dt_end = xbc_end + 2 * HEADS
    u_end = dt_end + GMLP_WIDTH
    w = w_in[0]
    pad_lanes = lambda a, fill=0.0: jnp.pad(a, ((0, 0), (0, LANES - a.shape[1])), constant_values=fill)
    head_lane = jnp.arange(LANES)[:, None]
    chan_head = (jnp.arange(SSD_WIDTH) // HEAD_DIM)[None, :]
    ar = jnp.arange(CHUNK)
    art = jnp.arange(TM_OUT)
    are = jnp.arange(N_EXPERTS)
    return {
        "g1": norm1_g[0][None, :],
        "w_z": w[:, :z_end].astype(BF16),
        "w_xbc": w[:, z_end:xbc_end].astype(BF16),
        "w_dt": pad_lanes(w[:, xbc_end:dt_end]).astype(BF16),
        "w_u": w[:, dt_end:u_end].astype(BF16),
        "w_v": w[:, u_end:].astype(BF16),
        "conv_w": jnp.pad(conv_w[0], ((0, 8 - CONV_WIDTH), (0, 0))),
        "conv_b": conv_b[0][None, :],
        "dt_bias": pad_lanes(jnp.concatenate([dt_bias_f[0], dt_bias_b[0]])[None, :]),
        "ln_g": gmlp_ln_g[0][None, :],
        "ln_b": gmlp_ln_b[0][None, :],
        "a_log": pad_lanes(jnp.concatenate([a_log_f[0], a_log_b[0]])[None, :]),
        "tri_lo": (ar[:, None] >= ar[None, :]).astype(BF16),
        "tri_up": (ar[:, None] <= ar[None, :]).astype(BF16),
        "rexp_f": jnp.tile((head_lane == chan_head).astype(BF16), (2, 1)),
        "rexp_b": jnp.tile((head_lane == chan_head + HEADS).astype(BF16), (2, 1)),
        "d_skip": jnp.repeat(d_skip[0], HEAD_DIM)[None, :],
        "gs": ssd_norm_g[0][None, :],
        "w_sp": w_spatial[0].astype(BF16),
        "b_sp": jnp.broadcast_to(b_spatial[0][:, :, None], (GMLP_HEADS, CHUNK, GMLP_HEAD_DIM)),
        "w_out": w_out[0].astype(BF16),
        "g2": norm2_g[0][None, :],
        "w_r": pad_lanes(router_w[0]).astype(BF16),
        "b_r": pad_lanes(router_b[0][None, :], NEG_BIG),
        "tok_before": (art[:, None] < art[None, :]).astype(BF16),
        "expert_before": (are[None, :] < are[:, None]).astype(BF16),
        "w_gu": w_gate_up[0],
        "b_gu": b_gate_up[0][:, None, :],
        "w_d": w_down[0],
        "b_d": b_down[0][:, None, :],
        "gf": final_g[None, :],
    }


def _encoder(x, p):
    batch, seq_len, _ = x.shape
    t = batch * seq_len
    x2d = x.reshape(t, D_MODEL)
    sz, xbc, dt, u, v = _inproj(x2d, seq_len, p)
    yf, yb = _ssd(xbc, dt, batch, seq_len, p)
    x1, h2, mt, tile_rows = _outproj(yf, yb, sz, u, v, x2d, p)
    tile_rows = tile_rows[:, 0, :]
    tn, tc, tl = (tile_rows[:, k * N_EXPERTS:(k + 1) * N_EXPERTS] for k in range(3))

    counts = tile_rows[-1, 3 * N_EXPERTS:]
    rows_per_expert = t * TOP_K // N_EXPERTS
    bm = next((b for b in BM_CHOICES if rows_per_expert >= BM_MIN_BLOCKS * b), BM_CHOICES[-1])
    spare_blocks = -(-R_LOCAL // bm)
    padded = (counts + bm - 1) // bm * bm
    pad_end = jnp.cumsum(padded)
    pad_start = (pad_end - padded).astype(I32)
    n_tiles = t // TM_OUT
    n_blocks = -(-(t * TOP_K + n_tiles * N_EXPERTS * (SEG - 1) + N_EXPERTS * (bm - 1)) // bm)
    n_used = (pad_end[-1] // bm).astype(I32)
    blk_row = jnp.minimum(jnp.arange(n_blocks + spare_blocks, dtype=I32), n_used - 1) * bm
    onehot_e = (pad_end[None, :] <= blk_row[:, None]).astype(I32)
    block_e = jnp.minimum(jnp.sum(onehot_e, axis=1), N_EXPERTS - 1).astype(I32)
    n_used = n_used.reshape(1)

    group_row = jnp.arange(R_LOCAL // SEG, dtype=I32) * SEG
    owner = jnp.sum(((tl + tn)[:, None, :] <= group_row[None, :, None]).astype(I32), axis=-1)
    is_owner = (owner[:, :, None] == jnp.arange(N_EXPERTS, dtype=I32)[None, None, :]).astype(I32)
    base = jnp.sum(is_owner * (pad_start[None, :] + tc - tl)[:, None, :], axis=-1)
    used = owner < N_EXPERTS
    pad_table = lambda a: jnp.pad(a, ((0, 0), (0, ROW_TABLE - a.shape[1]))).reshape(-1)
    dst_table = pad_table(jnp.where(used, base + group_row[None, :], n_blocks * bm + group_row[None, :]))
    src_table = pad_table(jnp.where(used, base + group_row[None, :], 0))

    xs = _dispatch(pad_start, counts, n_used, dst_table, h2, mt, (n_blocks + spare_blocks) * bm, bm)
    ys = _experts(block_e, n_used, xs, p, bm)
    y = _combine(src_table, mt, x1, ys, p)
    return y.reshape(batch, seq_len, D_MODEL)


def kernel(x_prompt, x_sample, norm1_g, w_in, conv_w, conv_b, dt_bias_f, dt_bias_b, a_log_f, a_log_b,
           d_skip, ssd_norm_g, gmlp_ln_g, gmlp_ln_b, w_spatial, b_spatial, w_out, norm2_g, router_w,
           router_b, w_gate_up, b_gate_up, w_down, b_down, final_g):
    p = _prepare_params(norm1_g, w_in, conv_w, conv_b, dt_bias_f, dt_bias_b, a_log_f, a_log_b, d_skip,
                        ssd_norm_g, gmlp_ln_g, gmlp_ln_b, w_spatial, b_spatial, w_out, norm2_g,
                        router_w, router_b, w_gate_up, b_gate_up, w_down, b_down, final_g)
    return (_encoder(x_prompt, p), _encoder(x_sample, p))
```

```python
import functools

import jax
import jax.numpy as jnp
from jax import lax
from jax.experimental import pallas as pl
from jax.experimental.pallas import tpu as pltpu

F32 = jnp.float32
BF16 = jnp.bfloat16
U32 = jnp.uint32
I32 = jnp.int32

D_MODEL = 1024
SSD_WIDTH = 1024
HEAD_DIM = 64
HEADS = 16
GROUPS = 2
HEADS_PER_GROUP = 8
STATE = 128
CHUNK = 128
CONV_WIDTH = 5
CONV_HALF = 2
CONV_CH = SSD_WIDTH + 2 * GROUPS * STATE
GMLP_WIDTH = 1024
GMLP_HEADS = 8
GMLP_HEAD_DIM = 128
N_EXPERTS = 32
TOP_K = 4
D_FF = 1024
SWIGLU_LIMIT = 7.0
SWIGLU_ALPHA = 1.702
RMS_EPS = 1e-5
LN_EPS = 1e-5

LANES = 128
HALO = 16
SSD_CHUNKS_PER_STEP = 8
TM_IN = 512
IN_COLS = 256
IN_ROWS = 512
TM_OUT = 512
BM_CHOICES = (1024, 512)
BM_MIN_BLOCKS = 2
SEG = 8
COPY_ROWS = (32, 16, 8)
SORT_BLK = 256
R_LOCAL = -(-(TM_OUT * TOP_K + N_EXPERTS * (SEG - 1)) // SORT_BLK) * SORT_BLK
ROW_TABLE = 1024
HALF = D_MODEL // 2
VMEM_LIMIT = 56 * 1024 * 1024
NEG_BIG = -1e30
LOG2_E = 1.4426950408889634


def _dot(a, b):
    return jnp.dot(a, b, preferred_element_type=F32)


def _gelu(x):
    return 0.5 * x * (1.0 + lax.erf(x * (2.0 ** -0.5)))


def _pack_bf16_pair(lo, hi):
    lo_bits = lax.bitcast_convert_type(lo.astype(BF16).astype(F32), U32)
    hi_bits = lax.bitcast_convert_type(hi.astype(BF16).astype(F32), U32)
    return (lo_bits >> 16) | hi_bits


def _unpack_bf16_pair(w):
    lo = lax.bitcast_convert_type(w << 16, F32)
    hi = lax.bitcast_convert_type(w & jnp.uint32(0xFFFF0000), F32)
    return lo, hi


def _const_spec(shape):
    nd = len(shape)
    return pl.BlockSpec(shape, lambda *_: (0,) * nd)


def _inproj_kernel(x_ref, xp_ref, xn_ref, g1_ref, wz_ref, wxbc_ref, wdt_ref, wu_ref, wv_ref,
                   cw_ref, cb_ref, dtb_ref, lng_ref, lnb_ref,
                   sz_ref, xbc_ref, dt_ref, u_ref, v_ref, buf0, buf1, buf2, gbuf, *, tiles_per_seq):
    i = pl.program_id(0)
    tm = x_ref.shape[0]
    g1 = g1_ref[...]
    row_blocks = [(r0, IN_ROWS) for r0 in range(0, tm, IN_ROWS)]

    def norm(xv):
        ms = jnp.mean(xv * xv, axis=-1, keepdims=True)
        return xv * lax.rsqrt(ms + RMS_EPS) * g1

    keep_prev = jnp.where(i % tiles_per_seq == 0, 0.0, 1.0)
    keep_next = jnp.where(i % tiles_per_seq == tiles_per_seq - 1, 0.0, 1.0)
    h = norm(x_ref[...]).astype(BF16)
    h_all = jnp.concatenate([(norm(xp_ref[...]) * keep_prev).astype(BF16), h,
                             (norm(xn_ref[...]) * keep_next).astype(BF16)], axis=0)

    def silu_to(out_ref):
        def epilogue(buf, c0, wd):
            for r0, nr in row_blocks:
                r = buf[r0:r0 + nr, 0:wd]
                out_ref[r0:r0 + nr, c0:c0 + wd] = (r * jax.nn.sigmoid(r)).astype(BF16)
        return epilogue

    def conv_silu(buf, c0, wd):
        for r0, nr in row_blocks:
            acc = jnp.broadcast_to(cb_ref[:, c0:c0 + wd], (nr, wd))
            for k in range(CONV_WIDTH):
                acc = acc + cw_ref[k:k + 1, c0:c0 + wd] * buf[pl.ds(r0 + HALO - CONV_HALF + k, nr), 0:wd]
            xbc_ref[r0:r0 + nr, c0:c0 + wd] = (acc * jax.nn.sigmoid(acc)).astype(BF16)

    def softplus_dt(buf, c0, wd):
        dt_ref[...] = jax.nn.softplus(buf[0:tm, 0:wd] + dtb_ref[...])

    def gelu_u(buf, c0, wd):
        for r0, nr in row_blocks:
            u_ref[r0:r0 + nr, c0:c0 + wd] = _gelu(buf[r0:r0 + nr, 0:wd]).astype(BF16)

    sums = [jnp.zeros((nr, LANES), F32) for _, nr in row_blocks]

    def gelu_v(buf, c0, wd):
        for b, (r0, nr) in enumerate(row_blocks):
            g = _gelu(buf[r0:r0 + nr, 0:wd])
            gbuf[r0:r0 + nr, c0:c0 + wd] = g
            for l0 in range(0, wd, LANES):
                sums[b] = sums[b] + g[:, l0:l0 + LANES]

    heavy = [(h_all, wxbc_ref, c0, IN_COLS, conv_silu) for c0 in range(0, CONV_CH, IN_COLS)]
    light = [(h, wz_ref, c0, IN_COLS, silu_to(sz_ref)) for c0 in range(0, SSD_WIDTH, IN_COLS)]
    light += [(h, wu_ref, c0, IN_COLS, gelu_u) for c0 in range(0, GMLP_WIDTH, IN_COLS)]
    light += [(h, wv_ref, c0, IN_COLS, gelu_v) for c0 in range(0, GMLP_WIDTH, IN_COLS)]
    light.append((h, wdt_ref, 0, LANES, softplus_dt))
    stages = []
    while heavy or light:
        if heavy:
            stages.append(heavy.pop(0))
        if light:
            stages.append(light.pop(0))
    bufs = (buf0, buf1, buf2)
    ahead = len(bufs) - 1

    def matmul(j):
        lhs, w_ref, c0, wd, _ = stages[j]
        bufs[j % len(bufs)][0:lhs.shape[0], 0:wd] = _dot(lhs, w_ref[:, c0:c0 + wd])

    for j in range(ahead):
        matmul(j)
    for j, (_, _, c0, wd, epilogue) in enumerate(stages):
        if j + ahead < len(stages):
            matmul(j + ahead)
        epilogue(bufs[j % len(bufs)], c0, wd)

    inv_n = 1.0 / GMLP_WIDTH
    for b, (r0, nr) in enumerate(row_blocks):
        mu = jnp.sum(sums[b], axis=-1, keepdims=True) * inv_n
        sq = jnp.zeros((nr, LANES), F32)
        for l0 in range(0, GMLP_WIDTH, LANES):
            d = gbuf[r0:r0 + nr, l0:l0 + LANES] - mu
            sq = sq + d * d
        scale = lax.rsqrt(jnp.sum(sq, axis=-1, keepdims=True) * inv_n + LN_EPS)
        for c0 in range(0, GMLP_WIDTH, IN_COLS):
            g = gbuf[r0:r0 + nr, c0:c0 + IN_COLS]
            v_ref[r0:r0 + nr, c0:c0 + IN_COLS] = (
                (g - mu) * scale * lng_ref[:, c0:c0 + IN_COLS] + lnb_ref[:, c0:c0 + IN_COLS]).astype(BF16)


def _inproj(x2d, seq_len, p):
    t = x2d.shape[0]
    tm = TM_IN
    tiles_per_seq = seq_len // tm
    nh = tm // HALO
    last_halo = t // HALO - 1
    kern = functools.partial(_inproj_kernel, tiles_per_seq=tiles_per_seq)
    row = lambda i: (i, 0)
    out_shape = (
        jax.ShapeDtypeStruct((t, SSD_WIDTH), BF16),
        jax.ShapeDtypeStruct((t, CONV_CH), BF16),
        jax.ShapeDtypeStruct((t, LANES), F32),
        jax.ShapeDtypeStruct((t, GMLP_WIDTH), BF16),
        jax.ShapeDtypeStruct((t, GMLP_WIDTH), BF16),
    )
    return pl.pallas_call(
        kern,
        out_shape=out_shape,
        grid=(t // tm,),
        in_specs=[
            pl.BlockSpec((tm, D_MODEL), row),
            pl.BlockSpec((HALO, D_MODEL), lambda i: (jnp.maximum(i * nh - 1, 0), 0)),
            pl.BlockSpec((HALO, D_MODEL), lambda i: (jnp.minimum((i + 1) * nh, last_halo), 0)),
            _const_spec((1, D_MODEL)),
            _const_spec((D_MODEL, SSD_WIDTH)),
            _const_spec((D_MODEL, CONV_CH)),
            _const_spec((D_MODEL, LANES)),
            _const_spec((D_MODEL, GMLP_WIDTH)),
            _const_spec((D_MODEL, GMLP_WIDTH)),
            _const_spec((8, CONV_CH)),
            _const_spec((1, CONV_CH)),
            _const_spec((1, LANES)),
            _const_spec((1, GMLP_WIDTH)),
            _const_spec((1, GMLP_WIDTH)),
        ],
        out_specs=(
            pl.BlockSpec((tm, SSD_WIDTH), row),
            pl.BlockSpec((tm, CONV_CH), row),
            pl.BlockSpec((tm, LANES), row),
            pl.BlockSpec((tm, GMLP_WIDTH), row),
            pl.BlockSpec((tm, GMLP_WIDTH), row),
        ),
        scratch_shapes=[pltpu.VMEM((tm + 2 * HALO, IN_COLS), F32)] * 3 + [pltpu.VMEM((tm, GMLP_WIDTH), F32)],
        compiler_params=pltpu.CompilerParams(
            dimension_semantics=("arbitrary",), vmem_limit_bytes=VMEM_LIMIT),
        name="inproj",
    )(x2d, x2d, x2d, p["g1"], p["w_z"], p["w_xbc"], p["w_dt"], p["w_u"], p["w_v"],
      p["conv_w"], p["conv_b"], p["dt_bias"], p["ln_g"], p["ln_b"])


def _split2(q):
    hi = q.astype(BF16)
    lo = (q - hi.astype(F32)).astype(BF16)
    return hi, lo


def _ssd_role(xbc_ref, dt_ref, arow, tri_ref, rexp_ref, dskip_ref, s_ref, y_ref, *, r0, backward):
    lane0 = HEADS if backward else 0
    rows = slice(r0, r0 + CHUNK)
    dt_all = dt_ref[rows, :]
    adt = dt_all * arow
    a1 = adt.astype(BF16)
    r1 = adt - a1.astype(F32)
    a2 = r1.astype(BF16)
    a3 = (r1 - a2.astype(F32)).astype(BF16)
    acs3 = _dot(tri_ref[...], jnp.concatenate([a1, a2, a3], axis=1))
    acs = acs3[:, 0:LANES] + acs3[:, LANES:2 * LANES] + acs3[:, 2 * LANES:]
    edge = 0 if backward else CHUNK - 1
    tot = acs[edge:edge + 1, :]
    decay = jnp.exp(tot - acs)
    eacs = jnp.exp(acs)
    acs2 = acs * LOG2_E
    acs2_t = acs2.T

    hi, lo = _split2(jnp.concatenate([dt_all, decay, eacs], axis=0))
    expanded = _dot(jnp.concatenate([hi, lo], axis=1), rexp_ref[...])
    dt_e = expanded[0:CHUNK]
    decay_e = expanded[CHUNK:2 * CHUNK]
    eacs_e = expanded[2 * CHUNK:]
    etot_e = eacs_e[edge:edge + 1, :]

    xf = xbc_ref[rows, 0:SSD_WIDTH].astype(F32)
    xdt = xf * dt_e
    xdd = (xdt * decay_e).astype(BF16)
    xdt_b = xdt.astype(BF16)

    li = lax.broadcasted_iota(I32, (CHUNK, CHUNK), 0)
    si = lax.broadcasted_iota(I32, (CHUNK, CHUNK), 1)
    keep = (si >= li) if backward else (si <= li)
    left = si < HEAD_DIM
    zero_b = jnp.zeros((CHUNK, LANES), BF16)

    for g in range(GROUPS):
        b_g = xbc_ref[rows, SSD_WIDTH + g * STATE:SSD_WIDTH + (g + 1) * STATE]
        c_g = xbc_ref[rows, SSD_WIDTH + (GROUPS + g) * STATE:SSD_WIDTH + (GROUPS + g + 1) * STATE]
        b_t = b_g.astype(F32).T.astype(BF16)
        cb = _dot(c_g, b_t)
        gc0 = g * HEADS_PER_GROUP * HEAD_DIM
        gw = HEADS_PER_GROUP * HEAD_DIM
        s_old = s_ref[:, gc0:gc0 + gw]
        y_off = _dot(c_g, s_old.astype(BF16))
        s_new = _dot(b_t, xdd[:, gc0:gc0 + gw])
        s_ref[:, gc0:gc0 + gw] = s_old * etot_e[:, gc0:gc0 + gw] + s_new
        for pp in range(HEADS_PER_GROUP // 2):
            c0 = gc0 + pp * LANES
            ms = []
            for q in range(2):
                hl = lane0 + g * HEADS_PER_GROUP + 2 * pp + q
                seg = acs2[:, hl:hl + 1] - acs2_t[hl:hl + 1, :]
                lmat = jnp.exp2(jnp.where(keep, seg, -jnp.inf))
                ms.append((cb * lmat).astype(BF16))
            lhs = jnp.concatenate(ms, axis=1)
            xp = xdt_b[:, c0:c0 + LANES]
            rhs = jnp.concatenate([jnp.where(left, xp, zero_b), jnp.where(left, zero_b, xp)], axis=0)
            y = _dot(lhs, rhs) + y_off[:, pp * LANES:(pp + 1) * LANES] * eacs_e[:, c0:c0 + LANES]
            if not backward:
                y = y + xf[:, c0:c0 + LANES] * dskip_ref[:, c0:c0 + LANES]
            y_ref[rows, c0:c0 + LANES] = y.astype(BF16)


def _ssd_kernel(xf_ref, dtf_ref, xb_ref, dtb_ref, alog_ref, trilo_ref, triup_ref, rf_ref, rb_ref,
                dskip_ref, yf_ref, yb_ref, sf_ref, sb_ref):
    @pl.when(pl.program_id(1) == 0)
    def _():
        sf_ref[...] = jnp.zeros_like(sf_ref)
        sb_ref[...] = jnp.zeros_like(sb_ref)

    arow = -jnp.exp(alog_ref[...])
    n_sub = xf_ref.shape[0] // CHUNK
    for j in range(n_sub):
        _ssd_role(xf_ref, dtf_ref, arow, trilo_ref, rf_ref, dskip_ref, sf_ref, yf_ref,
                  r0=j * CHUNK, backward=False)
        _ssd_role(xb_ref, dtb_ref, arow, triup_ref, rb_ref, dskip_ref, sb_ref, yb_ref,
                  r0=(n_sub - 1 - j) * CHUNK, backward=True)


def _ssd(xbc, dt, batch, seq_len, p):
    t = xbc.shape[0]
    rows = SSD_CHUNKS_PER_STEP * CHUNK
    nc = seq_len // rows
    fwd = lambda b, c: (b * nc + c, 0)
    bwd = lambda b, c: (b * nc + nc - 1 - c, 0)
    return pl.pallas_call(
        _ssd_kernel,
        out_shape=(jax.ShapeDtypeStruct((t, SSD_WIDTH), BF16), jax.ShapeDtypeStruct((t, SSD_WIDTH), BF16)),
        grid=(batch, nc),
        in_specs=[
            pl.BlockSpec((rows, CONV_CH), fwd),
            pl.BlockSpec((rows, LANES), fwd),
            pl.BlockSpec((rows, CONV_CH), bwd),
            pl.BlockSpec((rows, LANES), bwd),
            _const_spec((1, LANES)),
            _const_spec((CHUNK, CHUNK)),
            _const_spec((CHUNK, CHUNK)),
            _const_spec((2 * LANES, SSD_WIDTH)),
            _const_spec((2 * LANES, SSD_WIDTH)),
            _const_spec((1, SSD_WIDTH)),
        ],
        out_specs=(pl.BlockSpec((rows, SSD_WIDTH), fwd), pl.BlockSpec((rows, SSD_WIDTH), bwd)),
        scratch_shapes=[pltpu.VMEM((STATE, SSD_WIDTH), F32), pltpu.VMEM((STATE, SSD_WIDTH), F32)],
        compiler_params=pltpu.CompilerParams(
            dimension_semantics=("arbitrary", "arbitrary"), vmem_limit_bytes=VMEM_LIMIT),
        name="ssd",
    )(xbc, dt, xbc, dt, p["a_log"], p["tri_lo"], p["tri_up"], p["rexp_f"], p["rexp_b"], p["d_skip"])


def _outproj_kernel(yf_ref, yb_ref, sz_ref, u_ref, v_ref, x_ref, gs_ref, wsp_ref, bsp_ref, wo_ref,
                    g2_ref, wr_ref, rb_ref, tri_ref, trie_ref,
                    x1_ref, h2_ref, mt_ref, tm_ref,
                    gm_buf, carry, logit_t):
    i = pl.program_id(0)
    tm = x_ref.shape[0]

    @pl.when(i == 0)
    def _():
        carry[...] = jnp.zeros_like(carry)
        logit_t[...] = jnp.zeros_like(logit_t)

    _route_tile(logit_t[...], jnp.where(i > 0, 1.0, 0.0), tri_ref, trie_ref, carry, mt_ref, tm_ref)

    gated = ((yf_ref[...] + yb_ref[...]) * sz_ref[...]).astype(F32)
    ms = jnp.mean(gated * gated, axis=-1, keepdims=True)
    ssd = (gated * lax.rsqrt(ms + RMS_EPS) * gs_ref[...]).astype(BF16)

    for c in range(tm // CHUNK):
        r0 = c * CHUNK
        for hh in range(GMLP_HEADS):
            c0 = hh * GMLP_HEAD_DIM
            sv = _dot(wsp_ref[hh], v_ref[r0:r0 + CHUNK, c0:c0 + GMLP_HEAD_DIM]) + bsp_ref[hh]
            gm_buf[r0:r0 + CHUNK, c0:c0 + GMLP_HEAD_DIM] = (
                u_ref[r0:r0 + CHUNK, c0:c0 + GMLP_HEAD_DIM].astype(F32) * sv).astype(BF16)

    x1 = x_ref[...] + _dot(jnp.concatenate([ssd, gm_buf[...]], axis=1), wo_ref[...])
    x1_ref[...] = x1

    ms2 = jnp.mean(x1 * x1, axis=-1, keepdims=True)
    h2 = (x1 * lax.rsqrt(ms2 + RMS_EPS) * g2_ref[...]).astype(BF16)
    h2_ref[...] = h2

    logits = _dot(h2, wr_ref[...]) + rb_ref[...]
    logit_t[...] = logits.T[0:N_EXPERTS, :]


def _route_tile(work, live, tri_ref, trie_ref, carry, mt_ref, tm_ref):
    tm = work.shape[1]
    erow = lax.broadcasted_iota(I32, (N_EXPERTS, tm), 0).astype(F32)
    sel_all = jnp.zeros((N_EXPERTS, tm), F32)
    sels, vals = [], []
    for _ in range(TOP_K):
        m = jnp.max(work, axis=0, keepdims=True)
        idx = jnp.min(jnp.where(work == m, erow, float(N_EXPERTS)), axis=0, keepdims=True)
        sel = erow == idx
        sel_all = jnp.where(sel, 1.0, sel_all)
        work = jnp.where(sel, -jnp.inf, work)
        sels.append(sel)
        vals.append(m)
    exps = [jnp.exp(v - vals[0]) for v in vals]
    denom = exps[0] + exps[1] + exps[2] + exps[3]

    lrank = _dot(sel_all.astype(BF16), tri_ref[...])
    n_e = jnp.sum(sel_all, axis=1, keepdims=True)
    n_al = jnp.broadcast_to(jnp.floor((n_e + (SEG - 1)) * (1.0 / SEG)) * (SEG * live), (N_EXPERTS, LANES))
    lstart = _dot(trie_ref[...], n_al.astype(BF16))
    old_cnt = carry[...]
    new_cnt = old_cnt + n_al
    carry[...] = new_cnt
    tile_row = jnp.concatenate([n_al, old_cnt, lstart, new_cnt], axis=0).T[0:1, :]
    tm_ref[0] = tile_row.astype(I32)

    local_row = lstart[:, 0:1] + lrank
    out_row = lax.broadcasted_iota(I32, (2 * TOP_K, tm), 0)
    meta = jnp.zeros((2 * TOP_K, tm), F32)
    for k in range(TOP_K):
        dk = jnp.sum(jnp.where(sels[k], local_row, 0.0), axis=0, keepdims=True)
        meta = jnp.where(out_row == k, dk, meta)
        meta = jnp.where(out_row == TOP_K + k, exps[k] / denom, meta)
    mt_ref[...] = meta


def _outproj(yf, yb, sz, u, v, x2d, p):
    t = x2d.shape[0]
    tm = TM_OUT
    row = lambda i: (i, 0)
    nt = t // tm
    out_shape = (
        jax.ShapeDtypeStruct((t, D_MODEL), F32),
        jax.ShapeDtypeStruct((t, D_MODEL), BF16),
        jax.ShapeDtypeStruct((nt * 2 * TOP_K, tm), F32),
        jax.ShapeDtypeStruct((nt, 1, LANES), I32),
    )
    row = lambda i: (jnp.minimum(i, nt - 1), 0)
    routed = lambda i: jnp.maximum(i - 1, 0)
    bf_row = pl.BlockSpec((tm, D_MODEL), row)
    return pl.pallas_call(
        _outproj_kernel,
        out_shape=out_shape,
        grid=(nt + 1,),
        in_specs=[
            bf_row, bf_row, bf_row, bf_row, bf_row, bf_row,
            _const_spec((1, SSD_WIDTH)),
            _const_spec((GMLP_HEADS, CHUNK, CHUNK)),
            _const_spec((GMLP_HEADS, CHUNK, GMLP_HEAD_DIM)),
            _const_spec((SSD_WIDTH + GMLP_WIDTH, D_MODEL)),
            _const_spec((1, D_MODEL)),
            _const_spec((D_MODEL, LANES)),
            _const_spec((1, LANES)),
            _const_spec((tm, tm)),
            _const_spec((N_EXPERTS, N_EXPERTS)),
        ],
        out_specs=(
            pl.BlockSpec((tm, D_MODEL), row),
            pl.BlockSpec((tm, D_MODEL), row),
            pl.BlockSpec((2 * TOP_K, tm), lambda i: (routed(i), 0)),
            pl.BlockSpec((1, 1, LANES), lambda i: (routed(i), 0, 0)),
        ),
        scratch_shapes=[pltpu.VMEM((tm, GMLP_WIDTH), BF16), pltpu.VMEM((N_EXPERTS, LANES), F32),
                        pltpu.VMEM((N_EXPERTS, tm), F32)],
        compiler_params=pltpu.CompilerParams(
            dimension_semantics=("arbitrary",), vmem_limit_bytes=VMEM_LIMIT),
        name="outproj",
    )(yf, yb, sz, u, v, x2d, p["gs"], p["w_sp"], p["b_sp"], p["w_out"], p["g2"], p["w_r"], p["b_r"],
      p["tok_before"], p["expert_before"])


def _copy_segment(make_copy, n, src_row, dst_row):
    big, mid, small = COPY_ROWS

    def body(c, carry_):
        make_copy(pl.multiple_of(src_row + c * big, SEG), pl.multiple_of(dst_row + c * big, SEG), big).start()
        return carry_
    nbig = n // big
    lax.fori_loop(0, nbig, body, 0)
    done = nbig * big

    @pl.when((n & mid) != 0)
    def _():
        make_copy(pl.multiple_of(src_row + done, SEG), pl.multiple_of(dst_row + done, SEG), mid).start()

    done2 = done + (n & mid)

    @pl.when((n & small) != 0)
    def _():
        make_copy(pl.multiple_of(src_row + done2, SEG), pl.multiple_of(dst_row + done2, SEG), small).start()


def _wait_rows(make_copy, nrows, limit=R_LOCAL):
    w = 1 << (limit.bit_length() - 1)
    while w >= SEG:
        @pl.when((nrows & w) != 0)
        def _(w=w):
            make_copy(0, 0, w).wait()
        w //= 2


def _onehot_rows(mt_ref, blk, values):
    tm = mt_ref.shape[1]
    riota = lax.broadcasted_iota(I32, (SORT_BLK, tm), 0).astype(F32).astype(BF16)
    out = jnp.zeros((SORT_BLK, tm), BF16)
    for k in reversed(range(TOP_K)):
        off = (mt_ref[k:k + 1, :] - float(blk * SORT_BLK)).astype(BF16)
        out = jnp.where(riota == off, values[k], out)
    return out


def _dispatch_kernel(ps_ref, cn_ref, nu_ref, dst_ref, h2_ref, mt_ref, xs_ref, sbuf, zbuf, sem):
    i = pl.program_id(0)
    nt = pl.num_programs(0)
    tm = h2_ref.shape[0]
    groups_per_blk = SORT_BLK // SEG

    def wait_tile():
        pltpu.make_async_copy(sbuf, xs_ref.at[pl.ds(0, R_LOCAL)], sem).wait()

    h2 = h2_ref[...]
    ones = [jnp.ones((1, tm), BF16)] * TOP_K
    for blk in range(R_LOCAL // SORT_BLK):
        s = _dot(_onehot_rows(mt_ref, blk, ones), h2)
        lo_bits = lax.bitcast_convert_type(s[:, 0:HALF], U32)
        hi_bits = lax.bitcast_convert_type(s[:, HALF:], U32)
        packed = (lo_bits >> 16) | hi_bits
        if blk == 0:
            @pl.when(i > 0)
            def _():
                wait_tile()
        sbuf[blk * SORT_BLK:(blk + 1) * SORT_BLK, :] = packed
        for g in range(blk * groups_per_blk, (blk + 1) * groups_per_blk):
            pltpu.make_async_copy(sbuf.at[pl.ds(g * SEG, SEG)],
                                  xs_ref.at[pl.ds(pl.multiple_of(dst_ref[g], SEG), SEG)], sem).start()

    @pl.when(i == nt - 1)
    def _():
        wait_tile()
        zbuf[...] = jnp.zeros_like(zbuf)

        def zero_copy(src_row, dst_row, rows):
            return pltpu.make_async_copy(zbuf.at[pl.ds(src_row, rows)], xs_ref.at[pl.ds(dst_row, rows)], sem)

        bm = zbuf.shape[0]

        def tail_rows(e):
            return (cn_ref[e] + (bm - 1)) // bm * bm - cn_ref[e]

        def start_tail(e, c):
            _copy_segment(zero_copy, tail_rows(e), 0, ps_ref[e] + cn_ref[e])
            return c
        lax.fori_loop(0, N_EXPERTS, start_tail, 0)

        def start_block(b, c):
            zero_copy(0, pl.multiple_of(b * bm, bm), bm).start()
            return c
        n_rows_total = xs_ref.shape[0]
        lax.fori_loop(nu_ref[0], n_rows_total // bm, start_block, 0)

        def wait_tail(e, c):
            _wait_rows(zero_copy, tail_rows(e), limit=bm - SEG)
            return c
        lax.fori_loop(0, N_EXPERTS, wait_tail, 0)

        def wait_block(b, c):
            zero_copy(0, 0, bm).wait()
            return c
        lax.fori_loop(nu_ref[0], n_rows_total // bm, wait_block, 0)


def _dispatch(pad_start, counts, n_used, dst_table, h2, mt, rows, bm):
    t = h2.shape[0]
    tm = TM_OUT
    grid_spec = pltpu.PrefetchScalarGridSpec(
        num_scalar_prefetch=3,
        grid=(t // tm,),
        in_specs=[
            pl.BlockSpec((ROW_TABLE,), lambda i, *_: (i,), memory_space=pltpu.SMEM),
            pl.BlockSpec((tm, D_MODEL), lambda i, *_: (i, 0)),
            pl.BlockSpec((2 * TOP_K, tm), lambda i, *_: (i, 0)),
        ],
        out_specs=pl.BlockSpec(memory_space=pl.ANY),
        scratch_shapes=[pltpu.VMEM((R_LOCAL, HALF), U32), pltpu.VMEM((bm, HALF), U32),
                        pltpu.SemaphoreType.DMA(())],
    )
    return pl.pallas_call(
        _dispatch_kernel,
        out_shape=jax.ShapeDtypeStruct((rows, HALF), U32),
        grid_spec=grid_spec,
        compiler_params=pltpu.CompilerParams(
            dimension_semantics=("arbitrary",), vmem_limit_bytes=VMEM_LIMIT, has_side_effects=True),
        name="dispatch",
    )(pad_start, counts, n_used, dst_table, h2, mt)


def _experts_kernel(be_ref, nu_ref, xs_ref, wgu_ref, bgu_ref, wd_ref, bd_ref, ys_ref, wgu_bf, wd_bf):
    i = pl.program_id(0)
    used = i < nu_ref[0]

    @pl.when(used & ((i == 0) | (be_ref[i] != be_ref[jnp.maximum(i - 1, 0)])))
    def _():
        wgu_bf[...] = wgu_ref[0].astype(BF16)
        wd_bf[...] = wd_ref[0].astype(BF16)

    @pl.when(used)
    def _():
        lo, hi = _unpack_bf16_pair(xs_ref[...])
        xb = jnp.concatenate([lo.astype(BF16), hi.astype(BF16)], axis=1)
        gu = _dot(xb, wgu_bf[...]) + bgu_ref[0]
        gate = jnp.minimum(gu[:, 0:D_FF], SWIGLU_LIMIT)
        up = jnp.clip(gu[:, D_FF:], -SWIGLU_LIMIT, SWIGLU_LIMIT)
        act = (up + 1.0) * (gate * jax.nn.sigmoid(gate * SWIGLU_ALPHA))
        y = _dot(act.astype(BF16), wd_bf[...]) + bd_ref[0]
        ys_ref[...] = _pack_bf16_pair(y[:, 0:HALF], y[:, HALF:])

    @pl.when(i >= nu_ref[0])
    def _():
        ys_ref[...] = jnp.zeros_like(ys_ref)


def _experts(block_e, n_used, xs, p, bm):
    rows = xs.shape[0]
    nb = rows // bm
    grid_spec = pltpu.PrefetchScalarGridSpec(
        num_scalar_prefetch=2,
        grid=(nb,),
        in_specs=[
            pl.BlockSpec((bm, HALF), lambda i, be, *_: (i, 0)),
            pl.BlockSpec((1, D_MODEL, 2 * D_FF), lambda i, be, *_: (be[i], 0, 0)),
            pl.BlockSpec((1, 1, 2 * D_FF), lambda i, be, *_: (be[i], 0, 0)),
            pl.BlockSpec((1, D_FF, D_MODEL), lambda i, be, *_: (be[i], 0, 0)),
            pl.BlockSpec((1, 1, D_MODEL), lambda i, be, *_: (be[i], 0, 0)),
        ],
        out_specs=pl.BlockSpec((bm, HALF), lambda i, be, *_: (i, 0)),
        scratch_shapes=[pltpu.VMEM((D_MODEL, 2 * D_FF), BF16), pltpu.VMEM((D_FF, D_MODEL), BF16)],
    )
    return pl.pallas_call(
        _experts_kernel,
        out_shape=jax.ShapeDtypeStruct((rows, HALF), U32),
        grid_spec=grid_spec,
        compiler_params=pltpu.CompilerParams(
            dimension_semantics=("arbitrary",), vmem_limit_bytes=VMEM_LIMIT),
        name="experts",
    )(block_e, n_used, xs, p["w_gu"], p["b_gu"], p["w_d"], p["b_d"])


def _combine_kernel(cur_ref, nxt_ref, mt_ref, x1_ref, gf_ref, ys_ref, o_ref,
                    ybuf, gt_buf, ylo_buf, yhi_buf, sems):
    i = pl.program_id(0)
    nt = pl.num_programs(0)

    def issue(table_ref, slot):
        for g in range(R_LOCAL // SEG):
            pltpu.make_async_copy(ys_ref.at[pl.ds(pl.multiple_of(table_ref[g], SEG), SEG)],
                                  ybuf.at[slot, pl.ds(g * SEG, SEG)], sems.at[slot]).start()

    def wait_slot(slot):
        pltpu.make_async_copy(ys_ref.at[pl.ds(0, R_LOCAL)], ybuf.at[slot], sems.at[slot]).wait()

    @pl.when(i == 0)
    def _():
        issue(cur_ref, 0)

    slot = i % 2
    issue(nxt_ref, 1 - slot)

    gates = [mt_ref[TOP_K + k:TOP_K + k + 1, :].astype(BF16) for k in range(TOP_K)]
    for blk in range(R_LOCAL // SORT_BLK):
        rows = slice(blk * SORT_BLK, (blk + 1) * SORT_BLK)
        gt_buf[rows, :] = _onehot_rows(mt_ref, blk, gates)
    wait_slot(slot)
    for blk in range(R_LOCAL // SORT_BLK):
        rows = slice(blk * SORT_BLK, (blk + 1) * SORT_BLK)
        lo, hi = _unpack_bf16_pair(ybuf[slot, rows, :])
        ylo_buf[rows, :] = lo.astype(BF16)
        yhi_buf[rows, :] = hi.astype(BF16)
    contract_rows = (((0,), (0,)), ((), ()))
    g_t = gt_buf[...]
    x1 = x1_ref[...]
    acc = jnp.concatenate(
        [x1[:, 0:HALF] + lax.dot_general(g_t, ylo_buf[...], contract_rows, preferred_element_type=F32),
         x1[:, HALF:] + lax.dot_general(g_t, yhi_buf[...], contract_rows, preferred_element_type=F32)], axis=1)
    ms = jnp.mean(acc * acc, axis=-1, keepdims=True)
    o_ref[...] = acc * lax.rsqrt(ms + RMS_EPS) * gf_ref[...]

    @pl.when(i == nt - 1)
    def _():
        wait_slot(1 - slot)


def _combine(src_table, mt, x1, ys, p):
    t = x1.shape[0]
    tm = TM_OUT
    nt = t // tm
    grid_spec = pltpu.PrefetchScalarGridSpec(
        num_scalar_prefetch=0,
        grid=(nt,),
        in_specs=[
            pl.BlockSpec((ROW_TABLE,), lambda i: (i,), memory_space=pltpu.SMEM),
            pl.BlockSpec((ROW_TABLE,), lambda i: (jnp.minimum(i + 1, nt - 1),), memory_space=pltpu.SMEM),
            pl.BlockSpec((2 * TOP_K, tm), lambda i: (i, 0)),
            pl.BlockSpec((tm, D_MODEL), lambda i: (i, 0)),
            pl.BlockSpec((1, D_MODEL), lambda i: (0, 0)),
            pl.BlockSpec(memory_space=pl.ANY),
        ],
        out_specs=pl.BlockSpec((tm, D_MODEL), lambda i: (i, 0)),
        scratch_shapes=[pltpu.VMEM((2, R_LOCAL, HALF), U32), pltpu.VMEM((R_LOCAL, tm), BF16),
                        pltpu.VMEM((R_LOCAL, HALF), BF16), pltpu.VMEM((R_LOCAL, HALF), BF16),
                        pltpu.SemaphoreType.DMA((2,))],
    )
    return pl.pallas_call(
        _combine_kernel,
        out_shape=jax.ShapeDtypeStruct((t, D_MODEL), F32),
        grid_spec=grid_spec,
        compiler_params=pltpu.CompilerParams(
            dimension_semantics=("arbitrary",), vmem_limit_bytes=VMEM_LIMIT),
        name="combine",
    )(src_table, src_table, mt, x1, p["gf"], ys)


def _prepare_params(norm1_g, w_in, conv_w, conv_b, dt_bias_f, dt_bias_b, a_log_f, a_log_b, d_skip,
                    ssd_norm_g, gmlp_ln_g, gmlp_ln_b, w_spatial, b_spatial, w_out, norm2_g,
                    router_w, router_b, w_gate_up, b_gate_up, w_down, b_down, final_g):
    z_end = SSD_WIDTH
    xbc_end = z_end + CONV_CH
    dt_end = xbc_end + 2 * HEADS
    u_end = dt_end + GMLP_WIDTH
    w = w_in[0]
    pad_lanes = lambda a, fill=0.0: jnp.pad(a, ((0, 0), (0, LANES - a.shape[1])), constant_values=fill)
    head_lane = jnp.arange(LANES)[:, None]
    chan_head = (jnp.arange(SSD_WIDTH) // HEAD_DIM)[None, :]
    ar = jnp.arange(CHUNK)
    art = jnp.arange(TM_OUT)
    are = jnp.arange(N_EXPERTS)
    return {
        "g1": norm1_g[0][None, :],
        "w_z": w[:, :z_end].astype(BF16),
        "w_xbc": w[:, z_end:xbc_end].astype(BF16),
        "w_dt": pad_lanes(w[:, xbc_end:dt_end]).astype(BF16),
        "w_u": w[:, dt_end:u_end].astype(BF16),
        "w_v": w[:, u_end:].astype(BF16),
        "conv_w": jnp.pad(conv_w[0], ((0, 8 - CONV_WIDTH), (0, 0))),
        "conv_b": conv_b[0][None, :],
        "dt_bias": pad_lanes(jnp.concatenate([dt_bias_f[0], dt_bias_b[0]])[None, :]),
        "ln_g": gmlp_ln_g[0][None, :],
        "ln_b": gmlp_ln_b[0][None, :],
        "a_log": pad_lanes(jnp.concatenate([a_log_f[0], a_log_b[0]])[None, :]),
        "tri_lo": (ar[:, None] >= ar[None, :]).astype(BF16),
        "tri_up": (ar[:, None] <= ar[None, :]).astype(BF16),
        "rexp_f": jnp.tile((head_lane == chan_head).astype(BF16), (2, 1)),
        "rexp_b": jnp.tile((head_lane == chan_head + HEADS).astype(BF16), (2, 1)),
        "d_skip": jnp.repeat(d_skip[0], HEAD_DIM)[None, :],
        "gs": ssd_norm_g[0][None, :],
        "w_sp": w_spatial[0].astype(BF16),
        "b_sp": jnp.broadcast_to(b_spatial[0][:, :, None], (GMLP_HEADS, CHUNK, GMLP_HEAD_DIM)),
        "w_out": w_out[0].astype(BF16),
        "g2": norm2_g[0][None, :],
        "w_r": pad_lanes(router_w[0]).astype(BF16),
        "b_r": pad_lanes(router_b[0][None, :], NEG_BIG),
        "tok_before": (art[:, None] < art[None, :]).astype(BF16),
        "expert_before": (are[None, :] < are[:, None]).astype(BF16),
        "w_gu": w_gate_up[0],
        "b_gu": b_gate_up[0][:, None, :],
        "w_d": w_down[0],
        "b_d": b_down[0][:, None, :],
        "gf": final_g[None, :],
    }


def _encoder(x, p):
    batch, seq_len, _ = x.shape
    t = batch * seq_len
    x2d = x.reshape(t, D_MODEL)
    sz, xbc, dt, u, v = _inproj(x2d, seq_len, p)
    yf, yb = _ssd(xbc, dt, batch, seq_len, p)
    x1, h2, mt, tile_rows = _outproj(yf, yb, sz, u, v, x2d, p)
    tile_rows = tile_rows[:, 0, :]
    tn, tc, tl = (tile_rows[:, k * N_EXPERTS:(k + 1) * N_EXPERTS] for k in range(3))

    counts = tile_rows[-1, 3 * N_EXPERTS:]
    rows_per_expert = t * TOP_K // N_EXPERTS
    bm = next((b for b in BM_CHOICES if rows_per_expert >= BM_MIN_BLOCKS * b), BM_CHOICES[-1])
    spare_blocks = -(-R_LOCAL // bm)
    padded = (counts + bm - 1) // bm * bm
    pad_end = jnp.cumsum(padded)
    pad_start = (pad_end - padded).astype(I32)
    n_tiles = t // TM_OUT
    n_blocks = -(-(t * TOP_K + n_tiles * N_EXPERTS * (SEG - 1) + N_EXPERTS * (bm - 1)) // bm)
    n_used = (pad_end[-1] // bm).astype(I32)
    blk_row = jnp.minimum(jnp.arange(n_blocks + spare_blocks, dtype=I32), n_used - 1) * bm
    onehot_e = (pad_end[None, :] <= blk_row[:, None]).astype(I32)
    block_e = jnp.minimum(jnp.sum(onehot_e, axis=1), N_EXPERTS - 1).astype(I32)
    n_used = n_used.reshape(1)

    group_row = jnp.arange(R_LOCAL // SEG, dtype=I32) * SEG
    owner = jnp.sum(((tl + tn)[:, None, :] <= group_row[None, :, None]).astype(I32), axis=-1)
    is_owner = (owner[:, :, None] == jnp.arange(N_EXPERTS, dtype=I32)[None, None, :]).astype(I32)
    base = jnp.sum(is_owner * (pad_start[None, :] + tc - tl)[:, None, :], axis=-1)
    used = owner < N_EXPERTS
    pad_table = lambda a: jnp.pad(a, ((0, 0), (0, ROW_TABLE - a.shape[1]))).reshape(-1)
    dst_table = pad_table(jnp.where(used, base + group_row[None, :], n_blocks * bm + group_row[None, :]))
    src_table = pad_table(jnp.where(used, base + group_row[None, :], 0))

    xs = _dispatch(pad_start, counts, n_used, dst_table, h2, mt, (n_blocks + spare_blocks) * bm, bm)
    ys = _experts(block_e, n_used, xs, p, bm)
    y = _combine(src_table, mt, x1, ys, p)
    return y.reshape(batch, seq_len, D_MODEL)


def kernel(x_prompt, x_sample, norm1_g, w_in, conv_w, conv_b, dt_bias_f, dt_bias_b, a_log_f, a_log_b,
           d_skip, ssd_norm_g, gmlp_ln_g, gmlp_ln_b, w_spatial, b_spatial, w_out, norm2_g, router_w,
           router_b, w_gate_up, b_gate_up, w_down, b_down, final_g):
    p = _prepare_params(norm1_g, w_in, conv_w, conv_b, dt_bias_f, dt_bias_b, a_log_f, a_log_b, d_skip,
                        ssd_norm_g, gmlp_ln_g, gmlp_ln_b, w_spatial, b_spatial, w_out, norm2_g,
                        router_w, router_b, w_gate_up, b_gate_up, w_down, b_down, final_g)
    return (_encoder(x_prompt, p), _encoder(x_sample, p))
```
